```python
import math
import jax, jax.numpy as jnp
from jax import lax
import numpy as np

D_MODEL = 1024
BATCH = 8
SEQ = 8192
DEPTH = 1
DEC_BATCH = 32
DEC_SEQ = 2048
PAST_LEN = 128

D_MIX = D_MODEL
D_FOURIER = D_MIX // 4
N_FOURIER_GROUPS = 4
FOURIER_GROUP_DIM = D_FOURIER // N_FOURIER_GROUPS
D_ATTN = D_MIX - D_FOURIER
HEAD_DIM = 64
N_HEADS = D_ATTN // HEAD_DIM
D_PROJ = D_FOURIER + 3 * D_ATTN
D_FF = 2816
WINDOW_DILATIONS = ((128, 1), (512, 4), (2048, 16))
RADIUS = 64
ROPE_THETA = 10000.0
RMS_EPS = 1e-6
MASK_VALUE = -1e30

kernel_name = "hybrid_fnet_dilated_attn_macaron_encoder"


def rms_norm(x, g):
    xf = x.astype(jnp.float32)
    inv = lax.rsqrt(jnp.mean(xf * xf, axis=-1, keepdims=True) + RMS_EPS)
    return (xf * inv * g.astype(jnp.float32)).astype(x.dtype)


def swiglu(h, w_gate, w_up, w_down):
    return (jax.nn.silu(h @ w_gate) * (h @ w_up)) @ w_down


def apply_rope(t):
    S = t.shape[1]
    inv_freq = ROPE_THETA ** (-jnp.arange(0, HEAD_DIM, 2, dtype=jnp.float32) / HEAD_DIM)
    freqs = jnp.arange(S, dtype=jnp.float32)[:, None] * inv_freq[None, :]
    emb = jnp.concatenate([freqs, freqs], axis=-1)[:, None, :]
    cos, sin = jnp.cos(emb), jnp.sin(emb)
    tf = t.astype(jnp.float32)
    t1, t2 = tf[..., : HEAD_DIM // 2], tf[..., HEAD_DIM // 2:]
    rot = jnp.concatenate([-t2, t1], axis=-1)
    return (tf * cos + rot * sin).astype(t.dtype)


def fourier_mix(u, w_f):
    B, S, _ = u.shape
    ug = u.reshape(B, S, N_FOURIER_GROUPS, FOURIER_GROUP_DIM).astype(jnp.float32)
    f = jnp.fft.fft2(ug, axes=(1, 3), norm="ortho").real.astype(u.dtype)
    return jnp.einsum('bsgc,gce->bsge', f, w_f).reshape(B, S, D_FOURIER)


def dilated_window_attention(q, k, v, dilation):
    B, S, H, Dh = q.shape
    d = dilation
    Lm = S // d
    nb = -(-Lm // RADIUS)
    Lp = nb * RADIUS
    pad = Lp - Lm
    qs = jnp.pad(q.reshape(B, Lm, d, H, Dh), ((0, 0), (0, pad), (0, 0), (0, 0), (0, 0)))
    kv_pad = ((0, 0), (RADIUS, pad + RADIUS), (0, 0), (0, 0), (0, 0))
    ks = jnp.pad(k.reshape(B, Lm, d, H, Dh), kv_pad)
    vs = jnp.pad(v.reshape(B, Lm, d, H, Dh), kv_pad)
    qb = qs.reshape(B, nb, RADIUS, d, H, Dh)
    kb = ks.reshape(B, nb + 2, RADIUS, d, H, Dh)
    vb = vs.reshape(B, nb + 2, RADIUS, d, H, Dh)
    kwin = jnp.concatenate([kb[:, :-2], kb[:, 1:-1], kb[:, 2:]], axis=2)
    vwin = jnp.concatenate([vb[:, :-2], vb[:, 1:-1], vb[:, 2:]], axis=2)
    logits = jnp.einsum('bnqrhd,bnkrhd->bnrhqk', qb, kwin).astype(jnp.float32) * (HEAD_DIM ** -0.5)
    qi = jnp.arange(RADIUS)[:, None]
    ki = jnp.arange(3 * RADIUS)[None, :]
    in_window = jnp.abs(ki - RADIUS - qi) <= RADIUS
    key_pos = jnp.arange(nb)[:, None] * RADIUS - RADIUS + jnp.arange(3 * RADIUS)[None, :]
    in_range = (key_pos >= 0) & (key_pos < Lm)
    valid = in_window[None, :, :] & in_range[:, None, :]
    logits = jnp.where(valid[None, :, None, None, :, :], logits, MASK_VALUE)
    lse = jax.nn.logsumexp(logits, axis=-1)
    p = jnp.exp(logits - lse[..., None]).astype(v.dtype)
    out = jnp.einsum('bnrhqk,bnkrhd->bnqrhd', p, vwin)
    out = out.reshape(B, Lp, d, H, Dh)[:, :Lm].reshape(B, S, H, Dh)
    lse = jnp.transpose(lse, (0, 1, 4, 2, 3)).reshape(B, Lp, d, H)[:, :Lm].reshape(B, S, H)
    return out, lse


def dilated_mixture_attention(q, k, v):
    outs, lses = [], []
    for window, dilation in WINDOW_DILATIONS:
        o, l = dilated_window_attention(q, k, v, dilation)
        outs.append(o.astype(jnp.float32))
        lses.append(l)
    w = jax.nn.softmax(jnp.stack(lses, axis=0), axis=0)
    out = jnp.sum(w[..., None] * jnp.stack(outs, axis=0), axis=0)
    B, S = q.shape[:2]
    return out.astype(q.dtype).reshape(B, S, D_ATTN)


def encoder_layer(x, ffn1_norm, ffn1_w_gate, ffn1_w_up, ffn1_w_down,
                  mix_norm, w_in, fourier_w, fourier_out_norm, attn_out_norm, w_out,
                  ffn2_norm, ffn2_w_gate, ffn2_w_up, ffn2_w_down):
    B, S, _ = x.shape
    x = x + 0.5 * swiglu(rms_norm(x, ffn1_norm), ffn1_w_gate, ffn1_w_up, ffn1_w_down)
    h = rms_norm(x, mix_norm)
    proj = h @ w_in
    u_f = proj[..., :D_FOURIER]
    q = proj[..., D_FOURIER:D_FOURIER + D_ATTN].reshape(B, S, N_HEADS, HEAD_DIM)
    k = proj[..., D_FOURIER + D_ATTN:D_FOURIER + 2 * D_ATTN].reshape(B, S, N_HEADS, HEAD_DIM)
    v = proj[..., D_FOURIER + 2 * D_ATTN:].reshape(B, S, N_HEADS, HEAD_DIM)
    q, k = apply_rope(q), apply_rope(k)
    o_f = fourier_mix(u_f, fourier_w)
    o_a = dilated_mixture_attention(q, k, v)
    mix = jnp.concatenate([rms_norm(o_f, fourier_out_norm), rms_norm(o_a, attn_out_norm)], axis=-1)
    x = x + mix @ w_out
    x = x + 0.5 * swiglu(rms_norm(x, ffn2_norm), ffn2_w_gate, ffn2_w_up, ffn2_w_down)
    return x


def setup_inputs(seed: int = 0) -> dict:
    key = jax.random.key(seed)
    ks = jax.random.split(key, 20)
    f32 = jnp.float32

    def w(k, shape, fan_in):
        return jax.random.normal(k, shape, f32) * (fan_in ** -0.5)

    def gain(k, shape):
        return 1.0 + 0.02 * jax.random.normal(k, shape, f32)

    return {
        "x_prompt": jax.random.normal(ks[0], (BATCH, SEQ, D_MODEL), f32),
        "x_sample": jax.random.normal(ks[1], (DEC_BATCH, DEC_SEQ, D_MODEL), f32),
        "ffn1_norm": gain(ks[2], (DEPTH, D_MODEL)),
        "ffn1_w_gate": w(ks[3], (DEPTH, D_MODEL, D_FF), D_MODEL),
        "ffn1_w_up": w(ks[4], (DEPTH, D_MODEL, D_FF), D_MODEL),
        "ffn1_w_down": w(ks[5], (DEPTH, D_FF, D_MODEL), D_FF),
        "mix_norm": gain(ks[6], (DEPTH, D_MODEL)),
        "w_in": w(ks[7], (DEPTH, D_MODEL, D_PROJ), D_MODEL),
        "fourier_w": w(ks[8], (DEPTH, N_FOURIER_GROUPS, FOURIER_GROUP_DIM, FOURIER_GROUP_DIM), FOURIER_GROUP_DIM),
        "fourier_out_norm": gain(ks[9], (DEPTH, D_FOURIER)),
        "attn_out_norm": gain(ks[10], (DEPTH, D_ATTN)),
        "w_out": w(ks[11], (DEPTH, D_MIX, D_MODEL), D_MIX),
        "ffn2_norm": gain(ks[12], (DEPTH, D_MODEL)),
        "ffn2_w_gate": w(ks[13], (DEPTH, D_MODEL, D_FF), D_MODEL),
        "ffn2_w_up": w(ks[14], (DEPTH, D_MODEL, D_FF), D_MODEL),
        "ffn2_w_down": w(ks[15], (DEPTH, D_FF, D_MODEL), D_FF),
        "final_norm": gain(ks[16], (D_MODEL,)),
    }


def trunk(x, ffn1_norm, ffn1_w_gate, ffn1_w_up, ffn1_w_down, mix_norm, w_in, fourier_w,
          fourier_out_norm, attn_out_norm, w_out, ffn2_norm, ffn2_w_gate, ffn2_w_up, ffn2_w_down,
          final_norm):
    for l in range(DEPTH):
        x = encoder_layer(x, ffn1_norm[l], ffn1_w_gate[l], ffn1_w_up[l], ffn1_w_down[l],
                          mix_norm[l], w_in[l], fourier_w[l], fourier_out_norm[l], attn_out_norm[l],
                          w_out[l], ffn2_norm[l], ffn2_w_gate[l], ffn2_w_up[l], ffn2_w_down[l])
    return rms_norm(x, final_norm)


def reference(x_prompt, x_sample, ffn1_norm, ffn1_w_gate, ffn1_w_up, ffn1_w_down, mix_norm, w_in,
              fourier_w, fourier_out_norm, attn_out_norm, w_out, ffn2_norm, ffn2_w_gate, ffn2_w_up,
              ffn2_w_down, final_norm):
    y_prompt = trunk(x_prompt, ffn1_norm, ffn1_w_gate, ffn1_w_up, ffn1_w_down, mix_norm, w_in,
                     fourier_w, fourier_out_norm, attn_out_norm, w_out, ffn2_norm, ffn2_w_gate,
                     ffn2_w_up, ffn2_w_down, final_norm)
    y_sample = trunk(x_sample, ffn1_norm, ffn1_w_gate, ffn1_w_up, ffn1_w_down, mix_norm, w_in,
                     fourier_w, fourier_out_norm, attn_out_norm, w_out, ffn2_norm, ffn2_w_gate,
                     ffn2_w_up, ffn2_w_down, final_norm)
    return (y_prompt, y_sample)
```

```python
import functools
import math

import jax
import jax.numpy as jnp
import numpy as np
from jax import lax
from jax.experimental import pallas as pl
from jax.experimental.pallas import tpu as pltpu

F32 = jnp.float32
BF16 = jnp.bfloat16

D_MODEL = 1024
D_FF = 2816
D_FOURIER = 256
N_FOURIER_GROUPS = 4
FOURIER_GROUP_DIM = 64
D_ATTN = 768
HEAD_DIM = 64
LANES = 128
N_HEAD_PAIRS = D_ATTN // LANES
D_PROJ = 2 * D_FOURIER + 3 * D_ATTN
DILATIONS = (1, 4, 16)
RADIUS = 64
ROPE_THETA = 10000.0
RMS_EPS = 1e-6
MASK_VALUE = -1e30

FF_CHUNK = 256
Q_BLOCK = 128
VMEM_LIMIT = 56 * 1024 * 1024


def _params(n_grid_dims):
    return pltpu.CompilerParams(dimension_semantics=("arbitrary",) * n_grid_dims,
                                vmem_limit_bytes=VMEM_LIMIT)


def _resident(shape):
    nd = len(shape)
    return pl.BlockSpec(shape, lambda *_: (0,) * nd, pipeline_mode=pl.Buffered(1))


def _rms(x, g):
    inv = lax.rsqrt(jnp.mean(x * x, axis=-1, keepdims=True) + RMS_EPS)
    return x * inv * g


def _dot(a, b):
    return jnp.dot(a, b, preferred_element_type=F32)


def _swiglu(h, wg_ref, wu_ref, wd_ref):
    acc = None
    for c in range(D_FF // FF_CHUNK):
        sl = slice(c * FF_CHUNK, (c + 1) * FF_CHUNK)
        g = _dot(h, wg_ref[:, sl])
        u = _dot(h, wu_ref[:, sl])
        a = (g * jax.nn.sigmoid(g) * u).astype(BF16)
        part = _dot(a, wd_ref[sl, :])
        acc = part if acc is None else acc + part
    return acc


def _fold_kernel(winf_ref, cbd_ref, sbd_ref, wbd_ref, o_ref):
    hi = lax.Precision.HIGHEST
    scale = FOURIER_GROUP_DIM ** -0.5
    gc = jnp.dot(cbd_ref[...], wbd_ref[...], precision=hi, preferred_element_type=F32) * scale
    gs = jnp.dot(sbd_ref[...], wbd_ref[...], precision=hi, preferred_element_type=F32) * scale
    winf = winf_ref[...]
    o_ref[:, :D_FOURIER] = jnp.dot(winf, gc, precision=hi, preferred_element_type=F32).astype(BF16)
    o_ref[:, D_FOURIER:] = jnp.dot(winf, gs, precision=hi, preferred_element_type=F32).astype(BF16)


def _fold_fourier_weights(w_in_f, fourier_w):
    c = np.arange(FOURIER_GROUP_DIM)
    ang = 2.0 * np.pi * ((c[:, None] * c[None, :]) % FOURIER_GROUP_DIM) / FOURIER_GROUP_DIM
    eye = np.eye(N_FOURIER_GROUPS)
    cbd = jnp.asarray(np.kron(eye, np.cos(ang)), F32)
    sbd = jnp.asarray(np.kron(eye, np.sin(ang)), F32)
    wbd = jnp.zeros((D_FOURIER, D_FOURIER), F32)
    for g in range(N_FOURIER_GROUPS):
        s = g * FOURIER_GROUP_DIM
        wbd = lax.dynamic_update_slice(wbd, fourier_w[g].astype(F32), (s, s))
    return pl.pallas_call(
        _fold_kernel,
        out_shape=jax.ShapeDtypeStruct((D_MODEL, 2 * D_FOURIER), BF16),
        name="fold_fourier",
    )(w_in_f, cbd, sbd, wbd)


def _rope(t, cos, sin_lo, sin_hi):
    return (t * cos + pltpu.roll(t, LANES - HEAD_DIM // 2, axis=1) * sin_lo
            + pltpu.roll(t, HEAD_DIM // 2, axis=1) * sin_hi)


def _ffn_proj_kernel(x_ref, g1_ref, wg_ref, wu_ref, wd_ref, gm_ref, win_ref,
                     cos_ref, slo_ref, shi_ref,
                     x1_ref, q_ref, k_ref, v_ref, a_ref):
    x = x_ref[...]
    h = _rms(x, g1_ref[...]).astype(BF16)
    x1 = x + 0.5 * _swiglu(h, wg_ref, wu_ref, wd_ref)
    x1_ref[...] = x1
    h2 = _rms(x1, gm_ref[...]).astype(BF16)

    pa = _dot(h2, win_ref[:, :2 * D_FOURIER])
    a_ref[0] = pa[:, :D_FOURIER].astype(BF16)
    a_ref[1] = pa[:, D_FOURIER:].astype(BF16)

    cos = cos_ref[...]
    slo = slo_ref[...]
    shi = shi_ref[...]
    base = 2 * D_FOURIER
    for j in range(D_ATTN // 256):
        c0 = j * 256
        qq = _dot(h2, win_ref[:, base + c0: base + c0 + 256])
        kk = _dot(h2, win_ref[:, base + D_ATTN + c0: base + D_ATTN + c0 + 256])
        vv = _dot(h2, win_ref[:, base + 2 * D_ATTN + c0: base + 2 * D_ATTN + c0 + 256])
        for i in range(2):
            sl = slice(i * LANES, (i + 1) * LANES)
            hp = 2 * j + i
            q_ref[hp] = (_rope(qq[:, sl], cos, slo, shi) * (HEAD_DIM ** -0.5)).astype(BF16)
            k_ref[hp] = _rope(kk[:, sl], cos, slo, shi).astype(BF16)
            v_ref[hp] = vv[:, sl].astype(BF16)


def _rope_tables(S):
    inv_freq = ROPE_THETA ** (-jnp.arange(0, HEAD_DIM, 2, dtype=F32) / HEAD_DIM)
    freqs = jnp.arange(S, dtype=F32)[:, None] * inv_freq[None, :]
    cos, sin = jnp.cos(freqs), jnp.sin(freqs)
    zero = jnp.zeros_like(sin)
    cos_t = jnp.tile(cos, (1, LANES // (HEAD_DIM // 2)))
    sin_lo = jnp.tile(jnp.concatenate([-sin, zero], axis=1), (1, LANES // HEAD_DIM))
    sin_hi = jnp.tile(jnp.concatenate([zero, sin], axis=1), (1, LANES // HEAD_DIM))
    return cos_t, sin_lo, sin_hi


def _ffn_proj(xf, B, S, g1, wg, wu, wd, gm, win, tm):
    N = B * S
    nst = S // tm
    cos_t, sin_lo, sin_hi = _rope_tables(S)
    tok = pl.BlockSpec((tm, D_MODEL), lambda i: (i, 0))
    tab = pl.BlockSpec((tm, LANES), lambda i: (i % nst, 0))
    qkv = pl.BlockSpec((None, N_HEAD_PAIRS, tm, LANES), lambda i: (i // nst, 0, i % nst, 0))
    qkv_shape = jax.ShapeDtypeStruct((B, N_HEAD_PAIRS, S, LANES), BF16)
    return pl.pallas_call(
        _ffn_proj_kernel,
        grid=(N // tm,),
        in_specs=[tok, _resident((1, D_MODEL)), _resident((D_MODEL, D_FF)), _resident((D_MODEL, D_FF)),
                  _resident((D_FF, D_MODEL)), _resident((1, D_MODEL)), _resident((D_MODEL, D_PROJ)),
                  tab, tab, tab],
        out_specs=[tok, qkv, qkv, qkv,
                   pl.BlockSpec((2, tm, D_FOURIER), lambda i: (0, i % nst, i // nst))],
        out_shape=[jax.ShapeDtypeStruct((N, D_MODEL), F32), qkv_shape, qkv_shape, qkv_shape,
                   jax.ShapeDtypeStruct((2, S, B * D_FOURIER), BF16)],
        compiler_params=_params(1),
        name="ffn_proj",
    )(xf, g1, wg, wu, wd, gm, win, cos_t, sin_lo, sin_hi)


def _band_bias(W):
    qi = np.arange(Q_BLOCK)[:, None]
    kj = np.arange(W)[None, :]
    tiles = [np.where(np.abs(kj + off - qi) <= RADIUS, 0.0, MASK_VALUE) for off in (0, -RADIUS, -2 * RADIUS)]
    return jnp.asarray(np.stack(tiles), F32)


def _attn_kernel(q_ref, k_ref, v_ref, bias_ref, o_ref, lse_ref, *, d, L):
    W = min(2 * Q_BLOCK, L)
    nblk = L // Q_BLOCK
    lane = lax.broadcasted_iota(jnp.int32, (1, LANES), 1)
    first_head = lane < HEAD_DIM
    contract_last = (((1,), (1,)), ((), ()))

    for r in range(d):
        cls = slice(r * LANES, (r + 1) * LANES)

        def block(i, carry, cls=cls):
            a = pl.multiple_of(i * Q_BLOCK, Q_BLOCK)
            ws = pl.multiple_of(jnp.clip(a - RADIUS, 0, L - W), RADIUS)
            typ = jnp.where(i == 0, 0, jnp.where(i == nblk - 1, 2, 1))
            q2 = q_ref[pl.ds(a, Q_BLOCK), cls]
            k2 = k_ref[pl.ds(ws, W), cls]
            v2 = v_ref[pl.ds(ws, W), cls]
            bias = bias_ref[typ]
            outs, lses = [], []
            for head_mask in (first_head, jnp.logical_not(first_head)):
                qh = jnp.where(head_mask, q2, jnp.zeros_like(q2))
                s = lax.dot_general(qh, k2, contract_last, preferred_element_type=F32) + bias
                m = jnp.max(s, axis=-1, keepdims=True)
                p = jnp.exp(s - m)
                l = jnp.sum(p, axis=-1, keepdims=True)
                pv = _dot(p.astype(BF16), v2)
                outs.append(pv / l)
                lses.append(jnp.broadcast_to(m + jnp.log(l), (Q_BLOCK, LANES)))
            o_ref[pl.ds(a, Q_BLOCK), cls] = jnp.where(first_head, outs[0], outs[1])
            lse_ref[pl.ds(a, Q_BLOCK), cls] = jnp.where(first_head, lses[0], lses[1])
            return carry

        lax.fori_loop(0, nblk, block, 0)


def _attention(q, k, v, d):
    B, _, S, _ = q.shape
    L = S // d
    assert L % Q_BLOCK == 0 and (L == Q_BLOCK or L >= 2 * Q_BLOCK)
    W = min(2 * Q_BLOCK, L)
    view = (B, N_HEAD_PAIRS, L, d * LANES)
    spec = pl.BlockSpec((None, None, L, d * LANES), lambda b, h: (b, h, 0, 0))
    o, lse = pl.pallas_call(
        functools.partial(_attn_kernel, d=d, L=L),
        grid=(B, N_HEAD_PAIRS),
        in_specs=[spec, spec, spec, _resident((3, Q_BLOCK, W))],
        out_specs=[spec, spec],
        out_shape=[jax.ShapeDtypeStruct(view, F32), jax.ShapeDtypeStruct(view, F32)],
        compiler_params=_params(2),
        name=f"attn_d{d}",
    )(q.reshape(view), k.reshape(view), v.reshape(view), _band_bias(W))
    full = (B, N_HEAD_PAIRS, S, LANES)
    return o.reshape(full), lse.reshape(full)


def _dft_kernel(bc_ref, nbs_ref, r1_ref, r2_ref, a_ref, o_ref):
    @pl.when(pl.program_id(2) == 0)
    def _():
        o_ref[...] = jnp.zeros_like(o_ref)

    bc, nbs = bc_ref[...], nbs_ref[...]
    r1, r2 = r1_ref[...], r2_ref[...]
    t_cos = (bc * r1 + nbs * r2).astype(BF16)
    t_nsin = (nbs * r1 - bc * r2).astype(BF16)
    o_ref[...] += _dot(t_cos, a_ref[0]) + _dot(t_nsin, a_ref[1])


def _seq_dft(a, S, ts, tk, tn):
    ncols = a.shape[-1]
    t = jnp.arange(S, dtype=jnp.int32)[None, :]
    w = 2.0 * math.pi / S
    ang_b = ((jnp.arange(ts, dtype=jnp.int32)[:, None] * t) % S).astype(F32) * w
    ang_r = ((jnp.arange(0, S, ts, dtype=jnp.int32)[:, None] * t) % S).astype(F32) * w
    bc, nbs = jnp.cos(ang_b), -jnp.sin(ang_b)
    scale = S ** -0.5
    r1 = (jnp.cos(ang_r) * scale).reshape(S // ts, 1, S)
    r2 = (jnp.sin(ang_r) * scale).reshape(S // ts, 1, S)
    btab = pl.BlockSpec((ts, tk), lambda s, n, t: (0, t))
    rtab = pl.BlockSpec((None, 1, tk), lambda s, n, t: (s, 0, t))
    return pl.pallas_call(
        _dft_kernel,
        grid=(S // ts, ncols // tn, S // tk),
        in_specs=[btab, btab, rtab, rtab, pl.BlockSpec((2, tk, tn), lambda s, n, t: (0, t, n))],
        out_specs=pl.BlockSpec((ts, tn), lambda s, n, t: (s, n)),
        out_shape=jax.ShapeDtypeStruct((S, ncols), F32),
        compiler_params=_params(3),
        name="seq_dft",
    )(bc, nbs, r1, r2, a)


def _out_ffn_kernel(x1_ref, of_ref, o1_ref, l1_ref, o4_ref, l4_ref, o16_ref, l16_ref,
                    gf_ref, ga_ref, wout_ref, g2_ref, wg_ref, wu_ref, wd_ref, gfin_ref, y_ref):
    merged = []
    for hp in range(N_HEAD_PAIRS):
        l1, l4, l16 = l1_ref[hp], l4_ref[hp], l16_ref[hp]
        m = jnp.maximum(jnp.maximum(l1, l4), l16)
        w1, w4, w16 = jnp.exp(l1 - m), jnp.exp(l4 - m), jnp.exp(l16 - m)
        merged.append((w1 * o1_ref[hp] + w4 * o4_ref[hp] + w16 * o16_ref[hp]) / (w1 + w4 + w16))
    oa = jnp.concatenate(merged, axis=1)
    na = _rms(oa, ga_ref[...]).astype(BF16)
    nf = _rms(of_ref[...], gf_ref[...]).astype(BF16)
    x2 = x1_ref[...] + _dot(nf, wout_ref[:D_FOURIER, :]) + _dot(na, wout_ref[D_FOURIER:, :])
    h = _rms(x2, g2_ref[...]).astype(BF16)
    x3 = x2 + 0.5 * _swiglu(h, wg_ref, wu_ref, wd_ref)
    y_ref[...] = _rms(x3, gfin_ref[...])


def _out_ffn(x1, of, attn, B, S, gf, ga, wout, g2, wg, wu, wd, gfin, tm):
    N = B * S
    nst = S // tm
    tok = pl.BlockSpec((tm, D_MODEL), lambda i: (i, 0))
    att = pl.BlockSpec((None, N_HEAD_PAIRS, tm, LANES), lambda i: (i // nst, 0, i % nst, 0))
    return pl.pallas_call(
        _out_ffn_kernel,
        grid=(N // tm,),
        in_specs=[tok, pl.BlockSpec((tm, D_FOURIER), lambda i: (i % nst, i // nst))] + [att] * 6 + [
            _resident((1, D_FOURIER)), _resident((1, D_ATTN)), _resident((D_MODEL, D_MODEL)),
            _resident((1, D_MODEL)), _resident((D_MODEL, D_FF)), _resident((D_MODEL, D_FF)),
            _resident((D_FF, D_MODEL)), _resident((1, D_MODEL))],
        out_specs=tok,
        out_shape=jax.ShapeDtypeStruct((N, D_MODEL), F32),
        compiler_params=_params(1),
        name="out_ffn",
    )(x1, of, *attn, gf, ga, wout, g2, wg, wu, wd, gfin)


def _row(g):
    return g.reshape(1, -1).astype(F32)


def _trunk(x, p, win):
    B, S, _ = x.shape
    x1, q, k, v, a = _ffn_proj(x.reshape(B * S, D_MODEL), B, S, _row(p["ffn1_norm"]), p["ffn1_w_gate"],
                               p["ffn1_w_up"], p["ffn1_w_down"], _row(p["mix_norm"]), win, tm=512)
    attn = []
    for d in DILATIONS:
        attn.extend(_attention(q, k, v, d))
    of = _seq_dft(a, S, ts=512, tk=512, tn=min(2048, a.shape[-1]))
    y = _out_ffn(x1, of, attn, B, S, _row(p["fourier_out_norm"]), _row(p["attn_out_norm"]), p["w_out"],
                 _row(p["ffn2_norm"]), p["ffn2_w_gate"], p["ffn2_w_up"], p["ffn2_w_down"],
                 _row(p["final_norm"]), tm=256)
    return y.reshape(B, S, D_MODEL)


def kernel(x_prompt, x_sample, ffn1_norm, ffn1_w_gate, ffn1_w_up, ffn1_w_down, mix_norm, w_in, fourier_w,
           fourier_out_norm, attn_out_norm, w_out, ffn2_norm, ffn2_w_gate, ffn2_w_up, ffn2_w_down, final_norm):
    assert ffn1_w_gate.shape[0] == 1, "single-layer trunk"
    folded = _fold_fourier_weights(w_in[0, :, :D_FOURIER], fourier_w[0])
    win = jnp.concatenate([folded, w_in[0, :, D_FOURIER:].astype(BF16)], axis=1)
    p = dict(
        ffn1_norm=ffn1_norm[0], ffn1_w_gate=ffn1_w_gate[0].astype(BF16), ffn1_w_up=ffn1_w_up[0].astype(BF16),
        ffn1_w_down=ffn1_w_down[0].astype(BF16), mix_norm=mix_norm[0],
        fourier_out_norm=fourier_out_norm[0], attn_out_norm=attn_out_norm[0], w_out=w_out[0].astype(BF16),
        ffn2_norm=ffn2_norm[0], ffn2_w_gate=ffn2_w_gate[0].astype(BF16), ffn2_w_up=ffn2_w_up[0].astype(BF16),
        ffn2_w_down=ffn2_w_down[0].astype(BF16), final_norm=final_norm)
    return (_trunk(x_prompt, p, win), _trunk(x_sample, p, win))
```

```python
import functools
import math

import jax
import jax.numpy as jnp
import numpy as np
from jax import lax
from jax.experimental import pallas as pl
from jax.experimental.pallas import tpu as pltpu

F32 = jnp.float32
BF16 = jnp.bfloat16

D_MODEL = 1024
D_FF = 2816
D_FOURIER = 256
N_FOURIER_GROUPS = 4
FOURIER_GROUP_DIM = 64
D_ATTN = 768
HEAD_DIM = 64
LANES = 128
N_HEAD_PAIRS = D_ATTN // LANES
D_PROJ = 2 * D_FOURIER + 3 * D_ATTN
RADIUS = 64
ROPE_THETA = 10000.0
RMS_EPS = 1e-6
MASK_VALUE = -1e30

FF_CHUNK = 256
Q_BLOCK = 128
KEY_WINDOW = Q_BLOCK + 2 * RADIUS
ATTN_UNROLL = 4
VMEM_LIMIT = 56 * 1024 * 1024


def _params(n_grid_dims):
    return pltpu.CompilerParams(dimension_semantics=("arbitrary",) * n_grid_dims,
                                vmem_limit_bytes=VMEM_LIMIT)


def _resident(shape):
    nd = len(shape)
    return pl.BlockSpec(shape, lambda *_: (0,) * nd, pipeline_mode=pl.Buffered(1))


def _rms(x, g):
    inv = lax.rsqrt(jnp.mean(x * x, axis=-1, keepdims=True) + RMS_EPS)
    return x * inv * g


def _dot(a, b):
    return jnp.dot(a, b, preferred_element_type=F32)


def _swiglu(h, wg_ref, wu_ref, wd_ref):
    acc = None
    for c in range(D_FF // FF_CHUNK):
        sl = slice(c * FF_CHUNK, (c + 1) * FF_CHUNK)
        g = _dot(h, wg_ref[:, sl])
        u = _dot(h, wu_ref[:, sl])
        a = (g * jax.nn.sigmoid(g) * u).astype(BF16)
        part = _dot(a, wd_ref[sl, :])
        acc = part if acc is None else acc + part
    return acc


def _fold_kernel(winf_ref, cbd_ref, sbd_ref, wbd_ref, o_ref):
    hi = lax.Precision.HIGHEST
    scale = FOURIER_GROUP_DIM ** -0.5
    gc = jnp.dot(cbd_ref[...], wbd_ref[...], precision=hi, preferred_element_type=F32) * scale
    gs = jnp.dot(sbd_ref[...], wbd_ref[...], precision=hi, preferred_element_type=F32) * scale
    winf = winf_ref[...]
    o_ref[:, :D_FOURIER] = jnp.dot(winf, gc, precision=hi, preferred_element_type=F32).astype(BF16)
    o_ref[:, D_FOURIER:] = jnp.dot(winf, gs, precision=hi, preferred_element_type=F32).astype(BF16)


def _fold_fourier_weights(w_in_f, fourier_w):
    c = np.arange(FOURIER_GROUP_DIM)
    ang = 2.0 * np.pi * ((c[:, None] * c[None, :]) % FOURIER_GROUP_DIM) / FOURIER_GROUP_DIM
    eye = np.eye(N_FOURIER_GROUPS)
    cbd = jnp.asarray(np.kron(eye, np.cos(ang)), F32)
    sbd = jnp.asarray(np.kron(eye, np.sin(ang)), F32)
    wbd = jnp.zeros((D_FOURIER, D_FOURIER), F32)
    for g in range(N_FOURIER_GROUPS):
        s = g * FOURIER_GROUP_DIM
        wbd = lax.dynamic_update_slice(wbd, fourier_w[g].astype(F32), (s, s))
    return pl.pallas_call(
        _fold_kernel,
        out_shape=jax.ShapeDtypeStruct((D_MODEL, 2 * D_FOURIER), BF16),
        name="fold_fourier",
    )(w_in_f, cbd, sbd, wbd)


def _ffn1_kernel(x_ref, g1_ref, wg_ref, wu_ref, wd_ref, x1_ref):
    x = x_ref[...]
    h = _rms(x, g1_ref[...]).astype(BF16)
    x1_ref[...] = x + 0.5 * _swiglu(h, wg_ref, wu_ref, wd_ref)


def _ffn1(xf, g1, wg, wu, wd, tm):
    N = xf.shape[0]
    tok = pl.BlockSpec((tm, D_MODEL), lambda i: (i, 0))
    return pl.pallas_call(
        _ffn1_kernel,
        grid=(N // tm,),
        in_specs=[tok, _resident((1, D_MODEL)), _resident((D_MODEL, D_FF)), _resident((D_MODEL, D_FF)),
                  _resident((D_FF, D_MODEL))],
        out_specs=tok,
        out_shape=jax.ShapeDtypeStruct((N, D_MODEL), F32),
        compiler_params=_params(1),
        name="ffn1",
    )(xf, g1, wg, wu, wd)


def _rope(t, cos, sin_lo, sin_hi):
    return (t * cos + pltpu.roll(t, LANES - HEAD_DIM // 2, axis=1) * sin_lo
            + pltpu.roll(t, HEAD_DIM // 2, axis=1) * sin_hi)


def _proj_kernel(x1_ref, gm_ref, win_ref, cos_ref, slo_ref, shi_ref,
                 a_ref, q1_ref, k1_ref, v1_ref, q4_ref, k4_ref, v4_ref, q16_ref, k16_ref, v16_ref,
                 nat_ref, cls4_ref):
    tm = x1_ref.shape[0]
    h2 = _rms(x1_ref[...], gm_ref[...]).astype(BF16)

    pa = _dot(h2, win_ref[:, :2 * D_FOURIER])
    a_ref[0] = pa[:, :D_FOURIER].astype(BF16)
    a_ref[1] = pa[:, D_FOURIER:].astype(BF16)

    def emit(t, hp, o1_ref, o4_ref, o16_ref):
        o1_ref[hp] = t.astype(BF16)
        nat_ref[...] = t
        for r4 in range(4):
            c4 = nat_ref[pl.ds(r4, tm // 4, stride=4), :]
            o4_ref[hp, r4] = c4.astype(BF16)
            cls4_ref[...] = c4
            for j in range(4):
                o16_ref[hp, r4 + 4 * j] = cls4_ref[pl.ds(j, tm // 16, stride=4), :].astype(BF16)

    cos = cos_ref[...]
    slo = slo_ref[...]
    shi = shi_ref[...]
    base = 2 * D_FOURIER
    for j in range(D_ATTN // 256):
        c0 = j * 256
        qq = _dot(h2, win_ref[:, base + c0: base + c0 + 256])
        kk = _dot(h2, win_ref[:, base + D_ATTN + c0: base + D_ATTN + c0 + 256])
        vv = _dot(h2, win_ref[:, base + 2 * D_ATTN + c0: base + 2 * D_ATTN + c0 + 256])
        for i in range(2):
            sl = slice(i * LANES, (i + 1) * LANES)
            hp = 2 * j + i
            emit(_rope(qq[:, sl], cos, slo, shi) * (HEAD_DIM ** -0.5), hp, q1_ref, q4_ref, q16_ref)
            emit(_rope(kk[:, sl], cos, slo, shi), hp, k1_ref, k4_ref, k16_ref)
            emit(vv[:, sl], hp, v1_ref, v4_ref, v16_ref)


def _rope_tables(S):
    inv_freq = ROPE_THETA ** (-jnp.arange(0, HEAD_DIM, 2, dtype=F32) / HEAD_DIM)
    freqs = jnp.arange(S, dtype=F32)[:, None] * inv_freq[None, :]
    cos, sin = jnp.cos(freqs), jnp.sin(freqs)
    zero = jnp.zeros_like(sin)
    cos_t = jnp.tile(cos, (1, LANES // (HEAD_DIM // 2)))
    sin_lo = jnp.tile(jnp.concatenate([-sin, zero], axis=1), (1, LANES // HEAD_DIM))
    sin_hi = jnp.tile(jnp.concatenate([zero, sin], axis=1), (1, LANES // HEAD_DIM))
    return cos_t, sin_lo, sin_hi


def _proj(x1, B, S, gm, win, tm):
    N = B * S
    nst = S // tm
    cos_t, sin_lo, sin_hi = _rope_tables(S)
    tok = pl.BlockSpec((tm, D_MODEL), lambda i: (i, 0))
    tab = pl.BlockSpec((tm, LANES), lambda i: (i % nst, 0))
    nat = pl.BlockSpec((None, N_HEAD_PAIRS, tm, LANES), lambda i: (i // nst, 0, i % nst, 0))
    nat_shape = jax.ShapeDtypeStruct((B, N_HEAD_PAIRS, S, LANES), BF16)

    def cls(d):
        spec = pl.BlockSpec((None, N_HEAD_PAIRS, d, tm // d, LANES), lambda i: (i // nst, 0, 0, i % nst, 0))
        return spec, jax.ShapeDtypeStruct((B, N_HEAD_PAIRS, d, S // d, LANES), BF16)

    c4, c4_shape = cls(4)
    c16, c16_shape = cls(16)
    return pl.pallas_call(
        _proj_kernel,
        grid=(N // tm,),
        in_specs=[tok, _resident((1, D_MODEL)), _resident((D_MODEL, D_PROJ)), tab, tab, tab],
        out_specs=[pl.BlockSpec((2, tm, D_FOURIER), lambda i: (0, i % nst, i // nst)),
                   nat, nat, nat, c4, c4, c4, c16, c16, c16],
        out_shape=[jax.ShapeDtypeStruct((2, S, B * D_FOURIER), BF16),
                   nat_shape, nat_shape, nat_shape, c4_shape, c4_shape, c4_shape,
                   c16_shape, c16_shape, c16_shape],
        scratch_shapes=[pltpu.VMEM((tm, LANES), F32), pltpu.VMEM((tm // 4, LANES), F32)],
        compiler_params=_params(1),
        name="proj",
    )(x1, gm, win, cos_t, sin_lo, sin_hi)


def _band_bias(offsets, W, extra=None):
    qi = np.arange(Q_BLOCK)[:, None]
    kj = np.arange(W)[None, :]
    tiles = []
    for t, off in enumerate(offsets):
        ok = np.abs(kj + off - qi) <= RADIUS
        if extra is not None:
            ok &= extra[t](kj)
        tiles.append(np.where(ok, 0.0, MASK_VALUE))
    return jnp.asarray(np.stack(tiles), F32)


def _attn_kernel(q1_ref, k1_ref, k1p_ref, k1n_ref, v1_ref, v1p_ref, v1n_ref,
                 q4_ref, k4_ref, v4_ref, q16_ref, k16_ref, v16_ref,
                 b1_ref, b4_ref, b16_ref,
                 o_ref, m_ref, l_ref, kwin_ref, vwin_ref, *, S, C):
    phase = pl.program_id(2)
    c = pl.program_id(3)
    lane = lax.broadcasted_iota(jnp.int32, (1, LANES), 1)
    first_head = lane < HEAD_DIM
    contract_last = (((1,), (1,)), ((), ()))

    def block_stats(q2, k2, v2, bias):
        v_ones = jnp.concatenate([v2, jnp.ones_like(v2)], axis=1)
        per_head = []
        for head_mask in (first_head, jnp.logical_not(first_head)):
            qh = jnp.where(head_mask, q2, jnp.zeros_like(q2))
            s = lax.dot_general(qh, k2, contract_last, preferred_element_type=F32) + bias
            m = jnp.max(s, axis=-1, keepdims=True)
            p = jnp.exp(s - m).astype(BF16)
            pvl = _dot(p, v_ones)
            per_head.append((pvl[:, :LANES], jnp.broadcast_to(m, (Q_BLOCK, LANES)), pvl[:, LANES:]))
        return tuple(jnp.where(first_head, a, b) for a, b in zip(*per_head))

    def merge(rows, pv, m, l, mode):
        if mode == "first":
            o_ref[rows, :] = pv
            m_ref[rows, :] = m
            l_ref[rows, :] = l
            return
        m_old = m_ref[rows, :]
        m_new = jnp.maximum(m_old, m)
        a_old = jnp.exp(m_old - m_new)
        a_blk = jnp.exp(m - m_new)
        acc = a_old * o_ref[rows, :] + a_blk * pv
        den = a_old * l_ref[rows, :] + a_blk * l
        if mode == "last":
            o_ref[rows, :] = acc / den
        else:
            o_ref[rows, :] = acc
            m_ref[rows, :] = m_new
            l_ref[rows, :] = den

    def strided_phase(d, q_ref, k_ref, v_ref, bias_ref, mode):
        L = S // d
        nblk = L // Q_BLOCK
        W = min(KEY_WINDOW, L)
        cpc = d // C
        shift = nblk.bit_length() - 1

        def group(j, carry):
            for u in range(ATTN_UNROLL):
                g = j * ATTN_UNROLL + u
                rl = lax.shift_right_logical(g, shift)
                i = jnp.bitwise_and(g, nblk - 1)
                a = pl.multiple_of(i * Q_BLOCK, Q_BLOCK)
                ws = pl.multiple_of(jnp.clip(a - RADIUS, 0, L - W), RADIUS)
                typ = jnp.where(i == 0, 0, jnp.where(i == nblk - 1, 2, 1))
                pv, m, l = block_stats(q_ref[rl, pl.ds(a, Q_BLOCK), :], k_ref[rl, pl.ds(ws, W), :],
                                       v_ref[rl, pl.ds(ws, W), :], bias_ref[typ])
                merge(pl.ds(a * d + c * cpc + rl, Q_BLOCK, stride=d), pv, m, l, mode)
            return carry

        lax.fori_loop(0, cpc * nblk // ATTN_UNROLL, group, 0)

    def contiguous_phase():
        Sc = S // C
        nblk = Sc // Q_BLOCK
        for win_ref, prev_ref, cur_ref, next_ref in ((kwin_ref, k1p_ref, k1_ref, k1n_ref),
                                                     (vwin_ref, v1p_ref, v1_ref, v1n_ref)):
            win_ref[0:RADIUS, :] = prev_ref[...]
            win_ref[RADIUS:RADIUS + Sc, :] = cur_ref[...]
            win_ref[RADIUS + Sc:, :] = next_ref[...]

        def group(j, carry):
            for u in range(ATTN_UNROLL):
                i = j * ATTN_UNROLL + u
                a = pl.multiple_of(i * Q_BLOCK, Q_BLOCK)
                seq_start = jnp.logical_and(c == 0, i == 0)
                seq_end = jnp.logical_and(c == C - 1, i == nblk - 1)
                typ = jnp.where(seq_start, 1, jnp.where(seq_end, 2, 0))
                pv, m, l = block_stats(q1_ref[pl.ds(a, Q_BLOCK), :], kwin_ref[pl.ds(a, KEY_WINDOW), :],
                                       vwin_ref[pl.ds(a, KEY_WINDOW), :], b1_ref[typ])
                merge(pl.ds(pl.multiple_of(c * Sc + a, Q_BLOCK), Q_BLOCK), pv, m, l, "last")
            return carry

        lax.fori_loop(0, nblk // ATTN_UNROLL, group, 0)

    @pl.when(phase == 0)
    def _():
        strided_phase(16, q16_ref, k16_ref, v16_ref, b16_ref, "first")

    @pl.when(phase == 1)
    def _():
        strided_phase(4, q4_ref, k4_ref, v4_ref, b4_ref, "mid")

    @pl.when(phase == 2)
    def _():
        contiguous_phase()


def _attention(qkv, B, S, C):
    q1, k1, v1, q4, k4, v4, q16, k16, v16 = qkv
    Sc = S // C
    assert Sc % (Q_BLOCK * ATTN_UNROLL) == 0 and 4 % C == 0
    halo_blocks = S // RADIUS

    def chunk_of(own_phase, p, c):
        return jnp.where(p < own_phase, 0, jnp.where(p > own_phase, C - 1, c))

    def cls_spec(d, own_phase):
        L = S // d
        assert L % Q_BLOCK == 0 and (L == Q_BLOCK or L >= KEY_WINDOW)
        return pl.BlockSpec((None, None, d // C, L, LANES),
                            lambda b, h, p, c: (b, h, chunk_of(own_phase, p, c), 0, 0))

    nat = pl.BlockSpec((None, None, Sc, LANES), lambda b, h, p, c: (b, h, chunk_of(2, p, c), 0))
    prev = pl.BlockSpec((None, None, RADIUS, LANES),
                        lambda b, h, p, c: (b, h, jnp.maximum(chunk_of(2, p, c) * (Sc // RADIUS) - 1, 0), 0))
    nxt = pl.BlockSpec((None, None, RADIUS, LANES),
                       lambda b, h, p, c: (b, h, jnp.minimum((chunk_of(2, p, c) + 1) * (Sc // RADIUS),
                                                              halo_blocks - 1), 0))
    c4, c16 = cls_spec(4, 1), cls_spec(16, 0)
    W4, W16 = min(KEY_WINDOW, S // 4), min(KEY_WINDOW, S // 16)
    clamped = (0, -RADIUS, -2 * RADIUS)
    b1 = _band_bias((-RADIUS,) * 3, KEY_WINDOW,
                    extra=(lambda kj: kj >= 0, lambda kj: kj >= RADIUS, lambda kj: kj < KEY_WINDOW - RADIUS))
    b4, b16 = _band_bias(clamped, W4), _band_bias(clamped, W16)
    return pl.pallas_call(
        functools.partial(_attn_kernel, S=S, C=C),
        grid=(B, N_HEAD_PAIRS, 3, C),
        in_specs=[nat, nat, prev, nxt, nat, prev, nxt, c4, c4, c4, c16, c16, c16,
                  _resident(b1.shape), _resident(b4.shape), _resident(b16.shape)],
        out_specs=pl.BlockSpec((None, None, S, LANES), lambda b, h, p, c: (b, h, 0, 0)),
        out_shape=jax.ShapeDtypeStruct((B, N_HEAD_PAIRS, S, LANES), F32),
        scratch_shapes=[pltpu.VMEM((S, LANES), F32), pltpu.VMEM((S, LANES), F32),
                        pltpu.VMEM((Sc + 2 * RADIUS, LANES), BF16), pltpu.VMEM((Sc + 2 * RADIUS, LANES), BF16)],
        compiler_params=_params(4),
        name="attn",
    )(q1, k1, k1, k1, v1, v1, v1, q4, k4, v4, q16, k16, v16, b1, b4, b16)


def _dft_kernel(bc_ref, nbs_ref, r1_ref, r2_ref, a_ref, o_ref):
    @pl.when(pl.program_id(2) == 0)
    def _():
        o_ref[...] = jnp.zeros_like(o_ref)

    bc, nbs = bc_ref[...], nbs_ref[...]
    r1, r2 = r1_ref[...], r2_ref[...]
    t_cos = (bc * r1 + nbs * r2).astype(BF16)
    t_nsin = (nbs * r1 - bc * r2).astype(BF16)
    o_ref[...] += _dot(t_cos, a_ref[0]) + _dot(t_nsin, a_ref[1])


def _seq_dft(a, S, ts, tk, tn):
    ncols = a.shape[-1]
    t = jnp.arange(S, dtype=jnp.int32)[None, :]
    w = 2.0 * math.pi / S
    ang_b = ((jnp.arange(ts, dtype=jnp.int32)[:, None] * t) % S).astype(F32) * w
    ang_r = ((jnp.arange(0, S, ts, dtype=jnp.int32)[:, None] * t) % S).astype(F32) * w
    bc, nbs = jnp.cos(ang_b), -jnp.sin(ang_b)
    scale = S ** -0.5
    r1 = (jnp.cos(ang_r) * scale).reshape(S // ts, 1, S)
    r2 = (jnp.sin(ang_r) * scale).reshape(S // ts, 1, S)
    btab = pl.BlockSpec((ts, tk), lambda s, n, t: (0, t))
    rtab = pl.BlockSpec((None, 1, tk), lambda s, n, t: (s, 0, t))
    return pl.pallas_call(
        _dft_kernel,
        grid=(S // ts, ncols // tn, S // tk),
        in_specs=[btab, btab, rtab, rtab, pl.BlockSpec((2, tk, tn), lambda s, n, t: (0, t, n))],
        out_specs=pl.BlockSpec((ts, tn), lambda s, n, t: (s, n)),
        out_shape=jax.ShapeDtypeStruct((S, ncols), F32),
        compiler_params=_params(3),
        name="seq_dft",
    )(bc, nbs, r1, r2, a)


def _out_ffn_kernel(x1_ref, of_ref, oa_ref, gf_ref, ga_ref, wout_ref, g2_ref, wg_ref, wu_ref, wd_ref,
                    gfin_ref, y_ref):
    oa = jnp.concatenate([oa_ref[hp] for hp in range(N_HEAD_PAIRS)], axis=1)
    na = _rms(oa, ga_ref[...]).astype(BF16)
    nf = _rms(of_ref[...], gf_ref[...]).astype(BF16)
    x2 = x1_ref[...] + _dot(nf, wout_ref[:D_FOURIER, :]) + _dot(na, wout_ref[D_FOURIER:, :])
    h = _rms(x2, g2_ref[...]).astype(BF16)
    x3 = x2 + 0.5 * _swiglu(h, wg_ref, wu_ref, wd_ref)
    y_ref[...] = _rms(x3, gfin_ref[...])


def _out_ffn(x1, of, oa, B, S, gf, ga, wout, g2, wg, wu, wd, gfin, tm):
    N = B * S
    nst = S // tm
    tok = pl.BlockSpec((tm, D_MODEL), lambda i: (i, 0))
    att = pl.BlockSpec((None, N_HEAD_PAIRS, tm, LANES), lambda i: (i // nst, 0, i % nst, 0))
    return pl.pallas_call(
        _out_ffn_kernel,
        grid=(N // tm,),
        in_specs=[tok, pl.BlockSpec((tm, D_FOURIER), lambda i: (i % nst, i // nst)), att,
                  _resident((1, D_FOURIER)), _resident((1, D_ATTN)), _resident((D_MODEL, D_MODEL)),
                  _resident((1, D_MODEL)), _resident((D_MODEL, D_FF)), _resident((D_MODEL, D_FF)),
                  _resident((D_FF, D_MODEL)), _resident((1, D_MODEL))],
        out_specs=tok,
        out_shape=jax.ShapeDtypeStruct((N, D_MODEL), F32),
        compiler_params=_params(1),
        name="out_ffn",
    )(x1, of, oa, gf, ga, wout, g2, wg, wu, wd, gfin)


def _row(g):
    return g.reshape(1, -1).astype(F32)


def _trunk(x, p, win):
    B, S, _ = x.shape
    x1 = _ffn1(x.reshape(B * S, D_MODEL), _row(p["ffn1_norm"]), p["ffn1_w_gate"], p["ffn1_w_up"],
               p["ffn1_w_down"], tm=512)
    a, *qkv = _proj(x1, B, S, _row(p["mix_norm"]), win, tm=512)
    oa = _attention(qkv, B, S, C=max(1, S // 4096))
    of = _seq_dft(a, S, ts=512, tk=512, tn=min(2048, a.shape[-1]))
    y = _out_ffn(x1, of, oa, B, S, _row(p["fourier_out_norm"]), _row(p["attn_out_norm"]), p["w_out"],
                 _row(p["ffn2_norm"]), p["ffn2_w_gate"], p["ffn2_w_up"], p["ffn2_w_down"],
                 _row(p["final_norm"]), tm=512)
    return y.reshape(B, S, D_MODEL)


def kernel(x_prompt, x_sample, ffn1_norm, ffn1_w_gate, ffn1_w_up, ffn1_w_down, mix_norm, w_in, fourier_w,
           fourier_out_norm, attn_out_norm, w_out, ffn2_norm, ffn2_w_gate, ffn2_w_up, ffn2_w_down, final_norm):
    assert ffn1_w_gate.shape[0] == 1, "single-layer trunk"
    folded = _fold_fourier_weights(w_in[0, :, :D_FOURIER], fourier_w[0])
    win = jnp.concatenate([folded, w_in[0, :, D_FOURIER:].astype(BF16)], axis=1)
    p = dict(
        ffn1_norm=ffn1_norm[0], ffn1_w_gate=ffn1_w_gate[0].astype(BF16), ffn1_w_up=ffn1_w_up[0].astype(BF16),
        ffn1_w_down=ffn1_w_down[0].astype(BF16), mix_norm=mix_norm[0],
        fourier_out_norm=fourier_out_norm[0], attn_out_norm=attn_out_norm[0], w_out=w_out[0].astype(BF16),
        ffn2_norm=ffn2_norm[0], ffn2_w_gate=ffn2_w_gate[0].astype(BF16), ffn2_w_up=ffn2_w_up[0].astype(BF16),
        ffn2_w_down=ffn2_w_down[0].astype(BF16), final_norm=final_norm)
    return (_trunk(x_prompt, p, win), _trunk(x_sample, p, win))
```

```python
import functools
import math

import jax
import jax.numpy as jnp
import numpy as np
from jax import lax
from jax.experimental import pallas as pl
from jax.experimental.pallas import tpu as pltpu

F32 = jnp.float32
BF16 = jnp.bfloat16

D_MODEL = 1024
D_FF = 2816
D_FOURIER = 256
N_FOURIER_GROUPS = 4
FOURIER_GROUP_DIM = 64
D_ATTN = 768
HEAD_DIM = 64
LANES = 128
N_HEAD_PAIRS = D_ATTN // LANES
D_PROJ = 2 * D_FOURIER + 3 * D_ATTN
RADIUS = 64
ROPE_THETA = 10000.0
RMS_EPS = 1e-6
MASK_VALUE = -1e30

FF_CHUNK = 256
Q_BLOCK = 128
KEY_WINDOW = Q_BLOCK + 2 * RADIUS
ATTN_UNROLL = 8
LOG2_E = math.log2(math.e)
VMEM_LIMIT = 56 * 1024 * 1024


def _params(n_grid_dims):
    return pltpu.CompilerParams(dimension_semantics=("arbitrary",) * n_grid_dims,
                                vmem_limit_bytes=VMEM_LIMIT)


def _resident(shape):
    nd = len(shape)
    return pl.BlockSpec(shape, lambda *_: (0,) * nd, pipeline_mode=pl.Buffered(1))


def _rms(x, g):
    inv = lax.rsqrt(jnp.mean(x * x, axis=-1, keepdims=True) + RMS_EPS)
    return x * inv * g


def _dot(a, b):
    return jnp.dot(a, b, preferred_element_type=F32)


def _swiglu(h, wg_ref, wu_ref, wd_ref):
    acc = None
    for c in range(D_FF // FF_CHUNK):
        sl = slice(c * FF_CHUNK, (c + 1) * FF_CHUNK)
        g = _dot(h, wg_ref[:, sl])
        u = _dot(h, wu_ref[:, sl])
        a = (g * jax.nn.sigmoid(g) * u).astype(BF16)
        part = _dot(a, wd_ref[sl, :])
        acc = part if acc is None else acc + part
    return acc


def _fold_kernel(winf_ref, cbd_ref, sbd_ref, wbd_ref, o_ref):
    hi = lax.Precision.HIGHEST
    scale = FOURIER_GROUP_DIM ** -0.5
    gc = jnp.dot(cbd_ref[...], wbd_ref[...], precision=hi, preferred_element_type=F32) * scale
    gs = jnp.dot(sbd_ref[...], wbd_ref[...], precision=hi, preferred_element_type=F32) * scale
    winf = winf_ref[...]
    o_ref[:, :D_FOURIER] = jnp.dot(winf, gc, precision=hi, preferred_element_type=F32).astype(BF16)
    o_ref[:, D_FOURIER:] = jnp.dot(winf, gs, precision=hi, preferred_element_type=F32).astype(BF16)


def _fold_fourier_weights(w_in_f, fourier_w):
    c = np.arange(FOURIER_GROUP_DIM)
    ang = 2.0 * np.pi * ((c[:, None] * c[None, :]) % FOURIER_GROUP_DIM) / FOURIER_GROUP_DIM
    eye = np.eye(N_FOURIER_GROUPS)
    cbd = jnp.asarray(np.kron(eye, np.cos(ang)), F32)
    sbd = jnp.asarray(np.kron(eye, np.sin(ang)), F32)
    wbd = jnp.zeros((D_FOURIER, D_FOURIER), F32)
    for g in range(N_FOURIER_GROUPS):
        s = g * FOURIER_GROUP_DIM
        wbd = lax.dynamic_update_slice(wbd, fourier_w[g].astype(F32), (s, s))
    return pl.pallas_call(
        _fold_kernel,
        out_shape=jax.ShapeDtypeStruct((D_MODEL, 2 * D_FOURIER), BF16),
        name="fold_fourier",
    )(w_in_f, cbd, sbd, wbd)


def _ffn1_kernel(x_ref, g1_ref, wg_ref, wu_ref, wd_ref, x1_ref):
    x = x_ref[...]
    h = _rms(x, g1_ref[...]).astype(BF16)
    x1_ref[...] = x + 0.5 * _swiglu(h, wg_ref, wu_ref, wd_ref)


def _ffn1(xf, g1, wg, wu, wd, tm):
    N = xf.shape[0]
    tok = pl.BlockSpec((tm, D_MODEL), lambda i: (i, 0))
    return pl.pallas_call(
        _ffn1_kernel,
        grid=(N // tm,),
        in_specs=[tok, _resident((1, D_MODEL)), _resident((D_MODEL, D_FF)), _resident((D_MODEL, D_FF)),
                  _resident((D_FF, D_MODEL))],
        out_specs=tok,
        out_shape=jax.ShapeDtypeStruct((N, D_MODEL), F32),
        compiler_params=_params(1),
        name="ffn1",
    )(xf, g1, wg, wu, wd)


def _rope(t, cos, sin_lo, sin_hi):
    return (t * cos + pltpu.roll(t, LANES - HEAD_DIM // 2, axis=1) * sin_lo
            + pltpu.roll(t, HEAD_DIM // 2, axis=1) * sin_hi)


def _proj_kernel(x1_ref, gm_ref, win_ref, cos_ref, slo_ref, shi_ref,
                 a_ref, q1_ref, k1_ref, v1_ref, q4_ref, k4_ref, v4_ref, q16_ref, k16_ref, v16_ref,
                 nat_ref, cls4_ref):
    tm = x1_ref.shape[0]
    h2 = _rms(x1_ref[...], gm_ref[...]).astype(BF16)

    pa = _dot(h2, win_ref[:, :2 * D_FOURIER])
    a_ref[0] = pa[:, :D_FOURIER].astype(BF16)
    a_ref[1] = pa[:, D_FOURIER:].astype(BF16)

    def emit(t, hp, o1_ref, o4_ref, o16_ref):
        o1_ref[hp] = t.astype(BF16)
        nat_ref[...] = t
        for r4 in range(4):
            c4 = nat_ref[pl.ds(r4, tm // 4, stride=4), :]
            o4_ref[hp, r4] = c4.astype(BF16)
            cls4_ref[...] = c4
            for j in range(4):
                o16_ref[hp, r4 + 4 * j] = cls4_ref[pl.ds(j, tm // 16, stride=4), :].astype(BF16)

    cos = cos_ref[...]
    slo = slo_ref[...]
    shi = shi_ref[...]
    base = 2 * D_FOURIER
    for j in range(D_ATTN // 256):
        c0 = j * 256
        qq = _dot(h2, win_ref[:, base + c0: base + c0 + 256])
        kk = _dot(h2, win_ref[:, base + D_ATTN + c0: base + D_ATTN + c0 + 256])
        vv = _dot(h2, win_ref[:, base + 2 * D_ATTN + c0: base + 2 * D_ATTN + c0 + 256])
        for i in range(2):
            sl = slice(i * LANES, (i + 1) * LANES)
            hp = 2 * j + i
            emit(_rope(qq[:, sl], cos, slo, shi) * (HEAD_DIM ** -0.5 * LOG2_E), hp, q1_ref, q4_ref, q16_ref)
            emit(_rope(kk[:, sl], cos, slo, shi), hp, k1_ref, k4_ref, k16_ref)
            emit(vv[:, sl], hp, v1_ref, v4_ref, v16_ref)


def _rope_tables(S):
    inv_freq = ROPE_THETA ** (-jnp.arange(0, HEAD_DIM, 2, dtype=F32) / HEAD_DIM)
    freqs = jnp.arange(S, dtype=F32)[:, None] * inv_freq[None, :]
    cos, sin = jnp.cos(freqs), jnp.sin(freqs)
    zero = jnp.zeros_like(sin)
    cos_t = jnp.tile(cos, (1, LANES // (HEAD_DIM // 2)))
    sin_lo = jnp.tile(jnp.concatenate([-sin, zero], axis=1), (1, LANES // HEAD_DIM))
    sin_hi = jnp.tile(jnp.concatenate([zero, sin], axis=1), (1, LANES // HEAD_DIM))
    return cos_t, sin_lo, sin_hi


def _proj(x1, B, S, gm, win, tm):
    N = B * S
    nst = S // tm
    cos_t, sin_lo, sin_hi = _rope_tables(S)
    tok = pl.BlockSpec((tm, D_MODEL), lambda i: (i, 0))
    tab = pl.BlockSpec((tm, LANES), lambda i: (i % nst, 0))
    nat = pl.BlockSpec((None, N_HEAD_PAIRS, tm, LANES), lambda i: (i // nst, 0, i % nst, 0))
    nat_shape = jax.ShapeDtypeStruct((B, N_HEAD_PAIRS, S, LANES), BF16)

    def cls(d):
        spec = pl.BlockSpec((None, N_HEAD_PAIRS, d, tm // d, LANES), lambda i: (i // nst, 0, 0, i % nst, 0))
        return spec, jax.ShapeDtypeStruct((B, N_HEAD_PAIRS, d, S // d, LANES), BF16)

    c4, c4_shape = cls(4)
    c16, c16_shape = cls(16)
    return pl.pallas_call(
        _proj_kernel,
        grid=(N // tm,),
        in_specs=[tok, _resident((1, D_MODEL)), _resident((D_MODEL, D_PROJ)), tab, tab, tab],
        out_specs=[pl.BlockSpec((2, tm, D_FOURIER), lambda i: (0, i % nst, i // nst)),
                   nat, nat, nat, c4, c4, c4, c16, c16, c16],
        out_shape=[jax.ShapeDtypeStruct((2, S, B * D_FOURIER), BF16),
                   nat_shape, nat_shape, nat_shape, c4_shape, c4_shape, c4_shape,
                   c16_shape, c16_shape, c16_shape],
        scratch_shapes=[pltpu.VMEM((tm, LANES), F32), pltpu.VMEM((tm // 4, LANES), F32)],
        compiler_params=_params(1),
        name="proj",
    )(x1, gm, win, cos_t, sin_lo, sin_hi)


def _band_bias(offsets, W, extra=None):
    qi = np.arange(Q_BLOCK)[:, None]
    kj = np.arange(W)[None, :]
    tiles = []
    for t, off in enumerate(offsets):
        ok = np.abs(kj + off - qi) <= RADIUS
        if extra is not None:
            ok &= extra[t](kj)
        tiles.append(np.where(ok, 0.0, MASK_VALUE))
    return jnp.asarray(np.stack(tiles), F32)


def _attn_kernel(q1_ref, k1_ref, k1p_ref, k1n_ref, v1_ref, v1p_ref, v1n_ref,
                 q4_ref, k4_ref, v4_ref, q16_ref, k16_ref, v16_ref,
                 b1_ref, b4_ref, b16_ref,
                 o_ref, m_ref, l_ref, kwin_ref, vwin_ref, *, S, C):
    phase = pl.program_id(2)
    c = pl.program_id(3)
    lane = lax.broadcasted_iota(jnp.int32, (1, LANES), 1)
    first_head = lane < HEAD_DIM
    contract_last = (((1,), (1,)), ((), ()))

    def block_stats(q2, k2, v2, bias):
        v_ones = jnp.concatenate([v2, jnp.ones_like(v2)], axis=1)
        per_head = []
        for head_mask in (first_head, jnp.logical_not(first_head)):
            qh = jnp.where(head_mask, q2, jnp.zeros_like(q2))
            s = lax.dot_general(qh, k2, contract_last, preferred_element_type=F32) + bias
            m = jnp.max(s, axis=-1, keepdims=True)
            p = jnp.exp2(s - m).astype(BF16)
            pvl = _dot(p, v_ones)
            per_head.append((pvl[:, :LANES], jnp.broadcast_to(m, (Q_BLOCK, LANES)), pvl[:, LANES:]))
        return tuple(jnp.where(first_head, a, b) for a, b in zip(*per_head))

    def merge(rows, pv, m, l, mode):
        if mode == "first":
            o_ref[rows, :] = pv
            m_ref[rows, :] = m
            l_ref[rows, :] = l
            return
        m_old = m_ref[rows, :]
        m_new = jnp.maximum(m_old, m)
        a_old = jnp.exp2(m_old - m_new)
        a_blk = jnp.exp2(m - m_new)
        acc = a_old * o_ref[rows, :] + a_blk * pv
        den = a_old * l_ref[rows, :] + a_blk * l
        if mode == "last":
            o_ref[rows, :] = acc / den
        else:
            o_ref[rows, :] = acc
            m_ref[rows, :] = m_new
            l_ref[rows, :] = den

    def strided_phase(d, q_ref, k_ref, v_ref, bias_ref, mode):
        L = S // d
        nblk = L // Q_BLOCK
        W = min(KEY_WINDOW, L)
        cpc = d // C
        shift = nblk.bit_length() - 1

        def group(j, carry):
            for u in range(ATTN_UNROLL):
                g = j * ATTN_UNROLL + u
                rl = lax.shift_right_logical(g, shift)
                i = jnp.bitwise_and(g, nblk - 1)
                a = pl.multiple_of(i * Q_BLOCK, Q_BLOCK)
                ws = pl.multiple_of(jnp.clip(a - RADIUS, 0, L - W), RADIUS)
                typ = jnp.where(i == 0, 0, jnp.where(i == nblk - 1, 2, 1))
                k2 = k_ref[rl, pl.ds(ws, W), :]
                v2 = v_ref[rl, pl.ds(ws, W), :]
                bias = bias_ref[typ]
                if W < KEY_WINDOW:
                    pad = jnp.zeros((KEY_WINDOW - W, LANES), BF16)
                    k2 = jnp.concatenate([k2, pad], axis=0)
                    v2 = jnp.concatenate([v2, pad], axis=0)
                    bias = jnp.concatenate([bias, jnp.full((Q_BLOCK, KEY_WINDOW - W), MASK_VALUE, F32)], axis=1)
                pv, m, l = block_stats(q_ref[rl, pl.ds(a, Q_BLOCK), :], k2, v2, bias)
                merge(pl.ds(a * d + c * cpc + rl, Q_BLOCK, stride=d), pv, m, l, mode)
            return carry

        lax.fori_loop(0, cpc * nblk // ATTN_UNROLL, group, 0)

    def contiguous_phase():
        Sc = S // C
        nblk = Sc // Q_BLOCK
        for win_ref, prev_ref, cur_ref, next_ref in ((kwin_ref, k1p_ref, k1_ref, k1n_ref),
                                                     (vwin_ref, v1p_ref, v1_ref, v1n_ref)):
            win_ref[0:RADIUS, :] = prev_ref[...]
            win_ref[RADIUS:RADIUS + Sc, :] = cur_ref[...]
            win_ref[RADIUS + Sc:, :] = next_ref[...]

        def group(j, carry):
            for u in range(ATTN_UNROLL):
                i = j * ATTN_UNROLL + u
                a = pl.multiple_of(i * Q_BLOCK, Q_BLOCK)
                seq_start = jnp.logical_and(c == 0, i == 0)
                seq_end = jnp.logical_and(c == C - 1, i == nblk - 1)
                typ = jnp.where(seq_start, 1, jnp.where(seq_end, 2, 0))
                pv, m, l = block_stats(q1_ref[pl.ds(a, Q_BLOCK), :], kwin_ref[pl.ds(a, KEY_WINDOW), :],
                                       vwin_ref[pl.ds(a, KEY_WINDOW), :], b1_ref[typ])
                merge(pl.ds(pl.multiple_of(c * Sc + a, Q_BLOCK), Q_BLOCK), pv, m, l, "last")
            return carry

        lax.fori_loop(0, nblk // ATTN_UNROLL, group, 0)

    @pl.when(phase == 0)
    def _():
        strided_phase(16, q16_ref, k16_ref, v16_ref, b16_ref, "first")

    @pl.when(phase == 1)
    def _():
        strided_phase(4, q4_ref, k4_ref, v4_ref, b4_ref, "mid")

    @pl.when(phase == 2)
    def _():
        contiguous_phase()


def _attention(qkv, B, S, C):
    q1, k1, v1, q4, k4, v4, q16, k16, v16 = qkv
    Sc = S // C
    assert Sc % (Q_BLOCK * ATTN_UNROLL) == 0 and 4 % C == 0
    halo_blocks = S // RADIUS

    def chunk_of(own_phase, p, c):
        return jnp.where(p < own_phase, 0, jnp.where(p > own_phase, C - 1, c))

    def cls_spec(d, own_phase):
        L = S // d
        assert L % Q_BLOCK == 0 and (L == Q_BLOCK or L >= KEY_WINDOW)
        return pl.BlockSpec((None, None, d // C, L, LANES),
                            lambda b, h, p, c: (b, h, chunk_of(own_phase, p, c), 0, 0))

    nat = pl.BlockSpec((None, None, Sc, LANES), lambda b, h, p, c: (b, h, chunk_of(2, p, c), 0))
    prev = pl.BlockSpec((None, None, RADIUS, LANES),
                        lambda b, h, p, c: (b, h, jnp.maximum(chunk_of(2, p, c) * (Sc // RADIUS) - 1, 0), 0))
    nxt = pl.BlockSpec((None, None, RADIUS, LANES),
                       lambda b, h, p, c: (b, h, jnp.minimum((chunk_of(2, p, c) + 1) * (Sc // RADIUS),
                                                              halo_blocks - 1), 0))
    c4, c16 = cls_spec(4, 1), cls_spec(16, 0)
    W4, W16 = min(KEY_WINDOW, S // 4), min(KEY_WINDOW, S // 16)
    clamped = (0, -RADIUS, -2 * RADIUS)
    b1 = _band_bias((-RADIUS,) * 3, KEY_WINDOW,
                    extra=(lambda kj: kj >= 0, lambda kj: kj >= RADIUS, lambda kj: kj < KEY_WINDOW - RADIUS))
    b4, b16 = _band_bias(clamped, W4), _band_bias(clamped, W16)
    return pl.pallas_call(
        functools.partial(_attn_kernel, S=S, C=C),
        grid=(B, N_HEAD_PAIRS, 3, C),
        in_specs=[nat, nat, prev, nxt, nat, prev, nxt, c4, c4, c4, c16, c16, c16,
                  _resident(b1.shape), _resident(b4.shape), _resident(b16.shape)],
        out_specs=pl.BlockSpec((None, None, S, LANES), lambda b, h, p, c: (b, h, 0, 0)),
        out_shape=jax.ShapeDtypeStruct((B, N_HEAD_PAIRS, S, LANES), F32),
        scratch_shapes=[pltpu.VMEM((S, LANES), F32), pltpu.VMEM((S, LANES), F32),
                        pltpu.VMEM((Sc + 2 * RADIUS, LANES), BF16), pltpu.VMEM((Sc + 2 * RADIUS, LANES), BF16)],
        compiler_params=_params(4),
        name="attn",
    )(q1, k1, k1, k1, v1, v1, v1, q4, k4, v4, q16, k16, v16, b1, b4, b16)


def _dft_kernel(bc_ref, nbs_ref, r1_ref, r2_ref, a_ref, o_ref):
    @pl.when(pl.program_id(2) == 0)
    def _():
        o_ref[...] = jnp.zeros_like(o_ref)

    bc, nbs = bc_ref[...], nbs_ref[...]
    r1, r2 = r1_ref[...], r2_ref[...]
    t_cos = (bc * r1 + nbs * r2).astype(BF16)
    t_nsin = (nbs * r1 - bc * r2).astype(BF16)
    o_ref[...] += _dot(t_cos, a_ref[0]) + _dot(t_nsin, a_ref[1])


def _seq_dft(a, S, ts, tk, tn):
    ncols = a.shape[-1]
    t = jnp.arange(S, dtype=jnp.int32)[None, :]
    w = 2.0 * math.pi / S
    ang_b = ((jnp.arange(ts, dtype=jnp.int32)[:, None] * t) % S).astype(F32) * w
    ang_r = ((jnp.arange(0, S, ts, dtype=jnp.int32)[:, None] * t) % S).astype(F32) * w
    bc, nbs = jnp.cos(ang_b), -jnp.sin(ang_b)
    scale = S ** -0.5
    r1 = (jnp.cos(ang_r) * scale).reshape(S // ts, 1, S)
    r2 = (jnp.sin(ang_r) * scale).reshape(S // ts, 1, S)
    btab = pl.BlockSpec((ts, tk), lambda s, n, t: (0, t))
    rtab = pl.BlockSpec((None, 1, tk), lambda s, n, t: (s, 0, t))
    return pl.pallas_call(
        _dft_kernel,
        grid=(S // ts, ncols // tn, S // tk),
        in_specs=[btab, btab, rtab, rtab, pl.BlockSpec((2, tk, tn), lambda s, n, t: (0, t, n))],
        out_specs=pl.BlockSpec((ts, tn), lambda s, n, t: (s, n)),
        out_shape=jax.ShapeDtypeStruct((S, ncols), F32),
        compiler_params=_params(3),
        name="seq_dft",
    )(bc, nbs, r1, r2, a)


def _out_ffn_kernel(x1_ref, of_ref, oa_ref, gf_ref, ga_ref, wout_ref, g2_ref, wg_ref, wu_ref, wd_ref,
                    gfin_ref, y_ref):
    oa = jnp.concatenate([oa_ref[hp] for hp in range(N_HEAD_PAIRS)], axis=1)
    na = _rms(oa, ga_ref[...]).astype(BF16)
    nf = _rms(of_ref[...], gf_ref[...]).astype(BF16)
    x2 = x1_ref[...] + _dot(nf, wout_ref[:D_FOURIER, :]) + _dot(na, wout_ref[D_FOURIER:, :])
    h = _rms(x2, g2_ref[...]).astype(BF16)
    x3 = x2 + 0.5 * _swiglu(h, wg_ref, wu_ref, wd_ref)
    y_ref[...] = _rms(x3, gfin_ref[...])


def _out_ffn(x1, of, oa, B, S, gf, ga, wout, g2, wg, wu, wd, gfin, tm):
    N = B * S
    nst = S // tm
    tok = pl.BlockSpec((tm, D_MODEL), lambda i: (i, 0))
    att = pl.BlockSpec((None, N_HEAD_PAIRS, tm, LANES), lambda i: (i // nst, 0, i % nst, 0))
    return pl.pallas_call(
        _out_ffn_kernel,
        grid=(N // tm,),
        in_specs=[tok, pl.BlockSpec((tm, D_FOURIER), lambda i: (i % nst, i // nst)), att,
                  _resident((1, D_FOURIER)), _resident((1, D_ATTN)), _resident((D_MODEL, D_MODEL)),
                  _resident((1, D_MODEL)), _resident((D_MODEL, D_FF)), _resident((D_MODEL, D_FF)),
                  _resident((D_FF, D_MODEL)), _resident((1, D_MODEL))],
        out_specs=tok,
        out_shape=jax.ShapeDtypeStruct((N, D_MODEL), F32),
        compiler_params=_params(1),
        name="out_ffn",
    )(x1, of, oa, gf, ga, wout, g2, wg, wu, wd, gfin)


def _row(g):
    return g.reshape(1, -1).astype(F32)


def _trunk(x, p, win):
    B, S, _ = x.shape
    x1 = _ffn1(x.reshape(B * S, D_MODEL), _row(p["ffn1_norm"]), p["ffn1_w_gate"], p["ffn1_w_up"],
               p["ffn1_w_down"], tm=512)
    a, *qkv = _proj(x1, B, S, _row(p["mix_norm"]), win, tm=512)
    oa = _attention(qkv, B, S, C=max(1, S // 4096))
    of = _seq_dft(a, S, ts=512, tk=512, tn=min(2048, a.shape[-1]))
    y = _out_ffn(x1, of, oa, B, S, _row(p["fourier_out_norm"]), _row(p["attn_out_norm"]), p["w_out"],
                 _row(p["ffn2_norm"]), p["ffn2_w_gate"], p["ffn2_w_up"], p["ffn2_w_down"],
                 _row(p["final_norm"]), tm=512)
    return y.reshape(B, S, D_MODEL)


def kernel(x_prompt, x_sample, ffn1_norm, ffn1_w_gate, ffn1_w_up, ffn1_w_down, mix_norm, w_in, fourier_w,
           fourier_out_norm, attn_out_norm, w_out, ffn2_norm, ffn2_w_gate, ffn2_w_up, ffn2_w_down, final_norm):
    assert ffn1_w_gate.shape[0] == 1, "single-layer trunk"
    folded = _fold_fourier_weights(w_in[0, :, :D_FOURIER], fourier_w[0])
    win = jnp.concatenate([folded, w_in[0, :, D_FOURIER:].astype(BF16)], axis=1)
    p = dict(
        ffn1_norm=ffn1_norm[0], ffn1_w_gate=ffn1_w_gate[0].astype(BF16), ffn1_w_up=ffn1_w_up[0].astype(BF16),
        ffn1_w_down=ffn1_w_down[0].astype(BF16), mix_norm=mix_norm[0],
        fourier_out_norm=fourier_out_norm[0], attn_out_norm=attn_out_norm[0], w_out=w_out[0].astype(BF16),
        ffn2_norm=ffn2_norm[0], ffn2_w_gate=ffn2_w_gate[0].astype(BF16), ffn2_w_up=ffn2_w_up[0].astype(BF16),
        ffn2_w_down=ffn2_w_down[0].astype(BF16), final_norm=final_norm)
    return (_trunk(x_prompt, p, win), _trunk(x_sample, p, win))
```

```python
import functools
import math

import jax
import jax.numpy as jnp
import numpy as np
from jax import lax
from jax.experimental import pallas as pl
from jax.experimental.pallas import tpu as pltpu

F32 = jnp.float32
BF16 = jnp.bfloat16

D_MODEL = 1024
D_FF = 2816
D_FOURIER = 256
N_FOURIER_GROUPS = 4
FOURIER_GROUP_DIM = 64
D_ATTN = 768
HEAD_DIM = 64
LANES = 128
N_HEAD_PAIRS = D_ATTN // LANES
D_PROJ = 2 * D_FOURIER + 3 * D_ATTN
RADIUS = 64
ROPE_THETA = 10000.0
RMS_EPS = 1e-6
MASK_VALUE = -1e30

FF_CHUNK = 256
Q_BLOCK = 128
KEY_WINDOW = Q_BLOCK + 2 * RADIUS
ATTN_UNROLL = 8
LOG2_E = math.log2(math.e)
VMEM_LIMIT = 56 * 1024 * 1024


def _params(n_grid_dims):
    return pltpu.CompilerParams(dimension_semantics=("arbitrary",) * n_grid_dims,
                                vmem_limit_bytes=VMEM_LIMIT)


def _resident(shape):
    nd = len(shape)
    return pl.BlockSpec(shape, lambda *_: (0,) * nd, pipeline_mode=pl.Buffered(1))


def _rms(x, g):
    inv = lax.rsqrt(jnp.mean(x * x, axis=-1, keepdims=True) + RMS_EPS)
    return x * inv * g


def _dot(a, b):
    return jnp.dot(a, b, preferred_element_type=F32)


def _swiglu(h, wg_ref, wu_ref, wd_ref):
    acc = None
    for c in range(D_FF // FF_CHUNK):
        sl = slice(c * FF_CHUNK, (c + 1) * FF_CHUNK)
        g = _dot(h, wg_ref[:, sl])
        u = _dot(h, wu_ref[:, sl])
        a = (g * jax.nn.sigmoid(g) * u).astype(BF16)
        part = _dot(a, wd_ref[sl, :])
        acc = part if acc is None else acc + part
    return acc


def _fold_kernel(winf_ref, cbd_ref, sbd_ref, wbd_ref, o_ref):
    hi = lax.Precision.HIGHEST
    scale = FOURIER_GROUP_DIM ** -0.5
    gc = jnp.dot(cbd_ref[...], wbd_ref[...], precision=hi, preferred_element_type=F32) * scale
    gs = jnp.dot(sbd_ref[...], wbd_ref[...], precision=hi, preferred_element_type=F32) * scale
    winf = winf_ref[...]
    o_ref[:, :D_FOURIER] = jnp.dot(winf, gc, precision=hi, preferred_element_type=F32).astype(BF16)
    o_ref[:, D_FOURIER:] = jnp.dot(winf, gs, precision=hi, preferred_element_type=F32).astype(BF16)


def _fold_fourier_weights(w_in_f, fourier_w):
    c = np.arange(FOURIER_GROUP_DIM)
    ang = 2.0 * np.pi * ((c[:, None] * c[None, :]) % FOURIER_GROUP_DIM) / FOURIER_GROUP_DIM
    eye = np.eye(N_FOURIER_GROUPS)
    cbd = jnp.asarray(np.kron(eye, np.cos(ang)), F32)
    sbd = jnp.asarray(np.kron(eye, np.sin(ang)), F32)
    wbd = jnp.zeros((D_FOURIER, D_FOURIER), F32)
    for g in range(N_FOURIER_GROUPS):
        s = g * FOURIER_GROUP_DIM
        wbd = lax.dynamic_update_slice(wbd, fourier_w[g].astype(F32), (s, s))
    return pl.pallas_call(
        _fold_kernel,
        out_shape=jax.ShapeDtypeStruct((D_MODEL, 2 * D_FOURIER), BF16),
        name="fold_fourier",
    )(w_in_f, cbd, sbd, wbd)


def _ffn1_kernel(x_ref, g1_ref, wg_ref, wu_ref, wd_ref, x1_ref):
    x = x_ref[...]
    h = _rms(x, g1_ref[...]).astype(BF16)
    x1_ref[...] = x + 0.5 * _swiglu(h, wg_ref, wu_ref, wd_ref)


def _ffn1(xf, g1, wg, wu, wd, tm):
    N = xf.shape[0]
    tok = pl.BlockSpec((tm, D_MODEL), lambda i: (i, 0))
    return pl.pallas_call(
        _ffn1_kernel,
        grid=(N // tm,),
        in_specs=[tok, _resident((1, D_MODEL)), _resident((D_MODEL, D_FF)), _resident((D_MODEL, D_FF)),
                  _resident((D_FF, D_MODEL))],
        out_specs=tok,
        out_shape=jax.ShapeDtypeStruct((N, D_MODEL), F32),
        compiler_params=_params(1),
        name="ffn1",
    )(xf, g1, wg, wu, wd)


def _rope(t, cos, sin_lo, sin_hi):
    return (t * cos + pltpu.roll(t, LANES - HEAD_DIM // 2, axis=1) * sin_lo
            + pltpu.roll(t, HEAD_DIM // 2, axis=1) * sin_hi)


def _proj_kernel(x1_ref, gm_ref, win_ref, cos_ref, slo_ref, shi_ref,
                 a_ref, q1_ref, k1_ref, v1_ref, q4_ref, k4_ref, v4_ref, q16_ref, k16_ref, v16_ref,
                 nat_ref, cls4_ref, pa_ref):
    tm = x1_ref.shape[0]
    h2 = _rms(x1_ref[...], gm_ref[...]).astype(BF16)

    pa = _dot(h2, win_ref[:, :2 * D_FOURIER])
    tiles_per_part = D_FOURIER // LANES
    for j in range(2 * tiles_per_part):
        pa_ref[j] = pa[:, j * LANES:(j + 1) * LANES]
    for par in range(2):
        for j in range(2 * tiles_per_part):
            lanes = slice((j % tiles_per_part) * LANES, (j % tiles_per_part + 1) * LANES)
            a_ref[par, j // tiles_per_part, :, lanes] = pa_ref[j, pl.ds(par, tm // 2, stride=2), :].astype(BF16)

    def emit(t, hp, o1_ref, o4_ref, o16_ref):
        o1_ref[hp] = t.astype(BF16)
        nat_ref[...] = t
        for r4 in range(4):
            c4 = nat_ref[pl.ds(r4, tm // 4, stride=4), :]
            o4_ref[hp, r4] = c4.astype(BF16)
            cls4_ref[...] = c4
            for j in range(4):
                o16_ref[hp, r4 + 4 * j] = cls4_ref[pl.ds(j, tm // 16, stride=4), :].astype(BF16)

    cos = cos_ref[...]
    slo = slo_ref[...]
    shi = shi_ref[...]
    base = 2 * D_FOURIER
    for j in range(D_ATTN // 256):
        c0 = j * 256
        qq = _dot(h2, win_ref[:, base + c0: base + c0 + 256])
        kk = _dot(h2, win_ref[:, base + D_ATTN + c0: base + D_ATTN + c0 + 256])
        vv = _dot(h2, win_ref[:, base + 2 * D_ATTN + c0: base + 2 * D_ATTN + c0 + 256])
        for i in range(2):
            sl = slice(i * LANES, (i + 1) * LANES)
            hp = 2 * j + i
            emit(_rope(qq[:, sl], cos, slo, shi) * (HEAD_DIM ** -0.5 * LOG2_E), hp, q1_ref, q4_ref, q16_ref)
            emit(_rope(kk[:, sl], cos, slo, shi), hp, k1_ref, k4_ref, k16_ref)
            emit(vv[:, sl], hp, v1_ref, v4_ref, v16_ref)


def _rope_tables(S):
    inv_freq = ROPE_THETA ** (-jnp.arange(0, HEAD_DIM, 2, dtype=F32) / HEAD_DIM)
    freqs = jnp.arange(S, dtype=F32)[:, None] * inv_freq[None, :]
    cos, sin = jnp.cos(freqs), jnp.sin(freqs)
    zero = jnp.zeros_like(sin)
    cos_t = jnp.tile(cos, (1, LANES // (HEAD_DIM // 2)))
    sin_lo = jnp.tile(jnp.concatenate([-sin, zero], axis=1), (1, LANES // HEAD_DIM))
    sin_hi = jnp.tile(jnp.concatenate([zero, sin], axis=1), (1, LANES // HEAD_DIM))
    return cos_t, sin_lo, sin_hi


def _proj(x1, B, S, gm, win, tm):
    N = B * S
    nst = S // tm
    cos_t, sin_lo, sin_hi = _rope_tables(S)
    tok = pl.BlockSpec((tm, D_MODEL), lambda i: (i, 0))
    tab = pl.BlockSpec((tm, LANES), lambda i: (i % nst, 0))
    nat = pl.BlockSpec((None, N_HEAD_PAIRS, tm, LANES), lambda i: (i // nst, 0, i % nst, 0))
    nat_shape = jax.ShapeDtypeStruct((B, N_HEAD_PAIRS, S, LANES), BF16)

    def cls(d):
        spec = pl.BlockSpec((None, N_HEAD_PAIRS, d, tm // d, LANES), lambda i: (i // nst, 0, 0, i % nst, 0))
        return spec, jax.ShapeDtypeStruct((B, N_HEAD_PAIRS, d, S // d, LANES), BF16)

    c4, c4_shape = cls(4)
    c16, c16_shape = cls(16)
    return pl.pallas_call(
        _proj_kernel,
        grid=(N // tm,),
        in_specs=[tok, _resident((1, D_MODEL)), _resident((D_MODEL, D_PROJ)), tab, tab, tab],
        out_specs=[pl.BlockSpec((2, 2, tm // 2, D_FOURIER), lambda i: (0, 0, i % nst, i // nst)),
                   nat, nat, nat, c4, c4, c4, c16, c16, c16],
        out_shape=[jax.ShapeDtypeStruct((2, 2, S // 2, B * D_FOURIER), BF16),
                   nat_shape, nat_shape, nat_shape, c4_shape, c4_shape, c4_shape,
                   c16_shape, c16_shape, c16_shape],
        scratch_shapes=[pltpu.VMEM((tm, LANES), F32), pltpu.VMEM((tm // 4, LANES), F32),
                        pltpu.VMEM((2 * D_FOURIER // LANES, tm, LANES), F32)],
        compiler_params=_params(1),
        name="proj",
    )(x1, gm, win, cos_t, sin_lo, sin_hi)


def _band_bias(offsets, W, extra=None):
    qi = np.arange(Q_BLOCK)[:, None]
    kj = np.arange(W)[None, :]
    tiles = []
    for t, off in enumerate(offsets):
        ok = np.abs(kj + off - qi) <= RADIUS
        if extra is not None:
            ok &= extra[t](kj)
        tiles.append(np.where(ok, 0.0, MASK_VALUE))
    return jnp.asarray(np.stack(tiles), F32)


def _attn_kernel(q1_ref, k1_ref, k1p_ref, k1n_ref, v1_ref, v1p_ref, v1n_ref,
                 q4_ref, k4_ref, v4_ref, q16_ref, k16_ref, v16_ref,
                 b1_ref, b4_ref, b16_ref,
                 o_ref, m_ref, l_ref, kwin_ref, vwin_ref, *, S, C):
    phase = pl.program_id(2)
    c = pl.program_id(3)
    lane = lax.broadcasted_iota(jnp.int32, (1, LANES), 1)
    first_head = lane < HEAD_DIM
    contract_last = (((1,), (1,)), ((), ()))

    def block_stats(q2, k2, v2, bias):
        v_ones = jnp.concatenate([v2, jnp.ones_like(v2)], axis=1)
        per_head = []
        for head_mask in (first_head, jnp.logical_not(first_head)):
            qh = jnp.where(head_mask, q2, jnp.zeros_like(q2))
            s = lax.dot_general(qh, k2, contract_last, preferred_element_type=F32) + bias
            m = jnp.max(s, axis=-1, keepdims=True)
            p = jnp.exp2(s - m).astype(BF16)
            pvl = _dot(p, v_ones)
            per_head.append((pvl[:, :LANES], jnp.broadcast_to(m, (Q_BLOCK, LANES)), pvl[:, LANES:]))
        return tuple(jnp.where(first_head, a, b) for a, b in zip(*per_head))

    def merge(rows, pv, m, l, mode):
        if mode == "first":
            o_ref[rows, :] = pv
            m_ref[rows, :] = m
            l_ref[rows, :] = l
            return
        m_old = m_ref[rows, :]
        m_new = jnp.maximum(m_old, m)
        a_old = jnp.exp2(m_old - m_new)
        a_blk = jnp.exp2(m - m_new)
        acc = a_old * o_ref[rows, :] + a_blk * pv
        den = a_old * l_ref[rows, :] + a_blk * l
        if mode == "last":
            o_ref[rows, :] = acc / den
        else:
            o_ref[rows, :] = acc
            m_ref[rows, :] = m_new
            l_ref[rows, :] = den

    def strided_phase(d, q_ref, k_ref, v_ref, bias_ref, mode):
        L = S // d
        nblk = L // Q_BLOCK
        W = min(KEY_WINDOW, L)
        cpc = d // C
        shift = nblk.bit_length() - 1

        def group(j, carry):
            for u in range(ATTN_UNROLL):
                g = j * ATTN_UNROLL + u
                rl = lax.shift_right_logical(g, shift)
                i = jnp.bitwise_and(g, nblk - 1)
                a = pl.multiple_of(i * Q_BLOCK, Q_BLOCK)
                ws = pl.multiple_of(jnp.clip(a - RADIUS, 0, L - W), RADIUS)
                typ = jnp.where(i == 0, 0, jnp.where(i == nblk - 1, 2, 1))
                k2 = k_ref[rl, pl.ds(ws, W), :]
                v2 = v_ref[rl, pl.ds(ws, W), :]
                bias = bias_ref[typ]
                if W < KEY_WINDOW:
                    pad = jnp.zeros((KEY_WINDOW - W, LANES), BF16)
                    k2 = jnp.concatenate([k2, pad], axis=0)
                    v2 = jnp.concatenate([v2, pad], axis=0)
                    bias = jnp.concatenate([bias, jnp.full((Q_BLOCK, KEY_WINDOW - W), MASK_VALUE, F32)], axis=1)
                pv, m, l = block_stats(q_ref[rl, pl.ds(a, Q_BLOCK), :], k2, v2, bias)
                merge(pl.ds(a * d + c * cpc + rl, Q_BLOCK, stride=d), pv, m, l, mode)
            return carry

        lax.fori_loop(0, cpc * nblk // ATTN_UNROLL, group, 0)

    def contiguous_phase():
        Sc = S // C
        nblk = Sc // Q_BLOCK
        for win_ref, prev_ref, cur_ref, next_ref in ((kwin_ref, k1p_ref, k1_ref, k1n_ref),
                                                     (vwin_ref, v1p_ref, v1_ref, v1n_ref)):
            win_ref[0:RADIUS, :] = prev_ref[...]
            win_ref[RADIUS:RADIUS + Sc, :] = cur_ref[...]
            win_ref[RADIUS + Sc:, :] = next_ref[...]

        def group(j, carry):
            for u in range(ATTN_UNROLL):
                i = j * ATTN_UNROLL + u
                a = pl.multiple_of(i * Q_BLOCK, Q_BLOCK)
                seq_start = jnp.logical_and(c == 0, i == 0)
                seq_end = jnp.logical_and(c == C - 1, i == nblk - 1)
                typ = jnp.where(seq_start, 1, jnp.where(seq_end, 2, 0))
                pv, m, l = block_stats(q1_ref[pl.ds(a, Q_BLOCK), :], kwin_ref[pl.ds(a, KEY_WINDOW), :],
                                       vwin_ref[pl.ds(a, KEY_WINDOW), :], b1_ref[typ])
                merge(pl.ds(pl.multiple_of(c * Sc + a, Q_BLOCK), Q_BLOCK), pv, m, l, "last")
            return carry

        lax.fori_loop(0, nblk // ATTN_UNROLL, group, 0)

    @pl.when(phase == 0)
    def _():
        strided_phase(16, q16_ref, k16_ref, v16_ref, b16_ref, "first")

    @pl.when(phase == 1)
    def _():
        strided_phase(4, q4_ref, k4_ref, v4_ref, b4_ref, "mid")

    @pl.when(phase == 2)
    def _():
        contiguous_phase()


def _attention(qkv, B, S, C):
    q1, k1, v1, q4, k4, v4, q16, k16, v16 = qkv
    Sc = S // C
    assert Sc % (Q_BLOCK * ATTN_UNROLL) == 0 and 4 % C == 0
    halo_blocks = S // RADIUS

    def chunk_of(own_phase, p, c):
        return jnp.where(p < own_phase, 0, jnp.where(p > own_phase, C - 1, c))

    def cls_spec(d, own_phase):
        L = S // d
        assert L % Q_BLOCK == 0 and (L == Q_BLOCK or L >= KEY_WINDOW)
        return pl.BlockSpec((None, None, d // C, L, LANES),
                            lambda b, h, p, c: (b, h, chunk_of(own_phase, p, c), 0, 0))

    nat = pl.BlockSpec((None, None, Sc, LANES), lambda b, h, p, c: (b, h, chunk_of(2, p, c), 0))
    prev = pl.BlockSpec((None, None, RADIUS, LANES),
                        lambda b, h, p, c: (b, h, jnp.maximum(chunk_of(2, p, c) * (Sc // RADIUS) - 1, 0), 0))
    nxt = pl.BlockSpec((None, None, RADIUS, LANES),
                       lambda b, h, p, c: (b, h, jnp.minimum((chunk_of(2, p, c) + 1) * (Sc // RADIUS),
                                                              halo_blocks - 1), 0))
    c4, c16 = cls_spec(4, 1), cls_spec(16, 0)
    W4, W16 = min(KEY_WINDOW, S // 4), min(KEY_WINDOW, S // 16)
    clamped = (0, -RADIUS, -2 * RADIUS)
    b1 = _band_bias((-RADIUS,) * 3, KEY_WINDOW,
                    extra=(lambda kj: kj >= 0, lambda kj: kj >= RADIUS, lambda kj: kj < KEY_WINDOW - RADIUS))
    b4, b16 = _band_bias(clamped, W4), _band_bias(clamped, W16)
    return pl.pallas_call(
        functools.partial(_attn_kernel, S=S, C=C),
        grid=(B, N_HEAD_PAIRS, 3, C),
        in_specs=[nat, nat, prev, nxt, nat, prev, nxt, c4, c4, c4, c16, c16, c16,
                  _resident(b1.shape), _resident(b4.shape), _resident(b16.shape)],
        out_specs=pl.BlockSpec((None, None, S, LANES), lambda b, h, p, c: (b, h, 0, 0)),
        out_shape=jax.ShapeDtypeStruct((B, N_HEAD_PAIRS, S, LANES), F32),
        scratch_shapes=[pltpu.VMEM((S, LANES), F32), pltpu.VMEM((S, LANES), F32),
                        pltpu.VMEM((Sc + 2 * RADIUS, LANES), BF16), pltpu.VMEM((Sc + 2 * RADIUS, LANES), BF16)],
        compiler_params=_params(4),
        name="attn",
    )(q1, k1, k1, k1, v1, v1, v1, q4, k4, v4, q16, k16, v16, b1, b4, b16)


def _dft_kernel(bc_ref, nbs_ref, r1_ref, r2_ref, a_ref, o_ref):
    t = pl.program_id(2)

    @pl.when(t == 0)
    def _():
        o_ref[...] = jnp.zeros_like(o_ref)

    for par in range(2):
        bc, nbs = bc_ref[par], nbs_ref[par]
        r1, r2 = r1_ref[par], r2_ref[par]
        t_cos = (bc * r1 + nbs * r2).astype(BF16)
        t_nsin = (nbs * r1 - bc * r2).astype(BF16)
        o_ref[par] += _dot(t_cos, a_ref[par, 0]) + _dot(t_nsin, a_ref[par, 1])

    @pl.when(t == pl.num_programs(2) - 1)
    def _():
        f0, f1 = o_ref[0], o_ref[1]
        o_ref[0] = f0 + f1
        o_ref[1] = f0 - f1


def _seq_dft(a, S, ts, tk, tn):
    ncols = a.shape[-1]
    H = S // 2
    t = 2 * jnp.arange(H, dtype=jnp.int32)[None, None, :] + jnp.arange(2, dtype=jnp.int32)[:, None, None]
    w = 2.0 * math.pi / S
    ang_b = ((jnp.arange(ts, dtype=jnp.int32)[None, :, None] * t) % S).astype(F32) * w
    ang_r = ((jnp.arange(0, H, ts, dtype=jnp.int32)[None, :, None] * t) % S).astype(F32) * w
    bc, nbs = jnp.cos(ang_b), -jnp.sin(ang_b)
    scale = S ** -0.5
    r1 = (jnp.cos(ang_r) * scale).reshape(2, H // ts, 1, H)
    r2 = (jnp.sin(ang_r) * scale).reshape(2, H // ts, 1, H)
    btab = pl.BlockSpec((2, ts, tk), lambda s, n, t: (0, 0, t))
    rtab = pl.BlockSpec((2, None, 1, tk), lambda s, n, t: (0, s, 0, t))
    out = pl.pallas_call(
        _dft_kernel,
        grid=(H // ts, ncols // tn, H // tk),
        in_specs=[btab, btab, rtab, rtab, pl.BlockSpec((2, 2, tk, tn), lambda s, n, t: (0, 0, t, n))],
        out_specs=pl.BlockSpec((2, ts, tn), lambda s, n, t: (0, s, n)),
        out_shape=jax.ShapeDtypeStruct((2, H, ncols), F32),
        compiler_params=_params(3),
        name="seq_dft",
    )(bc, nbs, r1, r2, a)
    return out.reshape(S, ncols)


def _out_ffn_kernel(x1_ref, of_ref, oa_ref, gf_ref, ga_ref, wout_ref, g2_ref, wg_ref, wu_ref, wd_ref,
                    gfin_ref, y_ref):
    oa = jnp.concatenate([oa_ref[hp] for hp in range(N_HEAD_PAIRS)], axis=1)
    na = _rms(oa, ga_ref[...]).astype(BF16)
    nf = _rms(of_ref[...], gf_ref[...]).astype(BF16)
    x2 = x1_ref[...] + _dot(nf, wout_ref[:D_FOURIER, :]) + _dot(na, wout_ref[D_FOURIER:, :])
    h = _rms(x2, g2_ref[...]).astype(BF16)
    x3 = x2 + 0.5 * _swiglu(h, wg_ref, wu_ref, wd_ref)
    y_ref[...] = _rms(x3, gfin_ref[...])


def _out_ffn(x1, of, oa, B, S, gf, ga, wout, g2, wg, wu, wd, gfin, tm):
    N = B * S
    nst = S // tm
    tok = pl.BlockSpec((tm, D_MODEL), lambda i: (i, 0))
    att = pl.BlockSpec((None, N_HEAD_PAIRS, tm, LANES), lambda i: (i // nst, 0, i % nst, 0))
    return pl.pallas_call(
        _out_ffn_kernel,
        grid=(N // tm,),
        in_specs=[tok, pl.BlockSpec((tm, D_FOURIER), lambda i: (i % nst, i // nst)), att,
                  _resident((1, D_FOURIER)), _resident((1, D_ATTN)), _resident((D_MODEL, D_MODEL)),
                  _resident((1, D_MODEL)), _resident((D_MODEL, D_FF)), _resident((D_MODEL, D_FF)),
                  _resident((D_FF, D_MODEL)), _resident((1, D_MODEL))],
        out_specs=tok,
        out_shape=jax.ShapeDtypeStruct((N, D_MODEL), F32),
        compiler_params=_params(1),
        name="out_ffn",
    )(x1, of, oa, gf, ga, wout, g2, wg, wu, wd, gfin)


def _row(g):
    return g.reshape(1, -1).astype(F32)


def _trunk(x, p, win):
    B, S, _ = x.shape
    x1 = _ffn1(x.reshape(B * S, D_MODEL), _row(p["ffn1_norm"]), p["ffn1_w_gate"], p["ffn1_w_up"],
               p["ffn1_w_down"], tm=512)
    a, *qkv = _proj(x1, B, S, _row(p["mix_norm"]), win, tm=512)
    oa = _attention(qkv, B, S, C=max(1, S // 4096))
    of = _seq_dft(a, S, ts=512, tk=512, tn=min(2048, a.shape[-1]))
    y = _out_ffn(x1, of, oa, B, S, _row(p["fourier_out_norm"]), _row(p["attn_out_norm"]), p["w_out"],
                 _row(p["ffn2_norm"]), p["ffn2_w_gate"], p["ffn2_w_up"], p["ffn2_w_down"],
                 _row(p["final_norm"]), tm=512)
    return y.reshape(B, S, D_MODEL)


def kernel(x_prompt, x_sample, ffn1_norm, ffn1_w_gate, ffn1_w_up, ffn1_w_down, mix_norm, w_in, fourier_w,
           fourier_out_norm, attn_out_norm, w_out, ffn2_norm, ffn2_w_gate, ffn2_w_up, ffn2_w_down, final_norm):
    assert ffn1_w_gate.shape[0] == 1, "single-layer trunk"
    folded = _fold_fourier_weights(w_in[0, :, :D_FOURIER], fourier_w[0])
    win = jnp.concatenate([folded, w_in[0, :, D_FOURIER:].astype(BF16)], axis=1)
    p = dict(
        ffn1_norm=ffn1_norm[0], ffn1_w_gate=ffn1_w_gate[0].astype(BF16), ffn1_w_up=ffn1_w_up[0].astype(BF16),
        ffn1_w_down=ffn1_w_down[0].astype(BF16), mix_norm=mix_norm[0],
        fourier_out_norm=fourier_out_norm[0], attn_out_norm=attn_out_norm[0], w_out=w_out[0].astype(BF16),
        ffn2_norm=ffn2_norm[0], ffn2_w_gate=ffn2_w_gate[0].astype(BF16), ffn2_w_up=ffn2_w_up[0].astype(BF16),
        ffn2_w_down=ffn2_w_down[0].astype(BF16), final_norm=final_norm)
    return (_trunk(x_prompt, p, win), _trunk(x_sample, p, win))
```

```python
import functools
import math

import jax
import jax.numpy as jnp
import numpy as np
from jax import lax
from jax.experimental import pallas as pl
from jax.experimental.pallas import tpu as pltpu

F32 = jnp.float32
BF16 = jnp.bfloat16

D_MODEL = 1024
D_FF = 2816
D_FOURIER = 256
N_FOURIER_GROUPS = 4
FOURIER_GROUP_DIM = 64
D_ATTN = 768
HEAD_DIM = 64
LANES = 128
N_HEAD_PAIRS = D_ATTN // LANES
D_PROJ = 2 * D_FOURIER + 3 * D_ATTN
RADIUS = 64
ROPE_THETA = 10000.0
RMS_EPS = 1e-6
MASK_VALUE = -1e30

FF_CHUNK = 256
Q_BLOCK = 128
KEY_WINDOW = Q_BLOCK + 2 * RADIUS
LOG2_E = math.log2(math.e)
VMEM_LIMIT = 56 * 1024 * 1024


def _params(n_grid_dims):
    return pltpu.CompilerParams(dimension_semantics=("arbitrary",) * n_grid_dims,
                                vmem_limit_bytes=VMEM_LIMIT)


def _resident(shape):
    nd = len(shape)
    return pl.BlockSpec(shape, lambda *_: (0,) * nd, pipeline_mode=pl.Buffered(1))


def _rms(x, g):
    inv = lax.rsqrt(jnp.mean(x * x, axis=-1, keepdims=True) + RMS_EPS)
    return x * inv * g


def _dot(a, b):
    return jnp.dot(a, b, preferred_element_type=F32)


def _swiglu(h, wg_ref, wu_ref, wd_ref):
    acc = None
    for c in range(D_FF // FF_CHUNK):
        sl = slice(c * FF_CHUNK, (c + 1) * FF_CHUNK)
        g = _dot(h, wg_ref[:, sl])
        u = _dot(h, wu_ref[:, sl])
        a = (g * jax.nn.sigmoid(g) * u).astype(BF16)
        part = _dot(a, wd_ref[sl, :])
        acc = part if acc is None else acc + part
    return acc


def _fold_kernel(winf_ref, cbd_ref, sbd_ref, wbd_ref, o_ref):
    hi = lax.Precision.HIGHEST
    scale = FOURIER_GROUP_DIM ** -0.5
    gc = jnp.dot(cbd_ref[...], wbd_ref[...], precision=hi, preferred_element_type=F32) * scale
    gs = jnp.dot(sbd_ref[...], wbd_ref[...], precision=hi, preferred_element_type=F32) * scale
    winf = winf_ref[...]
    o_ref[:, :D_FOURIER] = jnp.dot(winf, gc, precision=hi, preferred_element_type=F32).astype(BF16)
    o_ref[:, D_FOURIER:] = jnp.dot(winf, gs, precision=hi, preferred_element_type=F32).astype(BF16)


def _fold_fourier_weights(w_in_f, fourier_w):
    c = np.arange(FOURIER_GROUP_DIM)
    ang = 2.0 * np.pi * ((c[:, None] * c[None, :]) % FOURIER_GROUP_DIM) / FOURIER_GROUP_DIM
    eye = np.eye(N_FOURIER_GROUPS)
    cbd = jnp.asarray(np.kron(eye, np.cos(ang)), F32)
    sbd = jnp.asarray(np.kron(eye, np.sin(ang)), F32)
    wbd = jnp.zeros((D_FOURIER, D_FOURIER), F32)
    for g in range(N_FOURIER_GROUPS):
        s = g * FOURIER_GROUP_DIM
        wbd = lax.dynamic_update_slice(wbd, fourier_w[g].astype(F32), (s, s))
    return pl.pallas_call(
        _fold_kernel,
        out_shape=jax.ShapeDtypeStruct((D_MODEL, 2 * D_FOURIER), BF16),
        name="fold_fourier",
    )(w_in_f, cbd, sbd, wbd)


def _ffn1_kernel(x_ref, g1_ref, wg_ref, wu_ref, wd_ref, x1_ref):
    x = x_ref[...]
    h = _rms(x, g1_ref[...]).astype(BF16)
    x1_ref[...] = x + 0.5 * _swiglu(h, wg_ref, wu_ref, wd_ref)


def _ffn1(xf, g1, wg, wu, wd, tm):
    N = xf.shape[0]
    tok = pl.BlockSpec((tm, D_MODEL), lambda i: (i, 0))
    return pl.pallas_call(
        _ffn1_kernel,
        grid=(N // tm,),
        in_specs=[tok, _resident((1, D_MODEL)), _resident((D_MODEL, D_FF)), _resident((D_MODEL, D_FF)),
                  _resident((D_FF, D_MODEL))],
        out_specs=tok,
        out_shape=jax.ShapeDtypeStruct((N, D_MODEL), F32),
        compiler_params=_params(1),
        name="ffn1",
    )(xf, g1, wg, wu, wd)


def _rope(t, cos, sin_lo, sin_hi):
    return (t * cos + pltpu.roll(t, LANES - HEAD_DIM // 2, axis=1) * sin_lo
            + pltpu.roll(t, HEAD_DIM // 2, axis=1) * sin_hi)


def _proj_kernel(x1_ref, gm_ref, win_ref, cos_ref, slo_ref, shi_ref,
                 a_ref, q1_ref, k1_ref, v1_ref, q4_ref, k4_ref, v4_ref, q16_ref, k16_ref, v16_ref,
                 nat_ref, cls4_ref, pa_ref):
    tm = x1_ref.shape[0]
    h2 = _rms(x1_ref[...], gm_ref[...]).astype(BF16)

    pa = _dot(h2, win_ref[:, :2 * D_FOURIER])
    tiles_per_part = D_FOURIER // LANES
    for j in range(2 * tiles_per_part):
        pa_ref[j] = pa[:, j * LANES:(j + 1) * LANES]
    for par in range(2):
        for j in range(2 * tiles_per_part):
            lanes = slice((j % tiles_per_part) * LANES, (j % tiles_per_part + 1) * LANES)
            a_ref[par, j // tiles_per_part, :, lanes] = pa_ref[j, pl.ds(par, tm // 2, stride=2), :].astype(BF16)

    def emit(t, hp, o1_ref, o4_ref, o16_ref):
        o1_ref[hp] = t.astype(BF16)
        nat_ref[...] = t
        for r4 in range(4):
            c4 = nat_ref[pl.ds(r4, tm // 4, stride=4), :]
            o4_ref[hp, r4] = c4.astype(BF16)
            cls4_ref[...] = c4
            for j in range(4):
                o16_ref[hp, r4 + 4 * j] = cls4_ref[pl.ds(j, tm // 16, stride=4), :].astype(BF16)

    cos = cos_ref[...]
    slo = slo_ref[...]
    shi = shi_ref[...]
    base = 2 * D_FOURIER
    for j in range(D_ATTN // 256):
        c0 = j * 256
        qq = _dot(h2, win_ref[:, base + c0: base + c0 + 256])
        kk = _dot(h2, win_ref[:, base + D_ATTN + c0: base + D_ATTN + c0 + 256])
        vv = _dot(h2, win_ref[:, base + 2 * D_ATTN + c0: base + 2 * D_ATTN + c0 + 256])
        for i in range(2):
            sl = slice(i * LANES, (i + 1) * LANES)
            hp = 2 * j + i
            emit(_rope(qq[:, sl], cos, slo, shi) * (HEAD_DIM ** -0.5 * LOG2_E), hp, q1_ref, q4_ref, q16_ref)
            emit(_rope(kk[:, sl], cos, slo, shi), hp, k1_ref, k4_ref, k16_ref)
            emit(vv[:, sl], hp, v1_ref, v4_ref, v16_ref)


def _rope_tables(S):
    inv_freq = ROPE_THETA ** (-jnp.arange(0, HEAD_DIM, 2, dtype=F32) / HEAD_DIM)
    freqs = jnp.arange(S, dtype=F32)[:, None] * inv_freq[None, :]
    cos, sin = jnp.cos(freqs), jnp.sin(freqs)
    zero = jnp.zeros_like(sin)
    cos_t = jnp.tile(cos, (1, LANES // (HEAD_DIM // 2)))
    sin_lo = jnp.tile(jnp.concatenate([-sin, zero], axis=1), (1, LANES // HEAD_DIM))
    sin_hi = jnp.tile(jnp.concatenate([zero, sin], axis=1), (1, LANES // HEAD_DIM))
    return cos_t, sin_lo, sin_hi


def _proj(x1, B, S, gm, win, tm):
    N = B * S
    nst = S // tm
    cos_t, sin_lo, sin_hi = _rope_tables(S)
    tok = pl.BlockSpec((tm, D_MODEL), lambda i: (i, 0))
    tab = pl.BlockSpec((tm, LANES), lambda i: (i % nst, 0))
    nat = pl.BlockSpec((None, N_HEAD_PAIRS, tm, LANES), lambda i: (i // nst, 0, i % nst, 0))
    nat_shape = jax.ShapeDtypeStruct((B, N_HEAD_PAIRS, S, LANES), BF16)

    def cls(d):
        spec = pl.BlockSpec((None, N_HEAD_PAIRS, d, tm // d, LANES), lambda i: (i // nst, 0, 0, i % nst, 0))
        return spec, jax.ShapeDtypeStruct((B, N_HEAD_PAIRS, d, S // d, LANES), BF16)

    c4, c4_shape = cls(4)
    c16, c16_shape = cls(16)
    return pl.pallas_call(
        _proj_kernel,
        grid=(N // tm,),
        in_specs=[tok, _resident((1, D_MODEL)), _resident((D_MODEL, D_PROJ)), tab, tab, tab],
        out_specs=[pl.BlockSpec((2, 2, tm // 2, D_FOURIER), lambda i: (0, 0, i % nst, i // nst)),
                   nat, nat, nat, c4, c4, c4, c16, c16, c16],
        out_shape=[jax.ShapeDtypeStruct((2, 2, S // 2, B * D_FOURIER), BF16),
                   nat_shape, nat_shape, nat_shape, c4_shape, c4_shape, c4_shape,
                   c16_shape, c16_shape, c16_shape],
        scratch_shapes=[pltpu.VMEM((tm, LANES), F32), pltpu.VMEM((tm // 4, LANES), F32),
                        pltpu.VMEM((2 * D_FOURIER // LANES, tm, LANES), F32)],
        compiler_params=_params(1),
        name="proj",
    )(x1, gm, win, cos_t, sin_lo, sin_hi)


def _band_bias(offsets, W, extra=None):
    qi = np.arange(Q_BLOCK)[:, None]
    kj = np.arange(W)[None, :]
    tiles = []
    for t, off in enumerate(offsets):
        ok = np.abs(kj + off - qi) <= RADIUS
        if extra is not None:
            ok &= extra[t](kj)
        tiles.append(np.where(ok, 0.0, MASK_VALUE))
    return jnp.asarray(np.stack(tiles), F32)


def _attn_kernel(q1_ref, k1_ref, k1p_ref, k1n_ref, v1_ref, v1p_ref, v1n_ref,
                 q4_ref, k4_ref, v4_ref, q16_ref, k16_ref, v16_ref,
                 b1_ref, b4_ref, b16_ref,
                 o_ref, m_ref, l_ref, kwin_ref, vwin_ref, *, S, C):
    phase = pl.program_id(2)
    c = pl.program_id(3)
    lane = lax.broadcasted_iota(jnp.int32, (1, LANES), 1)
    first_head = lane < HEAD_DIM
    contract_last = (((1,), (1,)), ((), ()))

    def block_stats(q2, k2, v2, bias):
        v_ones = jnp.concatenate([v2, jnp.ones_like(v2)], axis=1)
        per_head = []
        for head_mask in (first_head, jnp.logical_not(first_head)):
            qh = jnp.where(head_mask, q2, jnp.zeros_like(q2))
            s = lax.dot_general(qh, k2, contract_last, preferred_element_type=F32) + bias
            m = jnp.max(s, axis=-1, keepdims=True)
            p = jnp.exp2(s - m).astype(BF16)
            pvl = _dot(p, v_ones)
            per_head.append((pvl[:, :LANES], jnp.broadcast_to(m, (Q_BLOCK, LANES)), pvl[:, LANES:]))
        return tuple(jnp.where(first_head, a, b) for a, b in zip(*per_head))

    def merge(rows, pv, m, l, mode):
        if mode == "first":
            o_ref[rows, :] = pv
            m_ref[rows, :] = m
            l_ref[rows, :] = l
            return
        m_old = m_ref[rows, :]
        m_new = jnp.maximum(m_old, m)
        a_old = jnp.exp2(m_old - m_new)
        a_blk = jnp.exp2(m - m_new)
        acc = a_old * o_ref[rows, :] + a_blk * pv
        den = a_old * l_ref[rows, :] + a_blk * l
        if mode == "last":
            o_ref[rows, :] = acc / den
        else:
            o_ref[rows, :] = acc
            m_ref[rows, :] = m_new
            l_ref[rows, :] = den

    def strided_phase(d, q_ref, k_ref, v_ref, bias_ref, mode):
        L = S // d
        nblk = L // Q_BLOCK
        W = min(KEY_WINDOW, L)
        cpc = d // C
        for rl in range(cpc):
            for i in range(nblk):
                a = i * Q_BLOCK
                ws = min(max(a - RADIUS, 0), L - W)
                typ = 0 if i == 0 else (2 if i == nblk - 1 else 1)
                k2 = k_ref[rl, ws:ws + W, :]
                v2 = v_ref[rl, ws:ws + W, :]
                bias = bias_ref[typ]
                if W < KEY_WINDOW:
                    pad = jnp.zeros((KEY_WINDOW - W, LANES), BF16)
                    k2 = jnp.concatenate([k2, pad], axis=0)
                    v2 = jnp.concatenate([v2, pad], axis=0)
                    bias = jnp.concatenate([bias, jnp.full((Q_BLOCK, KEY_WINDOW - W), MASK_VALUE, F32)], axis=1)
                pv, m, l = block_stats(q_ref[rl, a:a + Q_BLOCK, :], k2, v2, bias)
                merge(pl.ds(a * d + c * cpc + rl, Q_BLOCK, stride=d), pv, m, l, mode)

    def contiguous_phase():
        Sc = S // C
        nblk = Sc // Q_BLOCK
        for win_ref, prev_ref, cur_ref, next_ref in ((kwin_ref, k1p_ref, k1_ref, k1n_ref),
                                                     (vwin_ref, v1p_ref, v1_ref, v1n_ref)):
            win_ref[0:RADIUS, :] = prev_ref[...]
            win_ref[RADIUS:RADIUS + Sc, :] = cur_ref[...]
            win_ref[RADIUS + Sc:, :] = next_ref[...]

        for i in range(nblk):
            a = i * Q_BLOCK
            typ = 0
            if i == 0:
                typ = jnp.where(c == 0, 1, typ)
            if i == nblk - 1:
                typ = jnp.where(c == C - 1, 2, typ)
            pv, m, l = block_stats(q1_ref[a:a + Q_BLOCK, :], kwin_ref[a:a + KEY_WINDOW, :],
                                   vwin_ref[a:a + KEY_WINDOW, :], b1_ref[typ])
            merge(pl.ds(pl.multiple_of(c * Sc + a, Q_BLOCK), Q_BLOCK), pv, m, l, "last")

    @pl.when(phase == 0)
    def _():
        strided_phase(16, q16_ref, k16_ref, v16_ref, b16_ref, "first")

    @pl.when(phase == 1)
    def _():
        strided_phase(4, q4_ref, k4_ref, v4_ref, b4_ref, "mid")

    @pl.when(phase == 2)
    def _():
        contiguous_phase()


def _attention(qkv, B, S, C):
    q1, k1, v1, q4, k4, v4, q16, k16, v16 = qkv
    Sc = S // C
    assert Sc % Q_BLOCK == 0 and 4 % C == 0
    halo_blocks = S // RADIUS

    def chunk_of(own_phase, p, c):
        return jnp.where(p < own_phase, 0, jnp.where(p > own_phase, C - 1, c))

    def cls_spec(d, own_phase):
        L = S // d
        assert L % Q_BLOCK == 0 and (L == Q_BLOCK or L >= KEY_WINDOW)
        return pl.BlockSpec((None, None, d // C, L, LANES),
                            lambda b, h, p, c: (b, h, chunk_of(own_phase, p, c), 0, 0))

    nat = pl.BlockSpec((None, None, Sc, LANES), lambda b, h, p, c: (b, h, chunk_of(2, p, c), 0))
    prev = pl.BlockSpec((None, None, RADIUS, LANES),
                        lambda b, h, p, c: (b, h, jnp.maximum(chunk_of(2, p, c) * (Sc // RADIUS) - 1, 0), 0))
    nxt = pl.BlockSpec((None, None, RADIUS, LANES),
                       lambda b, h, p, c: (b, h, jnp.minimum((chunk_of(2, p, c) + 1) * (Sc // RADIUS),
                                                              halo_blocks - 1), 0))
    c4, c16 = cls_spec(4, 1), cls_spec(16, 0)
    W4, W16 = min(KEY_WINDOW, S // 4), min(KEY_WINDOW, S // 16)
    clamped = (0, -RADIUS, -2 * RADIUS)
    b1 = _band_bias((-RADIUS,) * 3, KEY_WINDOW,
                    extra=(lambda kj: kj >= 0, lambda kj: kj >= RADIUS, lambda kj: kj < KEY_WINDOW - RADIUS))
    b4, b16 = _band_bias(clamped, W4), _band_bias(clamped, W16)
    return pl.pallas_call(
        functools.partial(_attn_kernel, S=S, C=C),
        grid=(B, N_HEAD_PAIRS, 3, C),
        in_specs=[nat, nat, prev, nxt, nat, prev, nxt, c4, c4, c4, c16, c16, c16,
                  _resident(b1.shape), _resident(b4.shape), _resident(b16.shape)],
        out_specs=pl.BlockSpec((None, None, S, LANES), lambda b, h, p, c: (b, h, 0, 0)),
        out_shape=jax.ShapeDtypeStruct((B, N_HEAD_PAIRS, S, LANES), F32),
        scratch_shapes=[pltpu.VMEM((S, LANES), F32), pltpu.VMEM((S, LANES), F32),
                        pltpu.VMEM((Sc + 2 * RADIUS, LANES), BF16), pltpu.VMEM((Sc + 2 * RADIUS, LANES), BF16)],
        compiler_params=_params(4),
        name="attn",
    )(q1, k1, k1, k1, v1, v1, v1, q4, k4, v4, q16, k16, v16, b1, b4, b16)


def _dft_kernel(bc_ref, nbs_ref, r1_ref, r2_ref, a_ref, o_ref):
    t = pl.program_id(2)

    @pl.when(t == 0)
    def _():
        o_ref[...] = jnp.zeros_like(o_ref)

    for par in range(2):
        bc, nbs = bc_ref[par], nbs_ref[par]
        r1, r2 = r1_ref[par], r2_ref[par]
        t_cos = (bc * r1 + nbs * r2).astype(BF16)
        t_nsin = (nbs * r1 - bc * r2).astype(BF16)
        o_ref[par] += _dot(t_cos, a_ref[par, 0]) + _dot(t_nsin, a_ref[par, 1])

    @pl.when(t == pl.num_programs(2) - 1)
    def _():
        f0, f1 = o_ref[0], o_ref[1]
        o_ref[0] = f0 + f1
        o_ref[1] = f0 - f1


def _seq_dft(a, S, ts, tk, tn):
    ncols = a.shape[-1]
    H = S // 2
    t = 2 * jnp.arange(H, dtype=jnp.int32)[None, None, :] + jnp.arange(2, dtype=jnp.int32)[:, None, None]
    w = 2.0 * math.pi / S
    ang_b = ((jnp.arange(ts, dtype=jnp.int32)[None, :, None] * t) % S).astype(F32) * w
    ang_r = ((jnp.arange(0, H, ts, dtype=jnp.int32)[None, :, None] * t) % S).astype(F32) * w
    bc, nbs = jnp.cos(ang_b), -jnp.sin(ang_b)
    scale = S ** -0.5
    r1 = (jnp.cos(ang_r) * scale).reshape(2, H // ts, 1, H)
    r2 = (jnp.sin(ang_r) * scale).reshape(2, H // ts, 1, H)
    btab = pl.BlockSpec((2, ts, tk), lambda s, n, t: (0, 0, t))
    rtab = pl.BlockSpec((2, None, 1, tk), lambda s, n, t: (0, s, 0, t))
    out = pl.pallas_call(
        _dft_kernel,
        grid=(H // ts, ncols // tn, H // tk),
        in_specs=[btab, btab, rtab, rtab, pl.BlockSpec((2, 2, tk, tn), lambda s, n, t: (0, 0, t, n))],
        out_specs=pl.BlockSpec((2, ts, tn), lambda s, n, t: (0, s, n)),
        out_shape=jax.ShapeDtypeStruct((2, H, ncols), F32),
        compiler_params=_params(3),
        name="seq_dft",
    )(bc, nbs, r1, r2, a)
    return out.reshape(S, ncols)


def _out_ffn_kernel(x1_ref, of_ref, oa_ref, gf_ref, ga_ref, wout_ref, g2_ref, wg_ref, wu_ref, wd_ref,
                    gfin_ref, y_ref):
    oa = jnp.concatenate([oa_ref[hp] for hp in range(N_HEAD_PAIRS)], axis=1)
    na = _rms(oa, ga_ref[...]).astype(BF16)
    nf = _rms(of_ref[...], gf_ref[...]).astype(BF16)
    x2 = x1_ref[...] + _dot(nf, wout_ref[:D_FOURIER, :]) + _dot(na, wout_ref[D_FOURIER:, :])
    h = _rms(x2, g2_ref[...]).astype(BF16)
    x3 = x2 + 0.5 * _swiglu(h, wg_ref, wu_ref, wd_ref)
    y_ref[...] = _rms(x3, gfin_ref[...])


def _out_ffn(x1, of, oa, B, S, gf, ga, wout, g2, wg, wu, wd, gfin, tm):
    N = B * S
    nst = S // tm
    tok = pl.BlockSpec((tm, D_MODEL), lambda i: (i, 0))
    att = pl.BlockSpec((None, N_HEAD_PAIRS, tm, LANES), lambda i: (i // nst, 0, i % nst, 0))
    return pl.pallas_call(
        _out_ffn_kernel,
        grid=(N // tm,),
        in_specs=[tok, pl.BlockSpec((tm, D_FOURIER), lambda i: (i % nst, i // nst)), att,
                  _resident((1, D_FOURIER)), _resident((1, D_ATTN)), _resident((D_MODEL, D_MODEL)),
                  _resident((1, D_MODEL)), _resident((D_MODEL, D_FF)), _resident((D_MODEL, D_FF)),
                  _resident((D_FF, D_MODEL)), _resident((1, D_MODEL))],
        out_specs=tok,
        out_shape=jax.ShapeDtypeStruct((N, D_MODEL), F32),
        compiler_params=_params(1),
        name="out_ffn",
    )(x1, of, oa, gf, ga, wout, g2, wg, wu, wd, gfin)


def _row(g):
    return g.reshape(1, -1).astype(F32)


def _trunk(x, p, win):
    B, S, _ = x.shape
    x1 = _ffn1(x.reshape(B * S, D_MODEL), _row(p["ffn1_norm"]), p["ffn1_w_gate"], p["ffn1_w_up"],
               p["ffn1_w_down"], tm=512)
    a, *qkv = _proj(x1, B, S, _row(p["mix_norm"]), win, tm=512)
    oa = _attention(qkv, B, S, C=max(1, S // 4096))
    of = _seq_dft(a, S, ts=512, tk=512, tn=min(2048, a.shape[-1]))
    y = _out_ffn(x1, of, oa, B, S, _row(p["fourier_out_norm"]), _row(p["attn_out_norm"]), p["w_out"],
                 _row(p["ffn2_norm"]), p["ffn2_w_gate"], p["ffn2_w_up"], p["ffn2_w_down"],
                 _row(p["final_norm"]), tm=512)
    return y.reshape(B, S, D_MODEL)


def kernel(x_prompt, x_sample, ffn1_norm, ffn1_w_gate, ffn1_w_up, ffn1_w_down, mix_norm, w_in, fourier_w,
           fourier_out_norm, attn_out_norm, w_out, ffn2_norm, ffn2_w_gate, ffn2_w_up, ffn2_w_down, final_norm):
    assert ffn1_w_gate.shape[0] == 1, "single-layer trunk"
    folded = _fold_fourier_weights(w_in[0, :, :D_FOURIER], fourier_w[0])
    win = jnp.concatenate([folded, w_in[0, :, D_FOURIER:].astype(BF16)], axis=1)
    p = dict(
        ffn1_norm=ffn1_norm[0], ffn1_w_gate=ffn1_w_gate[0].astype(BF16), ffn1_w_up=ffn1_w_up[0].astype(BF16),
        ffn1_w_down=ffn1_w_down[0].astype(BF16), mix_norm=mix_norm[0],
        fourier_out_norm=fourier_out_norm[0], attn_out_norm=attn_out_norm[0], w_out=w_out[0].astype(BF16),
        ffn2_norm=ffn2_norm[0], ffn2_w_gate=ffn2_w_gate[0].astype(BF16), ffn2_w_up=ffn2_w_up[0].astype(BF16),
        ffn2_w_down=ffn2_w_down[0].astype(BF16), final_norm=final_norm)
    return (_trunk(x_prompt, p, win), _trunk(x_sample, p, win))
```

```python
import functools
import math

import jax
import jax.numpy as jnp
import numpy as np
from jax import lax
from jax.experimental import pallas as pl
from jax.experimental.pallas import tpu as pltpu

F32 = jnp.float32
BF16 = jnp.bfloat16

D_MODEL = 1024
D_FF = 2816
D_FOURIER = 256
N_FOURIER_GROUPS = 4
FOURIER_GROUP_DIM = 64
D_ATTN = 768
HEAD_DIM = 64
LANES = 128
N_HEAD_PAIRS = D_ATTN // LANES
D_PROJ = 2 * D_FOURIER + 3 * D_ATTN
RADIUS = 64
ROPE_THETA = 10000.0
RMS_EPS = 1e-6
MASK_VALUE = -1e30

FF_CHUNK = 256
Q_BLOCK = 128
KEY_WINDOW = Q_BLOCK + 2 * RADIUS
LOG2_E = math.log2(math.e)
VMEM_LIMIT = 56 * 1024 * 1024


def _params(n_grid_dims):
    return pltpu.CompilerParams(dimension_semantics=("arbitrary",) * n_grid_dims,
                                vmem_limit_bytes=VMEM_LIMIT)


def _resident(shape):
    nd = len(shape)
    return pl.BlockSpec(shape, lambda *_: (0,) * nd, pipeline_mode=pl.Buffered(1))


def _rms(x, g):
    inv = lax.rsqrt(jnp.mean(x * x, axis=-1, keepdims=True) + RMS_EPS)
    return x * inv * g


def _dot(a, b):
    return jnp.dot(a, b, preferred_element_type=F32)


def _swiglu(h, wg_ref, wu_ref, wd_ref):
    acc = None
    for c in range(D_FF // FF_CHUNK):
        sl = slice(c * FF_CHUNK, (c + 1) * FF_CHUNK)
        g = _dot(h, wg_ref[:, sl])
        u = _dot(h, wu_ref[:, sl])
        a = (g * jax.nn.sigmoid(g) * u).astype(BF16)
        part = _dot(a, wd_ref[sl, :])
        acc = part if acc is None else acc + part
    return acc


def _fold_kernel(winf_ref, cbd_ref, sbd_ref, wbd_ref, o_ref):
    hi = lax.Precision.HIGHEST
    scale = FOURIER_GROUP_DIM ** -0.5
    gc = jnp.dot(cbd_ref[...], wbd_ref[...], precision=hi, preferred_element_type=F32) * scale
    gs = jnp.dot(sbd_ref[...], wbd_ref[...], precision=hi, preferred_element_type=F32) * scale
    winf = winf_ref[...]
    o_ref[:, :D_FOURIER] = jnp.dot(winf, gc, precision=hi, preferred_element_type=F32).astype(BF16)
    o_ref[:, D_FOURIER:] = jnp.dot(winf, gs, precision=hi, preferred_element_type=F32).astype(BF16)


def _fold_fourier_weights(w_in_f, fourier_w):
    c = np.arange(FOURIER_GROUP_DIM)
    ang = 2.0 * np.pi * ((c[:, None] * c[None, :]) % FOURIER_GROUP_DIM) / FOURIER_GROUP_DIM
    eye = np.eye(N_FOURIER_GROUPS)
    cbd = jnp.asarray(np.kron(eye, np.cos(ang)), F32)
    sbd = jnp.asarray(np.kron(eye, np.sin(ang)), F32)
    wbd = jnp.zeros((D_FOURIER, D_FOURIER), F32)
    for g in range(N_FOURIER_GROUPS):
        s = g * FOURIER_GROUP_DIM
        wbd = lax.dynamic_update_slice(wbd, fourier_w[g].astype(F32), (s, s))
    return pl.pallas_call(
        _fold_kernel,
        out_shape=jax.ShapeDtypeStruct((D_MODEL, 2 * D_FOURIER), BF16),
        name="fold_fourier",
    )(w_in_f, cbd, sbd, wbd)


def _ffn1_kernel(x_ref, g1_ref, wg_ref, wu_ref, wd_ref, x1_ref):
    x = x_ref[...]
    h = _rms(x, g1_ref[...]).astype(BF16)
    x1_ref[...] = x + 0.5 * _swiglu(h, wg_ref, wu_ref, wd_ref)


def _ffn1(xf, g1, wg, wu, wd, tm):
    N = xf.shape[0]
    tok = pl.BlockSpec((tm, D_MODEL), lambda i: (i, 0))
    return pl.pallas_call(
        _ffn1_kernel,
        grid=(N // tm,),
        in_specs=[tok, _resident((1, D_MODEL)), _resident((D_MODEL, D_FF)), _resident((D_MODEL, D_FF)),
                  _resident((D_FF, D_MODEL))],
        out_specs=tok,
        out_shape=jax.ShapeDtypeStruct((N, D_MODEL), F32),
        compiler_params=_params(1),
        name="ffn1",
    )(xf, g1, wg, wu, wd)


def _rope(t, cos, sin_lo, sin_hi):
    return (t * cos + pltpu.roll(t, LANES - HEAD_DIM // 2, axis=1) * sin_lo
            + pltpu.roll(t, HEAD_DIM // 2, axis=1) * sin_hi)


def _proj_kernel(x1_ref, gm_ref, win_ref, cos_ref, slo_ref, shi_ref,
                 a_ref, o1_ref, o4_ref, o16_ref,
                 nat_ref, cls4_ref, pa_ref):
    tm = x1_ref.shape[0]
    h2 = _rms(x1_ref[...], gm_ref[...]).astype(BF16)

    pa = _dot(h2, win_ref[:, :2 * D_FOURIER])
    tiles_per_part = D_FOURIER // LANES
    for j in range(2 * tiles_per_part):
        pa_ref[j] = pa[:, j * LANES:(j + 1) * LANES]
    for par in range(2):
        for j in range(2 * tiles_per_part):
            lanes = slice((j % tiles_per_part) * LANES, (j % tiles_per_part + 1) * LANES)
            a_ref[par, j // tiles_per_part, :, lanes] = pa_ref[j, pl.ds(par, tm // 2, stride=2), :].astype(BF16)

    def emit(t, hp, slot):
        o1_ref[hp, slot] = t.astype(BF16)
        nat_ref[...] = t
        for r4 in range(4):
            c4 = nat_ref[pl.ds(r4, tm // 4, stride=4), :]
            o4_ref[hp, slot, r4] = c4.astype(BF16)
            cls4_ref[...] = c4
            for j in range(4):
                o16_ref[hp, slot, r4 + 4 * j] = cls4_ref[pl.ds(j, tm // 16, stride=4), :].astype(BF16)

    cos = cos_ref[...]
    slo = slo_ref[...]
    shi = shi_ref[...]
    base = 2 * D_FOURIER
    for j in range(D_ATTN // 256):
        c0 = j * 256
        qq = _dot(h2, win_ref[:, base + c0: base + c0 + 256])
        kk = _dot(h2, win_ref[:, base + D_ATTN + c0: base + D_ATTN + c0 + 256])
        vv = _dot(h2, win_ref[:, base + 2 * D_ATTN + c0: base + 2 * D_ATTN + c0 + 256])
        for i in range(2):
            sl = slice(i * LANES, (i + 1) * LANES)
            hp = 2 * j + i
            emit(_rope(qq[:, sl], cos, slo, shi) * (HEAD_DIM ** -0.5 * LOG2_E), hp, Q_SLOT)
            emit(_rope(kk[:, sl], cos, slo, shi), hp, K_SLOT)
            emit(vv[:, sl], hp, V_SLOT)


def _rope_tables(S):
    inv_freq = ROPE_THETA ** (-jnp.arange(0, HEAD_DIM, 2, dtype=F32) / HEAD_DIM)
    freqs = jnp.arange(S, dtype=F32)[:, None] * inv_freq[None, :]
    cos, sin = jnp.cos(freqs), jnp.sin(freqs)
    zero = jnp.zeros_like(sin)
    cos_t = jnp.tile(cos, (1, LANES // (HEAD_DIM // 2)))
    sin_lo = jnp.tile(jnp.concatenate([-sin, zero], axis=1), (1, LANES // HEAD_DIM))
    sin_hi = jnp.tile(jnp.concatenate([zero, sin], axis=1), (1, LANES // HEAD_DIM))
    return cos_t, sin_lo, sin_hi


def _proj(x1, B, S, gm, win, tm):
    N = B * S
    nst = S // tm
    cos_t, sin_lo, sin_hi = _rope_tables(S)
    tok = pl.BlockSpec((tm, D_MODEL), lambda i: (i, 0))
    tab = pl.BlockSpec((tm, LANES), lambda i: (i % nst, 0))
    nat = pl.BlockSpec((None, N_HEAD_PAIRS, 3, tm, LANES), lambda i: (i // nst, 0, 0, i % nst, 0))
    nat_shape = jax.ShapeDtypeStruct((B, N_HEAD_PAIRS, 3, S, LANES), BF16)

    def cls(d):
        spec = pl.BlockSpec((None, N_HEAD_PAIRS, 3, d, tm // d, LANES),
                            lambda i: (i // nst, 0, 0, 0, i % nst, 0))
        return spec, jax.ShapeDtypeStruct((B, N_HEAD_PAIRS, 3, d, S // d, LANES), BF16)

    c4, c4_shape = cls(4)
    c16, c16_shape = cls(16)
    return pl.pallas_call(
        _proj_kernel,
        grid=(N // tm,),
        in_specs=[tok, _resident((1, D_MODEL)), _resident((D_MODEL, D_PROJ)), tab, tab, tab],
        out_specs=[pl.BlockSpec((2, 2, tm // 2, D_FOURIER), lambda i: (0, 0, i % nst, i // nst)),
                   nat, c4, c16],
        out_shape=[jax.ShapeDtypeStruct((2, 2, S // 2, B * D_FOURIER), BF16), nat_shape, c4_shape, c16_shape],
        scratch_shapes=[pltpu.VMEM((tm, LANES), F32), pltpu.VMEM((tm // 4, LANES), F32),
                        pltpu.VMEM((2 * D_FOURIER // LANES, tm, LANES), F32)],
        compiler_params=_params(1),
        name="proj",
    )(x1, gm, win, cos_t, sin_lo, sin_hi)


def _band_bias(offsets, W, extra=None):
    qi = np.arange(Q_BLOCK)[:, None]
    kj = np.arange(W)[None, :]
    tiles = []
    for t, off in enumerate(offsets):
        ok = np.abs(kj + off - qi) <= RADIUS
        if extra is not None:
            ok &= extra[t](kj)
        tiles.append(np.where(ok, 0.0, MASK_VALUE))
    return jnp.asarray(np.stack(tiles), F32)


K_SLOT, V_SLOT, Q_SLOT = 0, 1, 2


def _attn_kernel(nat_ref, prev_ref, next_ref, c4_ref, c16_ref, b1_ref, b4_ref, b16_ref,
                 o_ref, m_ref, l_ref, *, S, C, NB):
    phase = pl.program_id(2)
    c = pl.program_id(3)
    lane = lax.broadcasted_iota(jnp.int32, (1, LANES), 1)
    first_head = lane < HEAD_DIM
    contract_last = (((1,), (1,)), ((), ()))

    def block_stats(q2, k2, v2, bias):
        v_ones = jnp.concatenate([v2, jnp.ones_like(v2)], axis=1)
        per_head = []
        for head_mask in (first_head, jnp.logical_not(first_head)):
            qh = jnp.where(head_mask, q2, jnp.zeros_like(q2))
            s = lax.dot_general(qh, k2, contract_last, preferred_element_type=F32) + bias
            m = jnp.max(s, axis=-1, keepdims=True)
            p = jnp.exp2(s - m).astype(BF16)
            pvl = _dot(p, v_ones)
            per_head.append((pvl[:, :LANES], jnp.broadcast_to(m, (Q_BLOCK, LANES)), pvl[:, LANES:]))
        return tuple(jnp.where(first_head, a, b) for a, b in zip(*per_head))

    def merge(bb, rows, pv, m, l, mode):
        if mode == "first":
            o_ref[bb, rows, :] = pv
            m_ref[bb, rows, :] = m
            l_ref[bb, rows, :] = l
            return
        m_old = m_ref[bb, rows, :]
        m_new = jnp.maximum(m_old, m)
        a_old = jnp.exp2(m_old - m_new)
        a_blk = jnp.exp2(m - m_new)
        acc = a_old * o_ref[bb, rows, :] + a_blk * pv
        den = a_old * l_ref[bb, rows, :] + a_blk * l
        if mode == "last":
            o_ref[bb, rows, :] = acc / den
        else:
            o_ref[bb, rows, :] = acc
            m_ref[bb, rows, :] = m_new
            l_ref[bb, rows, :] = den

    def strided_phase(d, cls_ref, bias_ref, mode):
        L = S // d
        nblk = L // Q_BLOCK
        W = min(KEY_WINDOW, L)
        cpc = d // C
        for bb in range(NB):
            for rl in range(cpc):
                for i in range(nblk):
                    a = i * Q_BLOCK
                    ws = min(max(a - RADIUS, 0), L - W)
                    typ = 0 if i == 0 else (2 if i == nblk - 1 else 1)
                    k2 = cls_ref[bb, K_SLOT, rl, ws:ws + W, :]
                    v2 = cls_ref[bb, V_SLOT, rl, ws:ws + W, :]
                    bias = bias_ref[typ]
                    if W < KEY_WINDOW:
                        pad = jnp.zeros((KEY_WINDOW - W, LANES), BF16)
                        k2 = jnp.concatenate([k2, pad], axis=0)
                        v2 = jnp.concatenate([v2, pad], axis=0)
                        bias = jnp.concatenate([bias, jnp.full((Q_BLOCK, KEY_WINDOW - W), MASK_VALUE, F32)],
                                               axis=1)
                    pv, m, l = block_stats(cls_ref[bb, Q_SLOT, rl, a:a + Q_BLOCK, :], k2, v2, bias)
                    merge(bb, pl.ds(a * d + c * cpc + rl, Q_BLOCK, stride=d), pv, m, l, mode)

    def contiguous_phase():
        Sc = S // C
        nblk = Sc // Q_BLOCK

        def window(bb, slot, a):
            lo, hi = a - RADIUS, a + Q_BLOCK + RADIUS
            parts = []
            if lo < 0:
                parts.append(prev_ref[bb, slot])
            parts.append(nat_ref[bb, slot, max(lo, 0):min(hi, Sc), :])
            if hi > Sc:
                parts.append(next_ref[bb, slot])
            return parts[0] if len(parts) == 1 else jnp.concatenate(parts, axis=0)

        for bb in range(NB):
            for i in range(nblk):
                a = i * Q_BLOCK
                typ = 0
                if i == 0:
                    typ = jnp.where(c == 0, 1, typ)
                if i == nblk - 1:
                    typ = jnp.where(c == C - 1, 2, typ)
                pv, m, l = block_stats(nat_ref[bb, Q_SLOT, a:a + Q_BLOCK, :], window(bb, K_SLOT, a),
                                       window(bb, V_SLOT, a), b1_ref[typ])
                merge(bb, pl.ds(pl.multiple_of(c * Sc + a, Q_BLOCK), Q_BLOCK), pv, m, l, "last")

    @pl.when(phase == 0)
    def _():
        strided_phase(16, c16_ref, b16_ref, "first")

    @pl.when(phase == 1)
    def _():
        strided_phase(4, c4_ref, b4_ref, "mid")

    @pl.when(phase == 2)
    def _():
        contiguous_phase()


def _attention(nat_kvq, c4_kvq, c16_kvq, B, S, C, NB):
    Sc = S // C
    assert Sc % Q_BLOCK == 0 and 4 % C == 0 and B % NB == 0
    halo_blocks = S // RADIUS

    def chunk_of(own_phase, p, c):
        return jnp.where(p < own_phase, 0, jnp.where(p > own_phase, C - 1, c))

    def cls_spec(d, own_phase):
        L = S // d
        assert L % Q_BLOCK == 0 and (L == Q_BLOCK or L >= KEY_WINDOW)
        return pl.BlockSpec((NB, None, 3, d // C, L, LANES),
                            lambda b, h, p, c: (b, h, 0, chunk_of(own_phase, p, c), 0, 0))

    nat = pl.BlockSpec((NB, None, 3, Sc, LANES), lambda b, h, p, c: (b, h, 0, chunk_of(2, p, c), 0))
    prev = pl.BlockSpec((NB, None, 2, RADIUS, LANES),
                        lambda b, h, p, c: (b, h, 0, jnp.maximum(chunk_of(2, p, c) * (Sc // RADIUS) - 1, 0), 0))
    nxt = pl.BlockSpec((NB, None, 2, RADIUS, LANES),
                       lambda b, h, p, c: (b, h, 0, jnp.minimum((chunk_of(2, p, c) + 1) * (Sc // RADIUS),
                                                                 halo_blocks - 1), 0))
    c4, c16 = cls_spec(4, 1), cls_spec(16, 0)
    W4, W16 = min(KEY_WINDOW, S // 4), min(KEY_WINDOW, S // 16)
    clamped = (0, -RADIUS, -2 * RADIUS)
    b1 = _band_bias((-RADIUS,) * 3, KEY_WINDOW,
                    extra=(lambda kj: kj >= 0, lambda kj: kj >= RADIUS, lambda kj: kj < KEY_WINDOW - RADIUS))
    b4, b16 = _band_bias(clamped, W4), _band_bias(clamped, W16)
    return pl.pallas_call(
        functools.partial(_attn_kernel, S=S, C=C, NB=NB),
        grid=(B // NB, N_HEAD_PAIRS, 3, C),
        in_specs=[nat, prev, nxt, c4, c16, _resident(b1.shape), _resident(b4.shape), _resident(b16.shape)],
        out_specs=pl.BlockSpec((NB, None, S, LANES), lambda b, h, p, c: (b, h, 0, 0)),
        out_shape=jax.ShapeDtypeStruct((B, N_HEAD_PAIRS, S, LANES), F32),
        scratch_shapes=[pltpu.VMEM((NB, S, LANES), F32), pltpu.VMEM((NB, S, LANES), F32)],
        compiler_params=_params(4),
        name="attn",
    )(nat_kvq, nat_kvq, nat_kvq, c4_kvq, c16_kvq, b1, b4, b16)


def _dft_kernel(bc_ref, nbs_ref, r1_ref, r2_ref, a_ref, o_ref):
    t = pl.program_id(2)

    @pl.when(t == 0)
    def _():
        o_ref[...] = jnp.zeros_like(o_ref)

    for par in range(2):
        bc, nbs = bc_ref[par], nbs_ref[par]
        r1, r2 = r1_ref[par], r2_ref[par]
        t_cos = (bc * r1 + nbs * r2).astype(BF16)
        t_nsin = (nbs * r1 - bc * r2).astype(BF16)
        o_ref[par] += _dot(t_cos, a_ref[par, 0]) + _dot(t_nsin, a_ref[par, 1])

    @pl.when(t == pl.num_programs(2) - 1)
    def _():
        f0, f1 = o_ref[0], o_ref[1]
        o_ref[0] = f0 + f1
        o_ref[1] = f0 - f1


def _seq_dft(a, S, ts, tk, tn):
    ncols = a.shape[-1]
    H = S // 2
    t = 2 * jnp.arange(H, dtype=jnp.int32)[None, None, :] + jnp.arange(2, dtype=jnp.int32)[:, None, None]
    w = 2.0 * math.pi / S
    ang_b = ((jnp.arange(ts, dtype=jnp.int32)[None, :, None] * t) % S).astype(F32) * w
    ang_r = ((jnp.arange(0, H, ts, dtype=jnp.int32)[None, :, None] * t) % S).astype(F32) * w
    bc, nbs = jnp.cos(ang_b), -jnp.sin(ang_b)
    scale = S ** -0.5
    r1 = (jnp.cos(ang_r) * scale).reshape(2, H // ts, 1, H)
    r2 = (jnp.sin(ang_r) * scale).reshape(2, H // ts, 1, H)
    btab = pl.BlockSpec((2, ts, tk), lambda s, n, t: (0, 0, t))
    rtab = pl.BlockSpec((2, None, 1, tk), lambda s, n, t: (0, s, 0, t))
    out = pl.pallas_call(
        _dft_kernel,
        grid=(H // ts, ncols // tn, H // tk),
        in_specs=[btab, btab, rtab, rtab, pl.BlockSpec((2, 2, tk, tn), lambda s, n, t: (0, 0, t, n))],
        out_specs=pl.BlockSpec((2, ts, tn), lambda s, n, t: (0, s, n)),
        out_shape=jax.ShapeDtypeStruct((2, H, ncols), F32),
        compiler_params=_params(3),
        name="seq_dft",
    )(bc, nbs, r1, r2, a)
    return out.reshape(S, ncols)


def _out_ffn_kernel(x1_ref, of_ref, oa_ref, gf_ref, ga_ref, wout_ref, g2_ref, wg_ref, wu_ref, wd_ref,
                    gfin_ref, y_ref):
    oa = jnp.concatenate([oa_ref[hp] for hp in range(N_HEAD_PAIRS)], axis=1)
    na = _rms(oa, ga_ref[...]).astype(BF16)
    nf = _rms(of_ref[...], gf_ref[...]).astype(BF16)
    x2 = x1_ref[...] + _dot(nf, wout_ref[:D_FOURIER, :]) + _dot(na, wout_ref[D_FOURIER:, :])
    h = _rms(x2, g2_ref[...]).astype(BF16)
    x3 = x2 + 0.5 * _swiglu(h, wg_ref, wu_ref, wd_ref)
    y_ref[...] = _rms(x3, gfin_ref[...])


def _out_ffn(x1, of, oa, B, S, gf, ga, wout, g2, wg, wu, wd, gfin, tm):
    N = B * S
    nst = S // tm
    tok = pl.BlockSpec((tm, D_MODEL), lambda i: (i, 0))
    att = pl.BlockSpec((None, N_HEAD_PAIRS, tm, LANES), lambda i: (i // nst, 0, i % nst, 0))
    return pl.pallas_call(
        _out_ffn_kernel,
        grid=(N // tm,),
        in_specs=[tok, pl.BlockSpec((tm, D_FOURIER), lambda i: (i % nst, i // nst)), att,
                  _resident((1, D_FOURIER)), _resident((1, D_ATTN)), _resident((D_MODEL, D_MODEL)),
                  _resident((1, D_MODEL)), _resident((D_MODEL, D_FF)), _resident((D_MODEL, D_FF)),
                  _resident((D_FF, D_MODEL)), _resident((1, D_MODEL))],
        out_specs=tok,
        out_shape=jax.ShapeDtypeStruct((N, D_MODEL), F32),
        compiler_params=_params(1),
        name="out_ffn",
    )(x1, of, oa, gf, ga, wout, g2, wg, wu, wd, gfin)


def _row(g):
    return g.reshape(1, -1).astype(F32)


def _trunk(x, p, win):
    B, S, _ = x.shape
    x1 = _ffn1(x.reshape(B * S, D_MODEL), _row(p["ffn1_norm"]), p["ffn1_w_gate"], p["ffn1_w_up"],
               p["ffn1_w_down"], tm=512)
    a, nat_kvq, c4_kvq, c16_kvq = _proj(x1, B, S, _row(p["mix_norm"]), win, tm=512)
    blocks_per_step = 32
    C = max(1, S // (blocks_per_step * Q_BLOCK))
    NB = max(1, blocks_per_step * Q_BLOCK // S)
    oa = _attention(nat_kvq, c4_kvq, c16_kvq, B, S, C, NB)
    of = _seq_dft(a, S, ts=512, tk=512, tn=min(2048, a.shape[-1]))
    y = _out_ffn(x1, of, oa, B, S, _row(p["fourier_out_norm"]), _row(p["attn_out_norm"]), p["w_out"],
                 _row(p["ffn2_norm"]), p["ffn2_w_gate"], p["ffn2_w_up"], p["ffn2_w_down"],
                 _row(p["final_norm"]), tm=512)
    return y.reshape(B, S, D_MODEL)


def kernel(x_prompt, x_sample, ffn1_norm, ffn1_w_gate, ffn1_w_up, ffn1_w_down, mix_norm, w_in, fourier_w,
           fourier_out_norm, attn_out_norm, w_out, ffn2_norm, ffn2_w_gate, ffn2_w_up, ffn2_w_down, final_norm):
    assert ffn1_w_gate.shape[0] == 1, "single-layer trunk"
    folded = _fold_fourier_weights(w_in[0, :, :D_FOURIER], fourier_w[0])
    win = jnp.concatenate([folded, w_in[0, :, D_FOURIER:].astype(BF16)], axis=1)
    p = dict(
        ffn1_norm=ffn1_norm[0], ffn1_w_gate=ffn1_w_gate[0].astype(BF16), ffn1_w_up=ffn1_w_up[0].astype(BF16),
        ffn1_w_down=ffn1_w_down[0].astype(BF16), mix_norm=mix_norm[0],
        fourier_out_norm=fourier_out_norm[0], attn_out_norm=attn_out_norm[0], w_out=w_out[0].astype(BF16),
        ffn2_norm=ffn2_norm[0], ffn2_w_gate=ffn2_w_gate[0].astype(BF16), ffn2_w_up=ffn2_w_up[0].astype(BF16),
        ffn2_w_down=ffn2_w_down[0].astype(BF16), final_norm=final_norm)
    return (_trunk(x_prompt, p, win), _trunk(x_sample, p, win))
```

```python
import functools
import math

import jax
import jax.numpy as jnp
import numpy as np
from jax import lax
from jax.experimental import pallas as pl
from jax.experimental.pallas import tpu as pltpu

F32 = jnp.float32
BF16 = jnp.bfloat16

D_MODEL = 1024
D_FF = 2816
D_FOURIER = 256
N_FOURIER_GROUPS = 4
FOURIER_GROUP_DIM = 64
D_ATTN = 768
HEAD_DIM = 64
LANES = 128
N_HEAD_PAIRS = D_ATTN // LANES
D_PROJ = 2 * D_FOURIER + 3 * D_ATTN
RADIUS = 64
ROPE_THETA = 10000.0
RMS_EPS = 1e-6
MASK_VALUE = -1e30

FF_CHUNK = 256
FFN_TOKENS = 1024
PROJ_TOKENS = 512
Q_BLOCK = 128
KEY_WINDOW = Q_BLOCK + 2 * RADIUS
LOG2_E = math.log2(math.e)
VMEM_LIMIT = 56 * 1024 * 1024


def _params(n_grid_dims):
    return pltpu.CompilerParams(dimension_semantics=("arbitrary",) * n_grid_dims,
                                vmem_limit_bytes=VMEM_LIMIT)


def _resident(shape):
    nd = len(shape)
    return pl.BlockSpec(shape, lambda *_: (0,) * nd, pipeline_mode=pl.Buffered(1))


def _rms(x, g):
    inv = lax.rsqrt(jnp.mean(x * x, axis=-1, keepdims=True) + RMS_EPS)
    return x * inv * g


def _dot(a, b):
    return jnp.dot(a, b, preferred_element_type=F32)


def _swiglu(h, wg_ref, wu_ref, wd_ref):
    acc = None
    for c in range(D_FF // FF_CHUNK):
        sl = slice(c * FF_CHUNK, (c + 1) * FF_CHUNK)
        g = _dot(h, wg_ref[:, sl])
        u = _dot(h, wu_ref[:, sl])
        a = (g * jax.nn.sigmoid(g) * u).astype(BF16)
        part = _dot(a, wd_ref[sl, :])
        acc = part if acc is None else acc + part
    return acc


def _fold_kernel(winf_ref, cbd_ref, sbd_ref, wbd_ref, o_ref):
    hi = lax.Precision.HIGHEST
    scale = FOURIER_GROUP_DIM ** -0.5
    gc = jnp.dot(cbd_ref[...], wbd_ref[...], precision=hi, preferred_element_type=F32) * scale
    gs = jnp.dot(sbd_ref[...], wbd_ref[...], precision=hi, preferred_element_type=F32) * scale
    winf = winf_ref[...]
    o_ref[:, :D_FOURIER] = jnp.dot(winf, gc, precision=hi, preferred_element_type=F32).astype(BF16)
    o_ref[:, D_FOURIER:] = jnp.dot(winf, gs, precision=hi, preferred_element_type=F32).astype(BF16)


def _fold_fourier_weights(w_in_f, fourier_w):
    c = np.arange(FOURIER_GROUP_DIM)
    ang = 2.0 * np.pi * ((c[:, None] * c[None, :]) % FOURIER_GROUP_DIM) / FOURIER_GROUP_DIM
    eye = np.eye(N_FOURIER_GROUPS)
    cbd = jnp.asarray(np.kron(eye, np.cos(ang)), F32)
    sbd = jnp.asarray(np.kron(eye, np.sin(ang)), F32)
    wbd = jnp.zeros((D_FOURIER, D_FOURIER), F32)
    for g in range(N_FOURIER_GROUPS):
        s = g * FOURIER_GROUP_DIM
        wbd = lax.dynamic_update_slice(wbd, fourier_w[g].astype(F32), (s, s))
    return pl.pallas_call(
        _fold_kernel,
        out_shape=jax.ShapeDtypeStruct((D_MODEL, 2 * D_FOURIER), BF16),
        name="fold_fourier",
    )(w_in_f, cbd, sbd, wbd)


def _ffn1_kernel(x_ref, g1_ref, wg_ref, wu_ref, wd_ref, x1_ref):
    x = x_ref[...]
    h = _rms(x, g1_ref[...]).astype(BF16)
    x1_ref[...] = x + 0.5 * _swiglu(h, wg_ref, wu_ref, wd_ref)


def _ffn1(xf, g1, wg, wu, wd, tm):
    N = xf.shape[0]
    tok = pl.BlockSpec((tm, D_MODEL), lambda i: (i, 0))
    return pl.pallas_call(
        _ffn1_kernel,
        grid=(N // tm,),
        in_specs=[tok, _resident((1, D_MODEL)), _resident((D_MODEL, D_FF)), _resident((D_MODEL, D_FF)),
                  _resident((D_FF, D_MODEL))],
        out_specs=tok,
        out_shape=jax.ShapeDtypeStruct((N, D_MODEL), F32),
        compiler_params=_params(1),
        name="ffn1",
    )(xf, g1, wg, wu, wd)


def _rope(t, cos, sin_lo, sin_hi):
    return (t * cos + pltpu.roll(t, LANES - HEAD_DIM // 2, axis=1) * sin_lo
            + pltpu.roll(t, HEAD_DIM // 2, axis=1) * sin_hi)


def _proj_kernel(x1_ref, gm_ref, win_ref, cos_ref, slo_ref, shi_ref,
                 a_ref, o1_ref, o4_ref, o16_ref,
                 nat_ref, cls4_ref, pa_ref):
    tm = x1_ref.shape[0]
    h2 = _rms(x1_ref[...], gm_ref[...]).astype(BF16)

    def emit(t, hp, slot):
        o1_ref[hp, slot] = t.astype(BF16)
        e = hp * 3 + slot
        nat_ref[e] = t
        for r4 in range(4):
            c4 = nat_ref[e, pl.ds(r4, tm // 4, stride=4), :]
            o4_ref[hp, slot, r4] = c4.astype(BF16)
            cls4_ref[e, r4] = c4
            for j in range(4):
                o16_ref[hp, slot, r4 + 4 * j] = cls4_ref[e, r4, pl.ds(j, tm // 16, stride=4), :].astype(BF16)

    cos = cos_ref[...]
    slo = slo_ref[...]
    shi = shi_ref[...]
    base = 2 * D_FOURIER
    for j in range(D_ATTN // 256):
        c0 = j * 256
        qq = _dot(h2, win_ref[:, base + c0: base + c0 + 256])
        kk = _dot(h2, win_ref[:, base + D_ATTN + c0: base + D_ATTN + c0 + 256])
        vv = _dot(h2, win_ref[:, base + 2 * D_ATTN + c0: base + 2 * D_ATTN + c0 + 256])
        for i in range(2):
            sl = slice(i * LANES, (i + 1) * LANES)
            hp = 2 * j + i
            emit(_rope(qq[:, sl], cos, slo, shi) * (HEAD_DIM ** -0.5 * LOG2_E), hp, Q_SLOT)
            emit(_rope(kk[:, sl], cos, slo, shi), hp, K_SLOT)
            emit(vv[:, sl], hp, V_SLOT)

    pa = _dot(h2, win_ref[:, :2 * D_FOURIER])
    tiles_per_part = D_FOURIER // LANES
    for j in range(2 * tiles_per_part):
        pa_ref[j] = pa[:, j * LANES:(j + 1) * LANES]
    for par in range(2):
        for j in range(2 * tiles_per_part):
            lanes = slice((j % tiles_per_part) * LANES, (j % tiles_per_part + 1) * LANES)
            a_ref[par, j // tiles_per_part, :, lanes] = pa_ref[j, pl.ds(par, tm // 2, stride=2), :].astype(BF16)


def _rope_tables(S):
    inv_freq = ROPE_THETA ** (-jnp.arange(0, HEAD_DIM, 2, dtype=F32) / HEAD_DIM)
    freqs = jnp.arange(S, dtype=F32)[:, None] * inv_freq[None, :]
    cos, sin = jnp.cos(freqs), jnp.sin(freqs)
    zero = jnp.zeros_like(sin)
    cos_t = jnp.tile(cos, (1, LANES // (HEAD_DIM // 2)))
    sin_lo = jnp.tile(jnp.concatenate([-sin, zero], axis=1), (1, LANES // HEAD_DIM))
    sin_hi = jnp.tile(jnp.concatenate([zero, sin], axis=1), (1, LANES // HEAD_DIM))
    return cos_t, sin_lo, sin_hi


def _proj(x1, B, S, gm, win, tm):
    N = B * S
    nst = S // tm
    cos_t, sin_lo, sin_hi = _rope_tables(S)
    tok = pl.BlockSpec((tm, D_MODEL), lambda i: (i, 0))
    tab = pl.BlockSpec((tm, LANES), lambda i: (i % nst, 0))
    nat = pl.BlockSpec((None, N_HEAD_PAIRS, 3, tm, LANES), lambda i: (i // nst, 0, 0, i % nst, 0))
    nat_shape = jax.ShapeDtypeStruct((B, N_HEAD_PAIRS, 3, S, LANES), BF16)

    def cls(d):
        spec = pl.BlockSpec((None, N_HEAD_PAIRS, 3, d, tm // d, LANES),
                            lambda i: (i // nst, 0, 0, 0, i % nst, 0))
        return spec, jax.ShapeDtypeStruct((B, N_HEAD_PAIRS, 3, d, S // d, LANES), BF16)

    c4, c4_shape = cls(4)
    c16, c16_shape = cls(16)
    return pl.pallas_call(
        _proj_kernel,
        grid=(N // tm,),
        in_specs=[tok, _resident((1, D_MODEL)), _resident((D_MODEL, D_PROJ)), tab, tab, tab],
        out_specs=[pl.BlockSpec((2, 2, tm // 2, D_FOURIER), lambda i: (0, 0, i % nst, i // nst)),
                   nat, c4, c16],
        out_shape=[jax.ShapeDtypeStruct((2, 2, S // 2, B * D_FOURIER), BF16), nat_shape, c4_shape, c16_shape],
        scratch_shapes=[pltpu.VMEM((3 * N_HEAD_PAIRS, tm, LANES), F32),
                        pltpu.VMEM((3 * N_HEAD_PAIRS, 4, tm // 4, LANES), F32),
                        pltpu.VMEM((2 * D_FOURIER // LANES, tm, LANES), F32)],
        compiler_params=_params(1),
        name="proj",
    )(x1, gm, win, cos_t, sin_lo, sin_hi)


def _band_bias(offsets, W, extra=None):
    qi = np.arange(Q_BLOCK)[:, None]
    kj = np.arange(W)[None, :]
    tiles = []
    for t, off in enumerate(offsets):
        ok = np.abs(kj + off - qi) <= RADIUS
        if extra is not None:
            ok &= extra[t](kj)
        tiles.append(np.where(ok, 0.0, MASK_VALUE))
    return jnp.asarray(np.stack(tiles), F32)


K_SLOT, V_SLOT, Q_SLOT = 0, 1, 2


def _attn_kernel(nat_ref, prev_ref, next_ref, c4_ref, c16_ref, b1_ref, b4_ref, b16_ref,
                 o_ref, m_ref, l_ref, *, S, C, NB):
    phase = pl.program_id(2)
    c = pl.program_id(3)
    lane = lax.broadcasted_iota(jnp.int32, (1, LANES), 1)
    first_head = lane < HEAD_DIM
    contract_last = (((1,), (1,)), ((), ()))

    def block_stats(q2, k2, v2, bias):
        v_ones = jnp.concatenate([v2, jnp.ones_like(v2)], axis=1)
        per_head = []
        for head_mask in (first_head, jnp.logical_not(first_head)):
            qh = jnp.where(head_mask, q2, jnp.zeros_like(q2))
            s = lax.dot_general(qh, k2, contract_last, preferred_element_type=F32) + bias
            m = jnp.max(s, axis=-1, keepdims=True)
            p = jnp.exp2(s - m).astype(BF16)
            pvl = _dot(p, v_ones)
            per_head.append((pvl[:, :LANES], jnp.broadcast_to(m, (Q_BLOCK, LANES)), pvl[:, LANES:]))
        return tuple(jnp.where(first_head, a, b) for a, b in zip(*per_head))

    def merge(bb, rows, pv, m, l, mode):
        if mode == "first":
            o_ref[bb, rows, :] = pv
            m_ref[bb, rows, :] = m
            l_ref[bb, rows, :] = l
            return
        m_old = m_ref[bb, rows, :]
        m_new = jnp.maximum(m_old, m)
        a_old = jnp.exp2(m_old - m_new)
        a_blk = jnp.exp2(m - m_new)
        acc = a_old * o_ref[bb, rows, :] + a_blk * pv
        den = a_old * l_ref[bb, rows, :] + a_blk * l
        if mode == "last":
            o_ref[bb, rows, :] = acc / den
        else:
            o_ref[bb, rows, :] = acc
            m_ref[bb, rows, :] = m_new
            l_ref[bb, rows, :] = den

    def strided_phase(d, cls_ref, bias_ref, mode):
        L = S // d
        nblk = L // Q_BLOCK
        W = min(KEY_WINDOW, L)
        cpc = d // C
        for bb in range(NB):
            for rl in range(cpc):
                for i in range(nblk):
                    a = i * Q_BLOCK
                    ws = min(max(a - RADIUS, 0), L - W)
                    typ = 0 if i == 0 else (2 if i == nblk - 1 else 1)
                    k2 = cls_ref[bb, K_SLOT, rl, ws:ws + W, :]
                    v2 = cls_ref[bb, V_SLOT, rl, ws:ws + W, :]
                    bias = bias_ref[typ]
                    if W < KEY_WINDOW:
                        pad = jnp.zeros((KEY_WINDOW - W, LANES), BF16)
                        k2 = jnp.concatenate([k2, pad], axis=0)
                        v2 = jnp.concatenate([v2, pad], axis=0)
                        bias = jnp.concatenate([bias, jnp.full((Q_BLOCK, KEY_WINDOW - W), MASK_VALUE, F32)],
                                               axis=1)
                    pv, m, l = block_stats(cls_ref[bb, Q_SLOT, rl, a:a + Q_BLOCK, :], k2, v2, bias)
                    merge(bb, pl.ds(a * d + c * cpc + rl, Q_BLOCK, stride=d), pv, m, l, mode)

    def contiguous_phase():
        Sc = S // C
        nblk = Sc // Q_BLOCK

        def window(bb, slot, a):
            lo, hi = a - RADIUS, a + Q_BLOCK + RADIUS
            parts = []
            if lo < 0:
                parts.append(prev_ref[bb, slot])
            parts.append(nat_ref[bb, slot, max(lo, 0):min(hi, Sc), :])
            if hi > Sc:
                parts.append(next_ref[bb, slot])
            return parts[0] if len(parts) == 1 else jnp.concatenate(parts, axis=0)

        for bb in range(NB):
            for i in range(nblk):
                a = i * Q_BLOCK
                typ = 0
                if i == 0:
                    typ = jnp.where(c == 0, 1, typ)
                if i == nblk - 1:
                    typ = jnp.where(c == C - 1, 2, typ)
                pv, m, l = block_stats(nat_ref[bb, Q_SLOT, a:a + Q_BLOCK, :], window(bb, K_SLOT, a),
                                       window(bb, V_SLOT, a), b1_ref[typ])
                merge(bb, pl.ds(pl.multiple_of(c * Sc + a, Q_BLOCK), Q_BLOCK), pv, m, l, "last")

    @pl.when(phase == 0)
    def _():
        strided_phase(16, c16_ref, b16_ref, "first")

    @pl.when(phase == 1)
    def _():
        strided_phase(4, c4_ref, b4_ref, "mid")

    @pl.when(phase == 2)
    def _():
        contiguous_phase()


def _attention(nat_kvq, c4_kvq, c16_kvq, B, S, C, NB):
    Sc = S // C
    assert Sc % Q_BLOCK == 0 and 4 % C == 0 and B % NB == 0
    halo_blocks = S // RADIUS

    def chunk_of(own_phase, p, c):
        return jnp.where(p < own_phase, 0, jnp.where(p > own_phase, C - 1, c))

    def cls_spec(d, own_phase):
        L = S // d
        assert L % Q_BLOCK == 0 and (L == Q_BLOCK or L >= KEY_WINDOW)
        return pl.BlockSpec((NB, None, 3, d // C, L, LANES),
                            lambda b, h, p, c: (b, h, 0, chunk_of(own_phase, p, c), 0, 0))

    nat = pl.BlockSpec((NB, None, 3, Sc, LANES), lambda b, h, p, c: (b, h, 0, chunk_of(2, p, c), 0))
    prev = pl.BlockSpec((NB, None, 2, RADIUS, LANES),
                        lambda b, h, p, c: (b, h, 0, jnp.maximum(chunk_of(2, p, c) * (Sc // RADIUS) - 1, 0), 0))
    nxt = pl.BlockSpec((NB, None, 2, RADIUS, LANES),
                       lambda b, h, p, c: (b, h, 0, jnp.minimum((chunk_of(2, p, c) + 1) * (Sc // RADIUS),
                                                                 halo_blocks - 1), 0))
    c4, c16 = cls_spec(4, 1), cls_spec(16, 0)
    W4, W16 = min(KEY_WINDOW, S // 4), min(KEY_WINDOW, S // 16)
    clamped = (0, -RADIUS, -2 * RADIUS)
    b1 = _band_bias((-RADIUS,) * 3, KEY_WINDOW,
                    extra=(lambda kj: kj >= 0, lambda kj: kj >= RADIUS, lambda kj: kj < KEY_WINDOW - RADIUS))
    b4, b16 = _band_bias(clamped, W4), _band_bias(clamped, W16)
    return pl.pallas_call(
        functools.partial(_attn_kernel, S=S, C=C, NB=NB),
        grid=(B // NB, N_HEAD_PAIRS, 3, C),
        in_specs=[nat, prev, nxt, c4, c16, _resident(b1.shape), _resident(b4.shape), _resident(b16.shape)],
        out_specs=pl.BlockSpec((NB, None, S, LANES), lambda b, h, p, c: (b, h, 0, 0)),
        out_shape=jax.ShapeDtypeStruct((B, N_HEAD_PAIRS, S, LANES), F32),
        scratch_shapes=[pltpu.VMEM((NB, S, LANES), F32), pltpu.VMEM((NB, S, LANES), F32)],
        compiler_params=_params(4),
        name="attn",
    )(nat_kvq, nat_kvq, nat_kvq, c4_kvq, c16_kvq, b1, b4, b16)


def _dft_kernel(bc_ref, nbs_ref, r1_ref, r2_ref, a_ref, o_ref):
    t = pl.program_id(2)

    @pl.when(t == 0)
    def _():
        o_ref[...] = jnp.zeros_like(o_ref)

    for par in range(2):
        bc, nbs = bc_ref[par], nbs_ref[par]
        r1, r2 = r1_ref[par], r2_ref[par]
        t_cos = (bc * r1 + nbs * r2).astype(BF16)
        t_nsin = (nbs * r1 - bc * r2).astype(BF16)
        o_ref[par] += _dot(t_cos, a_ref[par, 0]) + _dot(t_nsin, a_ref[par, 1])

    @pl.when(t == pl.num_programs(2) - 1)
    def _():
        f0, f1 = o_ref[0], o_ref[1]
        o_ref[0] = f0 + f1
        o_ref[1] = f0 - f1


def _seq_dft(a, S, ts, tk, tn):
    ncols = a.shape[-1]
    H = S // 2
    t = 2 * jnp.arange(H, dtype=jnp.int32)[None, None, :] + jnp.arange(2, dtype=jnp.int32)[:, None, None]
    w = 2.0 * math.pi / S
    ang_r = ((jnp.arange(0, H, ts, dtype=jnp.int32)[None, :, None] * t) % S).astype(F32) * w
    lo = 32
    ang_hi = ((jnp.arange(0, ts, lo, dtype=jnp.int32)[None, :, None] * t) % S).astype(F32) * w
    ang_lo = ((jnp.arange(lo, dtype=jnp.int32)[None, :, None] * t) % S).astype(F32) * w
    c_hi, s_hi = jnp.cos(ang_hi)[:, :, None, :], jnp.sin(ang_hi)[:, :, None, :]
    c_lo, s_lo = jnp.cos(ang_lo)[:, None, :, :], jnp.sin(ang_lo)[:, None, :, :]
    bc = (c_hi * c_lo - s_hi * s_lo).reshape(2, ts, H)
    nbs = -(s_hi * c_lo + c_hi * s_lo).reshape(2, ts, H)
    scale = S ** -0.5
    r1 = (jnp.cos(ang_r) * scale).reshape(2, H // ts, 1, H)
    r2 = (jnp.sin(ang_r) * scale).reshape(2, H // ts, 1, H)
    btab = pl.BlockSpec((2, ts, tk), lambda s, n, t: (0, 0, t))
    rtab = pl.BlockSpec((2, None, 1, tk), lambda s, n, t: (0, s, 0, t))
    out = pl.pallas_call(
        _dft_kernel,
        grid=(H // ts, ncols // tn, H // tk),
        in_specs=[btab, btab, rtab, rtab, pl.BlockSpec((2, 2, tk, tn), lambda s, n, t: (0, 0, t, n))],
        out_specs=pl.BlockSpec((2, ts, tn), lambda s, n, t: (0, s, n)),
        out_shape=jax.ShapeDtypeStruct((2, H, ncols), F32),
        compiler_params=_params(3),
        name="seq_dft",
    )(bc, nbs, r1, r2, a)
    return out.reshape(S, ncols)


def _out_ffn_kernel(x1_ref, of_ref, oa_ref, gf_ref, ga_ref, wout_ref, g2_ref, wg_ref, wu_ref, wd_ref,
                    gfin_ref, y_ref):
    oa = jnp.concatenate([oa_ref[hp] for hp in range(N_HEAD_PAIRS)], axis=1)
    na = _rms(oa, ga_ref[...]).astype(BF16)
    nf = _rms(of_ref[...], gf_ref[...]).astype(BF16)
    x2 = x1_ref[...] + _dot(nf, wout_ref[:D_FOURIER, :]) + _dot(na, wout_ref[D_FOURIER:, :])
    h = _rms(x2, g2_ref[...]).astype(BF16)
    x3 = x2 + 0.5 * _swiglu(h, wg_ref, wu_ref, wd_ref)
    y_ref[...] = _rms(x3, gfin_ref[...])


def _out_ffn(x1, of, oa, B, S, gf, ga, wout, g2, wg, wu, wd, gfin, tm):
    N = B * S
    nst = S // tm
    tok = pl.BlockSpec((tm, D_MODEL), lambda i: (i, 0))
    att = pl.BlockSpec((None, N_HEAD_PAIRS, tm, LANES), lambda i: (i // nst, 0, i % nst, 0))
    return pl.pallas_call(
        _out_ffn_kernel,
        grid=(N // tm,),
        in_specs=[tok, pl.BlockSpec((tm, D_FOURIER), lambda i: (i % nst, i // nst)), att,
                  _resident((1, D_FOURIER)), _resident((1, D_ATTN)), _resident((D_MODEL, D_MODEL)),
                  _resident((1, D_MODEL)), _resident((D_MODEL, D_FF)), _resident((D_MODEL, D_FF)),
                  _resident((D_FF, D_MODEL)), _resident((1, D_MODEL))],
        out_specs=tok,
        out_shape=jax.ShapeDtypeStruct((N, D_MODEL), F32),
        compiler_params=_params(1),
        name="out_ffn",
    )(x1, of, oa, gf, ga, wout, g2, wg, wu, wd, gfin)


def _row(g):
    return g.reshape(1, -1).astype(F32)


def _trunk(x, p, win):
    B, S, _ = x.shape
    x1 = _ffn1(x.reshape(B * S, D_MODEL), _row(p["ffn1_norm"]), p["ffn1_w_gate"], p["ffn1_w_up"],
               p["ffn1_w_down"], tm=FFN_TOKENS)
    a, nat_kvq, c4_kvq, c16_kvq = _proj(x1, B, S, _row(p["mix_norm"]), win, tm=PROJ_TOKENS)
    blocks_per_step = 32
    C = max(1, S // (blocks_per_step * Q_BLOCK))
    NB = max(1, blocks_per_step * Q_BLOCK // S)
    oa = _attention(nat_kvq, c4_kvq, c16_kvq, B, S, C, NB)
    of = _seq_dft(a, S, ts=512, tk=512, tn=min(2048, a.shape[-1]))
    y = _out_ffn(x1, of, oa, B, S, _row(p["fourier_out_norm"]), _row(p["attn_out_norm"]), p["w_out"],
                 _row(p["ffn2_norm"]), p["ffn2_w_gate"], p["ffn2_w_up"], p["ffn2_w_down"],
                 _row(p["final_norm"]), tm=FFN_TOKENS)
    return y.reshape(B, S, D_MODEL)


def kernel(x_prompt, x_sample, ffn1_norm, ffn1_w_gate, ffn1_w_up, ffn1_w_down, mix_norm, w_in, fourier_w,
           fourier_out_norm, attn_out_norm, w_out, ffn2_norm, ffn2_w_gate, ffn2_w_up, ffn2_w_down, final_norm):
    assert ffn1_w_gate.shape[0] == 1, "single-layer trunk"
    folded = _fold_fourier_weights(w_in[0, :, :D_FOURIER], fourier_w[0])
    win = jnp.concatenate([folded, w_in[0, :, D_FOURIER:].astype(BF16)], axis=1)
    p = dict(
        ffn1_norm=ffn1_norm[0], ffn1_w_gate=ffn1_w_gate[0].astype(BF16), ffn1_w_up=ffn1_w_up[0].astype(BF16),
        ffn1_w_down=ffn1_w_down[0].astype(BF16), mix_norm=mix_norm[0],
        fourier_out_norm=fourier_out_norm[0], attn_out_norm=attn_out_norm[0], w_out=w_out[0].astype(BF16),
        ffn2_norm=ffn2_norm[0], ffn2_w_gate=ffn2_w_gate[0].astype(BF16), ffn2_w_up=ffn2_w_up[0].astype(BF16),
        ffn2_w_down=ffn2_w_down[0].astype(BF16), final_norm=final_norm)
    return (_trunk(x_prompt, p, win), _trunk(x_sample, p, win))
```

```python
import functools
import math

import jax
import jax.numpy as jnp
import numpy as np
from jax import lax
from jax.experimental import pallas as pl
from jax.experimental.pallas import tpu as pltpu

F32 = jnp.float32
BF16 = jnp.bfloat16

D_MODEL = 1024
D_FF = 2816
D_FOURIER = 256
N_FOURIER_GROUPS = 4
FOURIER_GROUP_DIM = 64
D_ATTN = 768
HEAD_DIM = 64
LANES = 128
N_HEAD_PAIRS = D_ATTN // LANES
D_PROJ = 2 * D_FOURIER + 3 * D_ATTN
RADIUS = 64
ROPE_THETA = 10000.0
RMS_EPS = 1e-6
MASK_VALUE = -1e30

FF_CHUNK = 256
FFN_TOKENS = 1024
PROJ_TOKENS = 512
Q_BLOCK = 128
KEY_WINDOW = Q_BLOCK + 2 * RADIUS
LOG2_E = math.log2(math.e)
VMEM_LIMIT = 56 * 1024 * 1024


def _params(n_grid_dims):
    return pltpu.CompilerParams(dimension_semantics=("arbitrary",) * n_grid_dims,
                                vmem_limit_bytes=VMEM_LIMIT)


def _resident(shape):
    nd = len(shape)
    return pl.BlockSpec(shape, lambda *_: (0,) * nd, pipeline_mode=pl.Buffered(1))


def _rms(x, g):
    inv = lax.rsqrt(jnp.mean(x * x, axis=-1, keepdims=True) + RMS_EPS)
    return x * inv * g


def _dot(a, b):
    return jnp.dot(a, b, preferred_element_type=F32)


def _swiglu(h, wg_ref, wu_ref, wd_ref):
    acc = None
    for c in range(D_FF // FF_CHUNK):
        sl = slice(c * FF_CHUNK, (c + 1) * FF_CHUNK)
        g = _dot(h, wg_ref[:, sl])
        u = _dot(h, wu_ref[:, sl])
        a = (g * jax.nn.sigmoid(g) * u).astype(BF16)
        part = _dot(a, wd_ref[sl, :])
        acc = part if acc is None else acc + part
    return acc


def _fold_kernel(winf_ref, cbd_ref, sbd_ref, wbd_ref, o_ref):
    hi = lax.Precision.HIGHEST
    scale = FOURIER_GROUP_DIM ** -0.5
    gc = jnp.dot(cbd_ref[...], wbd_ref[...], precision=hi, preferred_element_type=F32) * scale
    gs = jnp.dot(sbd_ref[...], wbd_ref[...], precision=hi, preferred_element_type=F32) * scale
    winf = winf_ref[...]
    o_ref[:, :D_FOURIER] = jnp.dot(winf, gc, precision=hi, preferred_element_type=F32).astype(BF16)
    o_ref[:, D_FOURIER:] = jnp.dot(winf, gs, precision=hi, preferred_element_type=F32).astype(BF16)


def _fold_fourier_weights(w_in_f, fourier_w):
    c = np.arange(FOURIER_GROUP_DIM)
    ang = 2.0 * np.pi * ((c[:, None] * c[None, :]) % FOURIER_GROUP_DIM) / FOURIER_GROUP_DIM
    eye = np.eye(N_FOURIER_GROUPS)
    cbd = jnp.asarray(np.kron(eye, np.cos(ang)), F32)
    sbd = jnp.asarray(np.kron(eye, np.sin(ang)), F32)
    wbd = jnp.zeros((D_FOURIER, D_FOURIER), F32)
    for g in range(N_FOURIER_GROUPS):
        s = g * FOURIER_GROUP_DIM
        wbd = lax.dynamic_update_slice(wbd, fourier_w[g].astype(F32), (s, s))
    return pl.pallas_call(
        _fold_kernel,
        out_shape=jax.ShapeDtypeStruct((D_MODEL, 2 * D_FOURIER), BF16),
        name="fold_fourier",
    )(w_in_f, cbd, sbd, wbd)


def _ffn1_kernel(x_ref, g1_ref, wg_ref, wu_ref, wd_ref, x1_ref):
    x = x_ref[...]
    h = _rms(x, g1_ref[...]).astype(BF16)
    x1_ref[...] = x + 0.5 * _swiglu(h, wg_ref, wu_ref, wd_ref)


def _ffn1(xf, g1, wg, wu, wd, tm):
    N = xf.shape[0]
    tok = pl.BlockSpec((tm, D_MODEL), lambda i: (i, 0))
    return pl.pallas_call(
        _ffn1_kernel,
        grid=(N // tm,),
        in_specs=[tok, _resident((1, D_MODEL)), _resident((D_MODEL, D_FF)), _resident((D_MODEL, D_FF)),
                  _resident((D_FF, D_MODEL))],
        out_specs=tok,
        out_shape=jax.ShapeDtypeStruct((N, D_MODEL), F32),
        compiler_params=_params(1),
        name="ffn1",
    )(xf, g1, wg, wu, wd)


def _rope(t, cos, sin_lo, sin_hi):
    return (t * cos + pltpu.roll(t, LANES - HEAD_DIM // 2, axis=1) * sin_lo
            + pltpu.roll(t, HEAD_DIM // 2, axis=1) * sin_hi)


def _proj_kernel(x1_ref, gm_ref, win_ref, cos_ref, slo_ref, shi_ref,
                 a_ref, o1_ref, o4_ref, o16_ref,
                 nat_ref, cls4_ref, pa_ref):
    tm = x1_ref.shape[0]
    h2 = _rms(x1_ref[...], gm_ref[...]).astype(BF16)

    def emit(t, hp, slot):
        e = hp * 3 + slot
        nat_ref[e] = t
        if slot == Q_SLOT:
            part = Q_BLOCK // 4
            for blk in range(tm // Q_BLOCK):
                for r4 in range(4):
                    rows = slice(blk * Q_BLOCK + r4 * part, blk * Q_BLOCK + (r4 + 1) * part)
                    o1_ref[hp, slot, rows, :] = nat_ref[e, pl.ds(blk * Q_BLOCK + r4, part, stride=4), :].astype(BF16)
        else:
            o1_ref[hp, slot] = t.astype(BF16)
        for r4 in range(4):
            c4 = nat_ref[e, pl.ds(r4, tm // 4, stride=4), :]
            o4_ref[hp, slot, r4] = c4.astype(BF16)
            cls4_ref[e, r4] = c4
            for j in range(4):
                o16_ref[hp, slot, r4 + 4 * j] = cls4_ref[e, r4, pl.ds(j, tm // 16, stride=4), :].astype(BF16)

    cos = cos_ref[...]
    slo = slo_ref[...]
    shi = shi_ref[...]
    base = 2 * D_FOURIER
    for j in range(D_ATTN // 256):
        c0 = j * 256
        qq = _dot(h2, win_ref[:, base + c0: base + c0 + 256])
        kk = _dot(h2, win_ref[:, base + D_ATTN + c0: base + D_ATTN + c0 + 256])
        vv = _dot(h2, win_ref[:, base + 2 * D_ATTN + c0: base + 2 * D_ATTN + c0 + 256])
        for i in range(2):
            sl = slice(i * LANES, (i + 1) * LANES)
            hp = 2 * j + i
            emit(_rope(qq[:, sl], cos, slo, shi) * (HEAD_DIM ** -0.5 * LOG2_E), hp, Q_SLOT)
            emit(_rope(kk[:, sl], cos, slo, shi), hp, K_SLOT)
            emit(vv[:, sl], hp, V_SLOT)

    pa = _dot(h2, win_ref[:, :2 * D_FOURIER])
    tiles_per_part = D_FOURIER // LANES
    for j in range(2 * tiles_per_part):
        pa_ref[j] = pa[:, j * LANES:(j + 1) * LANES]
    for par in range(2):
        for j in range(2 * tiles_per_part):
            lanes = slice((j % tiles_per_part) * LANES, (j % tiles_per_part + 1) * LANES)
            a_ref[par, j // tiles_per_part, :, lanes] = pa_ref[j, pl.ds(par, tm // 2, stride=2), :].astype(BF16)


def _rope_tables(S):
    inv_freq = ROPE_THETA ** (-jnp.arange(0, HEAD_DIM, 2, dtype=F32) / HEAD_DIM)
    freqs = jnp.arange(S, dtype=F32)[:, None] * inv_freq[None, :]
    cos, sin = jnp.cos(freqs), jnp.sin(freqs)
    zero = jnp.zeros_like(sin)
    cos_t = jnp.tile(cos, (1, LANES // (HEAD_DIM // 2)))
    sin_lo = jnp.tile(jnp.concatenate([-sin, zero], axis=1), (1, LANES // HEAD_DIM))
    sin_hi = jnp.tile(jnp.concatenate([zero, sin], axis=1), (1, LANES // HEAD_DIM))
    return cos_t, sin_lo, sin_hi


def _proj(x1, B, S, gm, win, tm):
    N = B * S
    nst = S // tm
    cos_t, sin_lo, sin_hi = _rope_tables(S)
    tok = pl.BlockSpec((tm, D_MODEL), lambda i: (i, 0))
    tab = pl.BlockSpec((tm, LANES), lambda i: (i % nst, 0))
    nat = pl.BlockSpec((None, N_HEAD_PAIRS, 3, tm, LANES), lambda i: (i // nst, 0, 0, i % nst, 0))
    nat_shape = jax.ShapeDtypeStruct((B, N_HEAD_PAIRS, 3, S, LANES), BF16)

    def cls(d):
        spec = pl.BlockSpec((None, N_HEAD_PAIRS, 3, d, tm // d, LANES),
                            lambda i: (i // nst, 0, 0, 0, i % nst, 0))
        return spec, jax.ShapeDtypeStruct((B, N_HEAD_PAIRS, 3, d, S // d, LANES), BF16)

    c4, c4_shape = cls(4)
    c16, c16_shape = cls(16)
    return pl.pallas_call(
        _proj_kernel,
        grid=(N // tm,),
        in_specs=[tok, _resident((1, D_MODEL)), _resident((D_MODEL, D_PROJ)), tab, tab, tab],
        out_specs=[pl.BlockSpec((2, 2, tm // 2, D_FOURIER), lambda i: (0, 0, i % nst, i // nst)),
                   nat, c4, c16],
        out_shape=[jax.ShapeDtypeStruct((2, 2, S // 2, B * D_FOURIER), BF16), nat_shape, c4_shape, c16_shape],
        scratch_shapes=[pltpu.VMEM((3 * N_HEAD_PAIRS, tm, LANES), F32),
                        pltpu.VMEM((3 * N_HEAD_PAIRS, 4, tm // 4, LANES), F32),
                        pltpu.VMEM((2 * D_FOURIER // LANES, tm, LANES), F32)],
        compiler_params=_params(1),
        name="proj",
    )(x1, gm, win, cos_t, sin_lo, sin_hi)


def _block_order():
    return np.arange(Q_BLOCK).reshape(Q_BLOCK // 4, 4).T.reshape(-1)


def _band_bias(offsets, W, extra=None):
    qi = np.arange(Q_BLOCK)[:, None]
    kj = np.arange(W)[None, :]
    tiles = []
    for t, off in enumerate(offsets):
        ok = np.abs(kj + off - qi) <= RADIUS
        if extra is not None:
            ok &= extra[t](kj)
        tiles.append(np.where(ok, 0.0, MASK_VALUE))
    return jnp.asarray(np.stack(tiles), F32)


K_SLOT, V_SLOT, Q_SLOT = 0, 1, 2


def _attn_kernel(nat_ref, prev_ref, next_ref, c4_ref, c16_ref, b1_ref, b4_ref, b16_ref,
                 o_ref, acc_ref, m_ref, l_ref, *, S, C, NB):
    phase = pl.program_id(2)
    c = pl.program_id(3)
    lane = lax.broadcasted_iota(jnp.int32, (1, LANES), 1)
    first_head = lane < HEAD_DIM
    contract_last = (((1,), (1,)), ((), ()))

    def block_stats(q2, k2, v2, bias):
        v_ones = jnp.concatenate([v2, jnp.ones_like(v2)], axis=1)
        per_head = []
        for head_mask in (first_head, jnp.logical_not(first_head)):
            qh = jnp.where(head_mask, q2, jnp.zeros_like(q2))
            s = lax.dot_general(qh, k2, contract_last, preferred_element_type=F32) + bias
            m = jnp.max(s, axis=-1, keepdims=True)
            p = jnp.exp2(s - m).astype(BF16)
            pvl = _dot(p, v_ones)
            per_head.append((pvl[:, :LANES], jnp.broadcast_to(m, (Q_BLOCK, LANES)), pvl[:, LANES:]))
        return tuple(jnp.where(first_head, a, b) for a, b in zip(*per_head))

    def state(ref, bb, rows):
        if isinstance(rows, (list, tuple)):
            return jnp.concatenate([ref[bb, r, :] for r in rows], axis=0)
        return ref[bb, rows, :]

    def merge(bb, rows, pv, m, l, mode):
        if mode == "first":
            acc_ref[bb, rows, :] = pv
            m_ref[bb, rows, :] = m
            l_ref[bb, rows, :] = l
            return None
        m_old = state(m_ref, bb, rows)
        m_new = jnp.maximum(m_old, m)
        a_old = jnp.exp2(m_old - m_new)
        a_blk = jnp.exp2(m - m_new)
        acc = a_old * state(acc_ref, bb, rows) + a_blk * pv
        den = a_old * state(l_ref, bb, rows) + a_blk * l
        if mode == "last":
            return acc / den
        acc_ref[bb, rows, :] = acc
        m_ref[bb, rows, :] = m_new
        l_ref[bb, rows, :] = den
        return None

    def strided_phase(d, cls_ref, bias_ref, mode):
        L = S // d
        nblk = L // Q_BLOCK
        W = min(KEY_WINDOW, L)
        cpc = d // C
        for bb in range(NB):
            for rl in range(cpc):
                for i in range(nblk):
                    a = i * Q_BLOCK
                    ws = min(max(a - RADIUS, 0), L - W)
                    typ = 0 if i == 0 else (2 if i == nblk - 1 else 1)
                    k2 = cls_ref[bb, K_SLOT, rl, ws:ws + W, :]
                    v2 = cls_ref[bb, V_SLOT, rl, ws:ws + W, :]
                    bias = bias_ref[typ]
                    if W < KEY_WINDOW:
                        pad = jnp.zeros((KEY_WINDOW - W, LANES), BF16)
                        k2 = jnp.concatenate([k2, pad], axis=0)
                        v2 = jnp.concatenate([v2, pad], axis=0)
                        bias = jnp.concatenate([bias, jnp.full((Q_BLOCK, KEY_WINDOW - W), MASK_VALUE, F32)],
                                               axis=1)
                    pv, m, l = block_stats(cls_ref[bb, Q_SLOT, rl, a:a + Q_BLOCK, :], k2, v2, bias)
                    cls = c * cpc + rl
                    if d == 4:
                        rows = pl.ds(pl.multiple_of(cls * L + a, Q_BLOCK), Q_BLOCK)
                    else:
                        rows = pl.ds((cls % 4) * (S // 4) + (d // 4) * a + cls // 4, Q_BLOCK, stride=4)
                    merge(bb, rows, pv, m, l, mode)

    def contiguous_phase():
        Sc = S // C
        nblk = Sc // Q_BLOCK

        def window(bb, slot, a):
            lo, hi = a - RADIUS, a + Q_BLOCK + RADIUS
            parts = []
            if lo < 0:
                parts.append(prev_ref[bb, slot])
            parts.append(nat_ref[bb, slot, max(lo, 0):min(hi, Sc), :])
            if hi > Sc:
                parts.append(next_ref[bb, slot])
            return parts[0] if len(parts) == 1 else jnp.concatenate(parts, axis=0)

        for bb in range(NB):
            for i in range(nblk):
                a = i * Q_BLOCK
                typ = 0
                if i == 0:
                    typ = jnp.where(c == 0, 1, typ)
                if i == nblk - 1:
                    typ = jnp.where(c == C - 1, 2, typ)
                pv, m, l = block_stats(nat_ref[bb, Q_SLOT, a:a + Q_BLOCK, :], window(bb, K_SLOT, a),
                                       window(bb, V_SLOT, a), b1_ref[typ])
                pos = c * Sc + a
                part = Q_BLOCK // 4
                rows = [pl.ds(pl.multiple_of(r4 * (S // 4) + pos // 4, part), part) for r4 in range(4)]
                out = merge(bb, rows, pv, m, l, "last")
                for r4 in range(4):
                    o_ref[bb, pl.ds(pos + r4, part, stride=4), :] = out[r4 * part:(r4 + 1) * part, :]

    @pl.when(phase == 0)
    def _():
        strided_phase(4, c4_ref, b4_ref, "first")

    @pl.when(phase == 1)
    def _():
        strided_phase(16, c16_ref, b16_ref, "mid")

    @pl.when(phase == 2)
    def _():
        contiguous_phase()


def _attention(nat_kvq, c4_kvq, c16_kvq, B, S, C, NB):
    Sc = S // C
    assert Sc % Q_BLOCK == 0 and 4 % C == 0 and B % NB == 0
    halo_blocks = S // RADIUS

    def chunk_of(own_phase, p, c):
        return jnp.where(p < own_phase, 0, jnp.where(p > own_phase, C - 1, c))

    def cls_spec(d, own_phase):
        L = S // d
        assert L % Q_BLOCK == 0 and (L == Q_BLOCK or L >= KEY_WINDOW)
        return pl.BlockSpec((NB, None, 3, d // C, L, LANES),
                            lambda b, h, p, c: (b, h, 0, chunk_of(own_phase, p, c), 0, 0))

    nat = pl.BlockSpec((NB, None, 3, Sc, LANES), lambda b, h, p, c: (b, h, 0, chunk_of(2, p, c), 0))
    prev = pl.BlockSpec((NB, None, 2, RADIUS, LANES),
                        lambda b, h, p, c: (b, h, 0, jnp.maximum(chunk_of(2, p, c) * (Sc // RADIUS) - 1, 0), 0))
    nxt = pl.BlockSpec((NB, None, 2, RADIUS, LANES),
                       lambda b, h, p, c: (b, h, 0, jnp.minimum((chunk_of(2, p, c) + 1) * (Sc // RADIUS),
                                                                 halo_blocks - 1), 0))
    c4, c16 = cls_spec(4, 0), cls_spec(16, 1)
    W4, W16 = min(KEY_WINDOW, S // 4), min(KEY_WINDOW, S // 16)
    clamped = (0, -RADIUS, -2 * RADIUS)
    b1 = _band_bias((-RADIUS,) * 3, KEY_WINDOW,
                    extra=(lambda kj: kj >= 0, lambda kj: kj >= RADIUS, lambda kj: kj < KEY_WINDOW - RADIUS))
    b1 = b1[:, _block_order(), :]
    b4, b16 = _band_bias(clamped, W4), _band_bias(clamped, W16)
    return pl.pallas_call(
        functools.partial(_attn_kernel, S=S, C=C, NB=NB),
        grid=(B // NB, N_HEAD_PAIRS, 3, C),
        in_specs=[nat, prev, nxt, c4, c16, _resident(b1.shape), _resident(b4.shape), _resident(b16.shape)],
        out_specs=pl.BlockSpec((NB, None, S, LANES), lambda b, h, p, c: (b, h, 0, 0)),
        out_shape=jax.ShapeDtypeStruct((B, N_HEAD_PAIRS, S, LANES), F32),
        scratch_shapes=[pltpu.VMEM((NB, S, LANES), F32)] * 3,
        compiler_params=_params(4),
        name="attn",
    )(nat_kvq, nat_kvq, nat_kvq, c4_kvq, c16_kvq, b1, b4, b16)


def _dft_kernel(bc_ref, nbs_ref, r1_ref, r2_ref, a_ref, o_ref):
    t = pl.program_id(2)

    @pl.when(t == 0)
    def _():
        o_ref[...] = jnp.zeros_like(o_ref)

    for par in range(2):
        bc, nbs = bc_ref[par], nbs_ref[par]
        r1, r2 = r1_ref[par], r2_ref[par]
        t_cos = (bc * r1 + nbs * r2).astype(BF16)
        t_nsin = (nbs * r1 - bc * r2).astype(BF16)
        o_ref[par] += _dot(t_cos, a_ref[par, 0]) + _dot(t_nsin, a_ref[par, 1])

    @pl.when(t == pl.num_programs(2) - 1)
    def _():
        f0, f1 = o_ref[0], o_ref[1]
        o_ref[0] = f0 + f1
        o_ref[1] = f0 - f1


def _seq_dft(a, S, ts, tk, tn):
    ncols = a.shape[-1]
    H = S // 2
    t = 2 * jnp.arange(H, dtype=jnp.int32)[None, None, :] + jnp.arange(2, dtype=jnp.int32)[:, None, None]
    w = 2.0 * math.pi / S
    ang_r = ((jnp.arange(0, H, ts, dtype=jnp.int32)[None, :, None] * t) % S).astype(F32) * w
    lo = 32
    ang_hi = ((jnp.arange(0, ts, lo, dtype=jnp.int32)[None, :, None] * t) % S).astype(F32) * w
    ang_lo = ((jnp.arange(lo, dtype=jnp.int32)[None, :, None] * t) % S).astype(F32) * w
    c_hi, s_hi = jnp.cos(ang_hi)[:, :, None, :], jnp.sin(ang_hi)[:, :, None, :]
    c_lo, s_lo = jnp.cos(ang_lo)[:, None, :, :], jnp.sin(ang_lo)[:, None, :, :]
    bc = (c_hi * c_lo - s_hi * s_lo).reshape(2, ts, H)
    nbs = -(s_hi * c_lo + c_hi * s_lo).reshape(2, ts, H)
    scale = S ** -0.5
    r1 = (jnp.cos(ang_r) * scale).reshape(2, H // ts, 1, H)
    r2 = (jnp.sin(ang_r) * scale).reshape(2, H // ts, 1, H)
    btab = pl.BlockSpec((2, ts, tk), lambda s, n, t: (0, 0, t))
    rtab = pl.BlockSpec((2, None, 1, tk), lambda s, n, t: (0, s, 0, t))
    out = pl.pallas_call(
        _dft_kernel,
        grid=(H // ts, ncols // tn, H // tk),
        in_specs=[btab, btab, rtab, rtab, pl.BlockSpec((2, 2, tk, tn), lambda s, n, t: (0, 0, t, n))],
        out_specs=pl.BlockSpec((2, ts, tn), lambda s, n, t: (0, s, n)),
        out_shape=jax.ShapeDtypeStruct((2, H, ncols), F32),
        compiler_params=_params(3),
        name="seq_dft",
    )(bc, nbs, r1, r2, a)
    return out.reshape(S, ncols)


def _out_ffn_kernel(x1_ref, of_ref, oa_ref, gf_ref, ga_ref, wout_ref, g2_ref, wg_ref, wu_ref, wd_ref,
                    gfin_ref, y_ref):
    oa = jnp.concatenate([oa_ref[hp] for hp in range(N_HEAD_PAIRS)], axis=1)
    na = _rms(oa, ga_ref[...]).astype(BF16)
    nf = _rms(of_ref[...], gf_ref[...]).astype(BF16)
    x2 = x1_ref[...] + _dot(nf, wout_ref[:D_FOURIER, :]) + _dot(na, wout_ref[D_FOURIER:, :])
    h = _rms(x2, g2_ref[...]).astype(BF16)
    x3 = x2 + 0.5 * _swiglu(h, wg_ref, wu_ref, wd_ref)
    y_ref[...] = _rms(x3, gfin_ref[...])


def _out_ffn(x1, of, oa, B, S, gf, ga, wout, g2, wg, wu, wd, gfin, tm):
    N = B * S
    nst = S // tm
    tok = pl.BlockSpec((tm, D_MODEL), lambda i: (i, 0))
    att = pl.BlockSpec((None, N_HEAD_PAIRS, tm, LANES), lambda i: (i // nst, 0, i % nst, 0))
    return pl.pallas_call(
        _out_ffn_kernel,
        grid=(N // tm,),
        in_specs=[tok, pl.BlockSpec((tm, D_FOURIER), lambda i: (i % nst, i // nst)), att,
                  _resident((1, D_FOURIER)), _resident((1, D_ATTN)), _resident((D_MODEL, D_MODEL)),
                  _resident((1, D_MODEL)), _resident((D_MODEL, D_FF)), _resident((D_MODEL, D_FF)),
                  _resident((D_FF, D_MODEL)), _resident((1, D_MODEL))],
        out_specs=tok,
        out_shape=jax.ShapeDtypeStruct((N, D_MODEL), F32),
        compiler_params=_params(1),
        name="out_ffn",
    )(x1, of, oa, gf, ga, wout, g2, wg, wu, wd, gfin)


def _row(g):
    return g.reshape(1, -1).astype(F32)


def _trunk(x, p, win):
    B, S, _ = x.shape
    x1 = _ffn1(x.reshape(B * S, D_MODEL), _row(p["ffn1_norm"]), p["ffn1_w_gate"], p["ffn1_w_up"],
               p["ffn1_w_down"], tm=FFN_TOKENS)
    a, nat_kvq, c4_kvq, c16_kvq = _proj(x1, B, S, _row(p["mix_norm"]), win, tm=PROJ_TOKENS)
    blocks_per_step = 32
    C = max(1, S // (blocks_per_step * Q_BLOCK))
    NB = max(1, blocks_per_step * Q_BLOCK // S)
    oa = _attention(nat_kvq, c4_kvq, c16_kvq, B, S, C, NB)
    of = _seq_dft(a, S, ts=512, tk=512, tn=min(2048, a.shape[-1]))
    y = _out_ffn(x1, of, oa, B, S, _row(p["fourier_out_norm"]), _row(p["attn_out_norm"]), p["w_out"],
                 _row(p["ffn2_norm"]), p["ffn2_w_gate"], p["ffn2_w_up"], p["ffn2_w_down"],
                 _row(p["final_norm"]), tm=FFN_TOKENS)
    return y.reshape(B, S, D_MODEL)


def kernel(x_prompt, x_sample, ffn1_norm, ffn1_w_gate, ffn1_w_up, ffn1_w_down, mix_norm, w_in, fourier_w,
           fourier_out_norm, attn_out_norm, w_out, ffn2_norm, ffn2_w_gate, ffn2_w_up, ffn2_w_down, final_norm):
    assert ffn1_w_gate.shape[0] == 1, "single-layer trunk"
    folded = _fold_fourier_weights(w_in[0, :, :D_FOURIER], fourier_w[0])
    win = jnp.concatenate([folded, w_in[0, :, D_FOURIER:].astype(BF16)], axis=1)
    p = dict(
        ffn1_norm=ffn1_norm[0], ffn1_w_gate=ffn1_w_gate[0].astype(BF16), ffn1_w_up=ffn1_w_up[0].astype(BF16),
        ffn1_w_down=ffn1_w_down[0].astype(BF16), mix_norm=mix_norm[0],
        fourier_out_norm=fourier_out_norm[0], attn_out_norm=attn_out_norm[0], w_out=w_out[0].astype(BF16),
        ffn2_norm=ffn2_norm[0], ffn2_w_gate=ffn2_w_gate[0].astype(BF16), ffn2_w_up=ffn2_w_up[0].astype(BF16),
        ffn2_w_down=ffn2_w_down[0].astype(BF16), final_norm=final_norm)
    return (_trunk(x_prompt, p, win), _trunk(x_sample, p, win))
```

```python
import functools
import math

import jax
import jax.numpy as jnp
import numpy as np
from jax import lax
from jax.experimental import pallas as pl
from jax.experimental.pallas import tpu as pltpu

F32 = jnp.float32
BF16 = jnp.bfloat16

D_MODEL = 1024
D_FF = 2816
D_FOURIER = 256
N_FOURIER_GROUPS = 4
FOURIER_GROUP_DIM = 64
D_ATTN = 768
HEAD_DIM = 64
LANES = 128
N_HEAD_PAIRS = D_ATTN // LANES
D_PROJ = 2 * D_FOURIER + 3 * D_ATTN
RADIUS = 64
ROPE_THETA = 10000.0
RMS_EPS = 1e-6
MASK_VALUE = -1e30

FF_CHUNK = 256
FFN_TOKENS = 1024
PROJ_TOKENS = 1024
SPLIT_BUFFERS = 6
Q_BLOCK = 128
KEY_WINDOW = Q_BLOCK + 2 * RADIUS
LOG2_E = math.log2(math.e)
VMEM_LIMIT = 56 * 1024 * 1024


def _params(n_grid_dims):
    return pltpu.CompilerParams(dimension_semantics=("arbitrary",) * n_grid_dims,
                                vmem_limit_bytes=VMEM_LIMIT)


def _resident(shape):
    nd = len(shape)
    return pl.BlockSpec(shape, lambda *_: (0,) * nd, pipeline_mode=pl.Buffered(1))


def _rms(x, g):
    inv = lax.rsqrt(jnp.mean(x * x, axis=-1, keepdims=True) + RMS_EPS)
    return x * inv * g


def _dot(a, b):
    return jnp.dot(a, b, preferred_element_type=F32)


def _swiglu(h, wg_ref, wu_ref, wd_ref):
    acc = None
    for c in range(D_FF // FF_CHUNK):
        sl = slice(c * FF_CHUNK, (c + 1) * FF_CHUNK)
        g = _dot(h, wg_ref[:, sl])
        u = _dot(h, wu_ref[:, sl])
        a = (g * jax.nn.sigmoid(g) * u).astype(BF16)
        part = _dot(a, wd_ref[sl, :])
        acc = part if acc is None else acc + part
    return acc


def _fold_kernel(winf_ref, cbd_ref, sbd_ref, wbd_ref, o_ref):
    hi = lax.Precision.HIGHEST
    scale = FOURIER_GROUP_DIM ** -0.5
    gc = jnp.dot(cbd_ref[...], wbd_ref[...], precision=hi, preferred_element_type=F32) * scale
    gs = jnp.dot(sbd_ref[...], wbd_ref[...], precision=hi, preferred_element_type=F32) * scale
    winf = winf_ref[...]
    o_ref[:, :D_FOURIER] = jnp.dot(winf, gc, precision=hi, preferred_element_type=F32).astype(BF16)
    o_ref[:, D_FOURIER:] = jnp.dot(winf, gs, precision=hi, preferred_element_type=F32).astype(BF16)


def _fold_fourier_weights(w_in_f, fourier_w):
    c = np.arange(FOURIER_GROUP_DIM)
    ang = 2.0 * np.pi * ((c[:, None] * c[None, :]) % FOURIER_GROUP_DIM) / FOURIER_GROUP_DIM
    eye = np.eye(N_FOURIER_GROUPS)
    cbd = jnp.asarray(np.kron(eye, np.cos(ang)), F32)
    sbd = jnp.asarray(np.kron(eye, np.sin(ang)), F32)
    wbd = jnp.zeros((D_FOURIER, D_FOURIER), F32)
    for g in range(N_FOURIER_GROUPS):
        s = g * FOURIER_GROUP_DIM
        wbd = lax.dynamic_update_slice(wbd, fourier_w[g].astype(F32), (s, s))
    return pl.pallas_call(
        _fold_kernel,
        out_shape=jax.ShapeDtypeStruct((D_MODEL, 2 * D_FOURIER), BF16),
        name="fold_fourier",
    )(w_in_f, cbd, sbd, wbd)


def _ffn1_kernel(x_ref, g1_ref, wg_ref, wu_ref, wd_ref, x1_ref):
    x = x_ref[...]
    h = _rms(x, g1_ref[...]).astype(BF16)
    x1_ref[...] = x + 0.5 * _swiglu(h, wg_ref, wu_ref, wd_ref)


def _ffn1(xf, g1, wg, wu, wd, tm):
    N = xf.shape[0]
    tok = pl.BlockSpec((tm, D_MODEL), lambda i: (i, 0))
    return pl.pallas_call(
        _ffn1_kernel,
        grid=(N // tm,),
        in_specs=[tok, _resident((1, D_MODEL)), _resident((D_MODEL, D_FF)), _resident((D_MODEL, D_FF)),
                  _resident((D_FF, D_MODEL))],
        out_specs=tok,
        out_shape=jax.ShapeDtypeStruct((N, D_MODEL), F32),
        compiler_params=_params(1),
        name="ffn1",
    )(xf, g1, wg, wu, wd)


def _rope(t, cos, sin_lo, sin_hi):
    return (t * cos + pltpu.roll(t, LANES - HEAD_DIM // 2, axis=1) * sin_lo
            + pltpu.roll(t, HEAD_DIM // 2, axis=1) * sin_hi)


def _proj_kernel(x1_ref, gm_ref, win_ref, cos_ref, slo_ref, shi_ref,
                 a_ref, o4_ref, o16_ref,
                 nat_ref, cls4_ref, pa_ref):
    tm = x1_ref.shape[0]
    h2 = _rms(x1_ref[...], gm_ref[...]).astype(BF16)

    def emit(t, hp, slot):
        e = (hp * 3 + slot) % nat_ref.shape[0]
        nat_ref[e] = t
        for r4 in range(4):
            c4 = nat_ref[e, pl.ds(r4, tm // 4, stride=4), :]
            o4_ref[hp, slot, r4] = c4.astype(BF16)
            cls4_ref[e, r4] = c4
            for j in range(4):
                o16_ref[hp, slot, r4 + 4 * j] = cls4_ref[e, r4, pl.ds(j, tm // 16, stride=4), :].astype(BF16)

    cos = cos_ref[...]
    slo = slo_ref[...]
    shi = shi_ref[...]
    base = 2 * D_FOURIER
    for j in range(D_ATTN // 256):
        c0 = j * 256
        qq = _dot(h2, win_ref[:, base + c0: base + c0 + 256])
        kk = _dot(h2, win_ref[:, base + D_ATTN + c0: base + D_ATTN + c0 + 256])
        vv = _dot(h2, win_ref[:, base + 2 * D_ATTN + c0: base + 2 * D_ATTN + c0 + 256])
        for i in range(2):
            sl = slice(i * LANES, (i + 1) * LANES)
            hp = 2 * j + i
            emit(_rope(qq[:, sl], cos, slo, shi) * (HEAD_DIM ** -0.5 * LOG2_E), hp, Q_SLOT)
            emit(_rope(kk[:, sl], cos, slo, shi), hp, K_SLOT)
            emit(vv[:, sl], hp, V_SLOT)

    pa = _dot(h2, win_ref[:, :2 * D_FOURIER])
    tiles_per_part = D_FOURIER // LANES
    for j in range(2 * tiles_per_part):
        pa_ref[j] = pa[:, j * LANES:(j + 1) * LANES]
    for par in range(2):
        for j in range(2 * tiles_per_part):
            lanes = slice((j % tiles_per_part) * LANES, (j % tiles_per_part + 1) * LANES)
            a_ref[par, j // tiles_per_part, :, lanes] = pa_ref[j, pl.ds(par, tm // 2, stride=2), :].astype(BF16)


def _rope_tables(S):
    inv_freq = ROPE_THETA ** (-jnp.arange(0, HEAD_DIM, 2, dtype=F32) / HEAD_DIM)
    freqs = jnp.arange(S, dtype=F32)[:, None] * inv_freq[None, :]
    cos, sin = jnp.cos(freqs), jnp.sin(freqs)
    zero = jnp.zeros_like(sin)
    cos_t = jnp.tile(cos, (1, LANES // (HEAD_DIM // 2)))
    sin_lo = jnp.tile(jnp.concatenate([-sin, zero], axis=1), (1, LANES // HEAD_DIM))
    sin_hi = jnp.tile(jnp.concatenate([zero, sin], axis=1), (1, LANES // HEAD_DIM))
    return cos_t, sin_lo, sin_hi


def _proj(x1, B, S, gm, win, tm):
    N = B * S
    nst = S // tm
    cos_t, sin_lo, sin_hi = _rope_tables(S)
    tok = pl.BlockSpec((tm, D_MODEL), lambda i: (i, 0))
    tab = pl.BlockSpec((tm, LANES), lambda i: (i % nst, 0))

    def cls(d):
        spec = pl.BlockSpec((None, N_HEAD_PAIRS, 3, d, tm // d, LANES),
                            lambda i: (i // nst, 0, 0, 0, i % nst, 0))
        return spec, jax.ShapeDtypeStruct((B, N_HEAD_PAIRS, 3, d, S // d, LANES), BF16)

    c4, c4_shape = cls(4)
    c16, c16_shape = cls(16)
    return pl.pallas_call(
        _proj_kernel,
        grid=(N // tm,),
        in_specs=[tok, _resident((1, D_MODEL)), _resident((D_MODEL, D_PROJ)), tab, tab, tab],
        out_specs=[pl.BlockSpec((2, 2, tm // 2, D_FOURIER), lambda i: (0, 0, i % nst, i // nst)),
                   c4, c16],
        out_shape=[jax.ShapeDtypeStruct((2, 2, S // 2, B * D_FOURIER), BF16), c4_shape, c16_shape],
        scratch_shapes=[pltpu.VMEM((SPLIT_BUFFERS, tm, LANES), F32),
                        pltpu.VMEM((SPLIT_BUFFERS, 4, tm // 4, LANES), F32),
                        pltpu.VMEM((2 * D_FOURIER // LANES, tm, LANES), F32)],
        compiler_params=_params(1),
        name="proj",
    )(x1, gm, win, cos_t, sin_lo, sin_hi)


def _class_order(n):
    return np.arange(n).reshape(n // 4, 4).T.reshape(-1)


def _band_bias(offsets, W, extra=None):
    qi = np.arange(Q_BLOCK)[:, None]
    kj = np.arange(W)[None, :]
    tiles = []
    for t, off in enumerate(offsets):
        ok = np.abs(kj + off - qi) <= RADIUS
        if extra is not None:
            ok &= extra[t](kj)
        tiles.append(np.where(ok, 0.0, MASK_VALUE))
    return jnp.asarray(np.stack(tiles), F32)


K_SLOT, V_SLOT, Q_SLOT = 0, 1, 2


def _attn_kernel(c4_ref, c16_ref, rows_ref, prev_ref, next_ref, b1_ref, b4_ref, b16_ref,
                 o_ref, acc_ref, m_ref, l_ref, *, S, C, NB):
    phase = pl.program_id(2)
    c = pl.program_id(3)
    lane = lax.broadcasted_iota(jnp.int32, (1, LANES), 1)
    first_head = lane < HEAD_DIM
    contract_last = (((1,), (1,)), ((), ()))

    def block_stats(q2, k2, v2, bias):
        v_ones = jnp.concatenate([v2, jnp.ones_like(v2)], axis=1)
        per_head = []
        for head_mask in (first_head, jnp.logical_not(first_head)):
            qh = jnp.where(head_mask, q2, jnp.zeros_like(q2))
            s = lax.dot_general(qh, k2, contract_last, preferred_element_type=F32) + bias
            m = jnp.max(s, axis=-1, keepdims=True)
            p = jnp.exp2(s - m).astype(BF16)
            pvl = _dot(p, v_ones)
            per_head.append((pvl[:, :LANES], jnp.broadcast_to(m, (Q_BLOCK, LANES)), pvl[:, LANES:]))
        return tuple(jnp.where(first_head, a, b) for a, b in zip(*per_head))

    def state(ref, bb, rows):
        if isinstance(rows, (list, tuple)):
            return jnp.concatenate([ref[bb, r, :] for r in rows], axis=0)
        return ref[bb, rows, :]

    def merge(bb, rows, pv, m, l, mode):
        if mode == "first":
            acc_ref[bb, rows, :] = pv
            m_ref[bb, rows, :] = m
            l_ref[bb, rows, :] = l
            return None
        m_old = state(m_ref, bb, rows)
        m_new = jnp.maximum(m_old, m)
        a_old = jnp.exp2(m_old - m_new)
        a_blk = jnp.exp2(m - m_new)
        acc = a_old * state(acc_ref, bb, rows) + a_blk * pv
        den = a_old * state(l_ref, bb, rows) + a_blk * l
        if mode == "last":
            return acc / den
        acc_ref[bb, rows, :] = acc
        m_ref[bb, rows, :] = m_new
        l_ref[bb, rows, :] = den
        return None

    def strided_phase(d, cls_ref, bias_ref, mode):
        L = S // d
        nblk = L // Q_BLOCK
        W = min(KEY_WINDOW, L)
        cpc = d // C
        for bb in range(NB):
            for rl in range(cpc):
                for i in range(nblk):
                    a = i * Q_BLOCK
                    ws = min(max(a - RADIUS, 0), L - W)
                    typ = 0 if i == 0 else (2 if i == nblk - 1 else 1)
                    k2 = cls_ref[bb, K_SLOT, rl, ws:ws + W, :]
                    v2 = cls_ref[bb, V_SLOT, rl, ws:ws + W, :]
                    bias = bias_ref[typ]
                    if W < KEY_WINDOW:
                        pad = jnp.zeros((KEY_WINDOW - W, LANES), BF16)
                        k2 = jnp.concatenate([k2, pad], axis=0)
                        v2 = jnp.concatenate([v2, pad], axis=0)
                        bias = jnp.concatenate([bias, jnp.full((Q_BLOCK, KEY_WINDOW - W), MASK_VALUE, F32)],
                                               axis=1)
                    pv, m, l = block_stats(cls_ref[bb, Q_SLOT, rl, a:a + Q_BLOCK, :], k2, v2, bias)
                    cls = c * cpc + rl
                    if d == 4:
                        rows = pl.ds(pl.multiple_of(cls * L + a, Q_BLOCK), Q_BLOCK)
                    else:
                        rows = pl.ds((cls % 4) * (S // 4) + (d // 4) * a + cls // 4, Q_BLOCK, stride=4)
                    merge(bb, rows, pv, m, l, mode)

    def contiguous_phase():
        Sc = S // C
        nblk = Sc // Q_BLOCK
        Lc = Sc // 4
        halo = RADIUS // 4

        def gather(bb, slot, lo, hi):
            parts = []
            for r4 in range(4):
                if lo < 0:
                    parts.append(prev_ref[bb, slot, r4])
                parts.append(rows_ref[bb, slot, r4, max(lo, 0) // 4:min(hi, Sc) // 4, :])
                if hi > Sc:
                    parts.append(next_ref[bb, slot, r4])
            return jnp.concatenate(parts, axis=0)

        assert prev_ref.shape[-2] == halo and Lc % halo == 0
        for bb in range(NB):
            for i in range(nblk):
                a = i * Q_BLOCK
                typ = 0
                if i == 0:
                    typ = jnp.where(c == 0, 1, typ)
                if i == nblk - 1:
                    typ = jnp.where(c == C - 1, 2, typ)
                pv, m, l = block_stats(gather(bb, Q_SLOT, a, a + Q_BLOCK),
                                       gather(bb, K_SLOT, a - RADIUS, a + Q_BLOCK + RADIUS),
                                       gather(bb, V_SLOT, a - RADIUS, a + Q_BLOCK + RADIUS), b1_ref[typ])
                pos = c * Sc + a
                part = Q_BLOCK // 4
                rows = [pl.ds(pl.multiple_of(r4 * (S // 4) + pos // 4, part), part) for r4 in range(4)]
                out = merge(bb, rows, pv, m, l, "last")
                for r4 in range(4):
                    o_ref[bb, pl.ds(pos + r4, part, stride=4), :] = out[r4 * part:(r4 + 1) * part, :]

    @pl.when(phase == 0)
    def _():
        strided_phase(4, c4_ref, b4_ref, "first")

    @pl.when(phase == 1)
    def _():
        strided_phase(16, c16_ref, b16_ref, "mid")

    @pl.when(phase == 2)
    def _():
        contiguous_phase()


def _attention(c4_kvq, c16_kvq, B, S, C, NB):
    Sc = S // C
    assert Sc % Q_BLOCK == 0 and 4 % C == 0 and B % NB == 0
    halo = RADIUS // 4
    Lc = Sc // 4
    halo_blocks = S // 4 // halo

    def chunk_of(own_phase, p, c):
        return jnp.where(p < own_phase, 0, jnp.where(p > own_phase, C - 1, c))

    def cls_spec(d, own_phase):
        L = S // d
        assert L % Q_BLOCK == 0 and (L == Q_BLOCK or L >= KEY_WINDOW)
        return pl.BlockSpec((NB, None, 3, d // C, L, LANES),
                            lambda b, h, p, c: (b, h, 0, chunk_of(own_phase, p, c), 0, 0))

    rows = pl.BlockSpec((NB, None, 3, 4, Lc, LANES), lambda b, h, p, c: (b, h, 0, 0, chunk_of(2, p, c), 0))
    prev = pl.BlockSpec((NB, None, 2, 4, halo, LANES),
                        lambda b, h, p, c: (b, h, 0, 0, jnp.maximum(chunk_of(2, p, c) * (Lc // halo) - 1, 0), 0))
    nxt = pl.BlockSpec((NB, None, 2, 4, halo, LANES),
                       lambda b, h, p, c: (b, h, 0, 0, jnp.minimum((chunk_of(2, p, c) + 1) * (Lc // halo),
                                                                    halo_blocks - 1), 0))
    c4, c16 = cls_spec(4, 0), cls_spec(16, 1)
    W4, W16 = min(KEY_WINDOW, S // 4), min(KEY_WINDOW, S // 16)
    clamped = (0, -RADIUS, -2 * RADIUS)
    b1 = _band_bias((-RADIUS,) * 3, KEY_WINDOW,
                    extra=(lambda kj: kj >= 0, lambda kj: kj >= RADIUS, lambda kj: kj < KEY_WINDOW - RADIUS))
    b1 = b1[:, _class_order(Q_BLOCK), :][:, :, _class_order(KEY_WINDOW)]
    b4, b16 = _band_bias(clamped, W4), _band_bias(clamped, W16)
    return pl.pallas_call(
        functools.partial(_attn_kernel, S=S, C=C, NB=NB),
        grid=(B // NB, N_HEAD_PAIRS, 3, C),
        in_specs=[c4, c16, rows, prev, nxt, _resident(b1.shape), _resident(b4.shape), _resident(b16.shape)],
        out_specs=pl.BlockSpec((NB, None, S, LANES), lambda b, h, p, c: (b, h, 0, 0)),
        out_shape=jax.ShapeDtypeStruct((B, N_HEAD_PAIRS, S, LANES), F32),
        scratch_shapes=[pltpu.VMEM((NB, S, LANES), F32)] * 3,
        compiler_params=_params(4),
        name="attn",
    )(c4_kvq, c16_kvq, c4_kvq, c4_kvq, c4_kvq, b1, b4, b16)


def _dft_kernel(bc_ref, nbs_ref, r1_ref, r2_ref, a_ref, o_ref):
    t = pl.program_id(2)

    @pl.when(t == 0)
    def _():
        o_ref[...] = jnp.zeros_like(o_ref)

    for par in range(2):
        bc, nbs = bc_ref[par], nbs_ref[par]
        r1, r2 = r1_ref[par], r2_ref[par]
        t_cos = (bc * r1 + nbs * r2).astype(BF16)
        t_nsin = (nbs * r1 - bc * r2).astype(BF16)
        o_ref[par] += _dot(t_cos, a_ref[par, 0]) + _dot(t_nsin, a_ref[par, 1])

    @pl.when(t == pl.num_programs(2) - 1)
    def _():
        f0, f1 = o_ref[0], o_ref[1]
        o_ref[0] = f0 + f1
        o_ref[1] = f0 - f1


def _seq_dft(a, S, ts, tk, tn):
    ncols = a.shape[-1]
    H = S // 2
    t = 2 * jnp.arange(H, dtype=jnp.int32)[None, None, :] + jnp.arange(2, dtype=jnp.int32)[:, None, None]
    w = 2.0 * math.pi / S
    ang_r = ((jnp.arange(0, H, ts, dtype=jnp.int32)[None, :, None] * t) % S).astype(F32) * w
    lo = 32
    ang_hi = ((jnp.arange(0, ts, lo, dtype=jnp.int32)[None, :, None] * t) % S).astype(F32) * w
    ang_lo = ((jnp.arange(lo, dtype=jnp.int32)[None, :, None] * t) % S).astype(F32) * w
    c_hi, s_hi = jnp.cos(ang_hi)[:, :, None, :], jnp.sin(ang_hi)[:, :, None, :]
    c_lo, s_lo = jnp.cos(ang_lo)[:, None, :, :], jnp.sin(ang_lo)[:, None, :, :]
    bc = (c_hi * c_lo - s_hi * s_lo).reshape(2, ts, H)
    nbs = -(s_hi * c_lo + c_hi * s_lo).reshape(2, ts, H)
    scale = S ** -0.5
    r1 = (jnp.cos(ang_r) * scale).reshape(2, H // ts, 1, H)
    r2 = (jnp.sin(ang_r) * scale).reshape(2, H // ts, 1, H)
    btab = pl.BlockSpec((2, ts, tk), lambda s, n, t: (0, 0, t))
    rtab = pl.BlockSpec((2, None, 1, tk), lambda s, n, t: (0, s, 0, t))
    out = pl.pallas_call(
        _dft_kernel,
        grid=(H // ts, ncols // tn, H // tk),
        in_specs=[btab, btab, rtab, rtab, pl.BlockSpec((2, 2, tk, tn), lambda s, n, t: (0, 0, t, n))],
        out_specs=pl.BlockSpec((2, ts, tn), lambda s, n, t: (0, s, n)),
        out_shape=jax.ShapeDtypeStruct((2, H, ncols), F32),
        compiler_params=_params(3),
        name="seq_dft",
    )(bc, nbs, r1, r2, a)
    return out.reshape(S, ncols)


def _out_ffn_kernel(x1_ref, of_ref, oa_ref, gf_ref, ga_ref, wout_ref, g2_ref, wg_ref, wu_ref, wd_ref,
                    gfin_ref, y_ref):
    oa = jnp.concatenate([oa_ref[hp] for hp in range(N_HEAD_PAIRS)], axis=1)
    na = _rms(oa, ga_ref[...]).astype(BF16)
    nf = _rms(of_ref[...], gf_ref[...]).astype(BF16)
    x2 = x1_ref[...] + _dot(nf, wout_ref[:D_FOURIER, :]) + _dot(na, wout_ref[D_FOURIER:, :])
    h = _rms(x2, g2_ref[...]).astype(BF16)
    x3 = x2 + 0.5 * _swiglu(h, wg_ref, wu_ref, wd_ref)
    y_ref[...] = _rms(x3, gfin_ref[...])


def _out_ffn(x1, of, oa, B, S, gf, ga, wout, g2, wg, wu, wd, gfin, tm):
    N = B * S
    nst = S // tm
    tok = pl.BlockSpec((tm, D_MODEL), lambda i: (i, 0))
    att = pl.BlockSpec((None, N_HEAD_PAIRS, tm, LANES), lambda i: (i // nst, 0, i % nst, 0))
    return pl.pallas_call(
        _out_ffn_kernel,
        grid=(N // tm,),
        in_specs=[tok, pl.BlockSpec((tm, D_FOURIER), lambda i: (i % nst, i // nst)), att,
                  _resident((1, D_FOURIER)), _resident((1, D_ATTN)), _resident((D_MODEL, D_MODEL)),
                  _resident((1, D_MODEL)), _resident((D_MODEL, D_FF)), _resident((D_MODEL, D_FF)),
                  _resident((D_FF, D_MODEL)), _resident((1, D_MODEL))],
        out_specs=tok,
        out_shape=jax.ShapeDtypeStruct((N, D_MODEL), F32),
        compiler_params=_params(1),
        name="out_ffn",
    )(x1, of, oa, gf, ga, wout, g2, wg, wu, wd, gfin)


def _row(g):
    return g.reshape(1, -1).astype(F32)


def _trunk(x, p, win):
    B, S, _ = x.shape
    x1 = _ffn1(x.reshape(B * S, D_MODEL), _row(p["ffn1_norm"]), p["ffn1_w_gate"], p["ffn1_w_up"],
               p["ffn1_w_down"], tm=FFN_TOKENS)
    a, c4_kvq, c16_kvq = _proj(x1, B, S, _row(p["mix_norm"]), win, tm=PROJ_TOKENS)
    blocks_per_step = 32
    C = max(1, S // (blocks_per_step * Q_BLOCK))
    NB = max(1, blocks_per_step * Q_BLOCK // S)
    oa = _attention(c4_kvq, c16_kvq, B, S, C, NB)
    of = _seq_dft(a, S, ts=512, tk=512, tn=min(2048, a.shape[-1]))
    y = _out_ffn(x1, of, oa, B, S, _row(p["fourier_out_norm"]), _row(p["attn_out_norm"]), p["w_out"],
                 _row(p["ffn2_norm"]), p["ffn2_w_gate"], p["ffn2_w_up"], p["ffn2_w_down"],
                 _row(p["final_norm"]), tm=FFN_TOKENS)
    return y.reshape(B, S, D_MODEL)


def kernel(x_prompt, x_sample, ffn1_norm, ffn1_w_gate, ffn1_w_up, ffn1_w_down, mix_norm, w_in, fourier_w,
           fourier_out_norm, attn_out_norm, w_out, ffn2_norm, ffn2_w_gate, ffn2_w_up, ffn2_w_down, final_norm):
    assert ffn1_w_gate.shape[0] == 1, "single-layer trunk"
    folded = _fold_fourier_weights(w_in[0, :, :D_FOURIER], fourier_w[0])
    win = jnp.concatenate([folded, w_in[0, :, D_FOURIER:].astype(BF16)], axis=1)
    p = dict(
        ffn1_norm=ffn1_norm[0], ffn1_w_gate=ffn1_w_gate[0].astype(BF16), ffn1_w_up=ffn1_w_up[0].astype(BF16),
        ffn1_w_down=ffn1_w_down[0].astype(BF16), mix_norm=mix_norm[0],
        fourier_out_norm=fourier_out_norm[0], attn_out_norm=attn_out_norm[0], w_out=w_out[0].astype(BF16),
        ffn2_norm=ffn2_norm[0], ffn2_w_gate=ffn2_w_gate[0].astype(BF16), ffn2_w_up=ffn2_w_up[0].astype(BF16),
        ffn2_w_down=ffn2_w_down[0].astype(BF16), final_norm=final_norm)
    return (_trunk(x_prompt, p, win), _trunk(x_sample, p, win))
```

```python
import functools
import math

import jax
import jax.numpy as jnp
import numpy as np
from jax import lax
from jax.experimental import pallas as pl
from jax.experimental.pallas import tpu as pltpu

F32 = jnp.float32
BF16 = jnp.bfloat16

D_MODEL = 1024
D_FF = 2816
D_FOURIER = 256
N_FOURIER_GROUPS = 4
FOURIER_GROUP_DIM = 64
D_ATTN = 768
HEAD_DIM = 64
LANES = 128
N_HEAD_PAIRS = D_ATTN // LANES
D_PROJ = 2 * D_FOURIER + 3 * D_ATTN
RADIUS = 64
ROPE_THETA = 10000.0
RMS_EPS = 1e-6
MASK_VALUE = -1e30

FF_CHUNK = 256
FFN_TOKENS = 1024
PROJ_TOKENS = 1024
SPLIT_BUFFERS = 6
Q_BLOCK = 128
KEY_WINDOW = Q_BLOCK + 2 * RADIUS
LOG2_E = math.log2(math.e)
VMEM_LIMIT = 56 * 1024 * 1024


def _params(n_grid_dims):
    return pltpu.CompilerParams(dimension_semantics=("arbitrary",) * n_grid_dims,
                                vmem_limit_bytes=VMEM_LIMIT)


def _resident(shape):
    nd = len(shape)
    return pl.BlockSpec(shape, lambda *_: (0,) * nd, pipeline_mode=pl.Buffered(1))


def _rms(x, g):
    inv = lax.rsqrt(jnp.mean(x * x, axis=-1, keepdims=True) + RMS_EPS)
    return x * inv * g


def _dot(a, b):
    return jnp.dot(a, b, preferred_element_type=F32)


def _swiglu(h, wg_ref, wu_ref, wd_ref):
    acc = None
    for c in range(D_FF // FF_CHUNK):
        sl = slice(c * FF_CHUNK, (c + 1) * FF_CHUNK)
        g = _dot(h, wg_ref[:, sl])
        u = _dot(h, wu_ref[:, sl])
        a = (g * jax.nn.sigmoid(g) * u).astype(BF16)
        part = _dot(a, wd_ref[sl, :])
        acc = part if acc is None else acc + part
    return acc


def _fold_kernel(winf_ref, cbd_ref, sbd_ref, wbd_ref, o_ref):
    hi = lax.Precision.HIGHEST
    scale = FOURIER_GROUP_DIM ** -0.5
    gc = jnp.dot(cbd_ref[...], wbd_ref[...], precision=hi, preferred_element_type=F32) * scale
    gs = jnp.dot(sbd_ref[...], wbd_ref[...], precision=hi, preferred_element_type=F32) * scale
    winf = winf_ref[...]
    o_ref[:, :D_FOURIER] = jnp.dot(winf, gc, precision=hi, preferred_element_type=F32).astype(BF16)
    o_ref[:, D_FOURIER:] = jnp.dot(winf, gs, precision=hi, preferred_element_type=F32).astype(BF16)


def _fold_fourier_weights(w_in_f, fourier_w):
    c = np.arange(FOURIER_GROUP_DIM)
    ang = 2.0 * np.pi * ((c[:, None] * c[None, :]) % FOURIER_GROUP_DIM) / FOURIER_GROUP_DIM
    eye = np.eye(N_FOURIER_GROUPS)
    cbd = jnp.asarray(np.kron(eye, np.cos(ang)), F32)
    sbd = jnp.asarray(np.kron(eye, np.sin(ang)), F32)
    wbd = jnp.zeros((D_FOURIER, D_FOURIER), F32)
    for g in range(N_FOURIER_GROUPS):
        s = g * FOURIER_GROUP_DIM
        wbd = lax.dynamic_update_slice(wbd, fourier_w[g].astype(F32), (s, s))
    return pl.pallas_call(
        _fold_kernel,
        out_shape=jax.ShapeDtypeStruct((D_MODEL, 2 * D_FOURIER), BF16),
        name="fold_fourier",
    )(w_in_f, cbd, sbd, wbd)


def _ffn1_kernel(x_ref, g1_ref, wg_ref, wu_ref, wd_ref, x1_ref):
    x = x_ref[...]
    h = _rms(x, g1_ref[...]).astype(BF16)
    x1_ref[...] = x + 0.5 * _swiglu(h, wg_ref, wu_ref, wd_ref)


def _ffn1(xf, g1, wg, wu, wd, tm):
    N = xf.shape[0]
    tok = pl.BlockSpec((tm, D_MODEL), lambda i: (i, 0))
    return pl.pallas_call(
        _ffn1_kernel,
        grid=(N // tm,),
        in_specs=[tok, _resident((1, D_MODEL)), _resident((D_MODEL, D_FF)), _resident((D_MODEL, D_FF)),
                  _resident((D_FF, D_MODEL))],
        out_specs=tok,
        out_shape=jax.ShapeDtypeStruct((N, D_MODEL), F32),
        compiler_params=_params(1),
        name="ffn1",
    )(xf, g1, wg, wu, wd)


def _rope(t, cos, sin_lo, sin_hi):
    return (t * cos + pltpu.roll(t, LANES - HEAD_DIM // 2, axis=1) * sin_lo
            + pltpu.roll(t, HEAD_DIM // 2, axis=1) * sin_hi)


def _proj_kernel(x1_ref, gm_ref, win_ref, cos_ref, slo_ref, shi_ref,
                 a_ref, o4_ref, o16_ref,
                 nat_ref, cls4_ref, pa_ref):
    tm = x1_ref.shape[0]
    h2 = _rms(x1_ref[...], gm_ref[...]).astype(BF16)

    def emit(t, hp, slot):
        e = (hp * 3 + slot) % nat_ref.shape[0]
        nat_ref[e] = t
        for r4 in range(4):
            c4 = nat_ref[e, pl.ds(r4, tm // 4, stride=4), :]
            o4_ref[hp, slot, r4] = c4.astype(BF16)
            cls4_ref[e, r4] = c4
            for j in range(4):
                o16_ref[hp, slot, r4 + 4 * j] = cls4_ref[e, r4, pl.ds(j, tm // 16, stride=4), :].astype(BF16)

    cos = cos_ref[...]
    slo = slo_ref[...]
    shi = shi_ref[...]
    base = 2 * D_FOURIER
    for j in range(D_ATTN // 256):
        c0 = j * 256
        qq = _dot(h2, win_ref[:, base + c0: base + c0 + 256])
        kk = _dot(h2, win_ref[:, base + D_ATTN + c0: base + D_ATTN + c0 + 256])
        vv = _dot(h2, win_ref[:, base + 2 * D_ATTN + c0: base + 2 * D_ATTN + c0 + 256])
        for i in range(2):
            sl = slice(i * LANES, (i + 1) * LANES)
            hp = 2 * j + i
            emit(_rope(qq[:, sl], cos, slo, shi) * (HEAD_DIM ** -0.5 * LOG2_E), hp, Q_SLOT)
            emit(_rope(kk[:, sl], cos, slo, shi), hp, K_SLOT)
            emit(vv[:, sl], hp, V_SLOT)

    pa = _dot(h2, win_ref[:, :2 * D_FOURIER])
    tiles_per_part = D_FOURIER // LANES
    for j in range(2 * tiles_per_part):
        pa_ref[j] = pa[:, j * LANES:(j + 1) * LANES]
    for par in range(2):
        for j in range(2 * tiles_per_part):
            lanes = slice((j % tiles_per_part) * LANES, (j % tiles_per_part + 1) * LANES)
            a_ref[par, j // tiles_per_part, :, lanes] = pa_ref[j, pl.ds(par, tm // 2, stride=2), :].astype(BF16)


def _rope_tables(S):
    inv_freq = ROPE_THETA ** (-jnp.arange(0, HEAD_DIM, 2, dtype=F32) / HEAD_DIM)
    freqs = jnp.arange(S, dtype=F32)[:, None] * inv_freq[None, :]
    cos, sin = jnp.cos(freqs), jnp.sin(freqs)
    zero = jnp.zeros_like(sin)
    cos_t = jnp.tile(cos, (1, LANES // (HEAD_DIM // 2)))
    sin_lo = jnp.tile(jnp.concatenate([-sin, zero], axis=1), (1, LANES // HEAD_DIM))
    sin_hi = jnp.tile(jnp.concatenate([zero, sin], axis=1), (1, LANES // HEAD_DIM))
    return cos_t, sin_lo, sin_hi


def _proj(x1, B, S, gm, win, tm):
    N = B * S
    nst = S // tm
    cos_t, sin_lo, sin_hi = _rope_tables(S)
    tok = pl.BlockSpec((tm, D_MODEL), lambda i: (i, 0))
    tab = pl.BlockSpec((tm, LANES), lambda i: (i % nst, 0))

    def cls(d):
        spec = pl.BlockSpec((None, N_HEAD_PAIRS, 3, d, tm // d, LANES),
                            lambda i: (i // nst, 0, 0, 0, i % nst, 0))
        return spec, jax.ShapeDtypeStruct((B, N_HEAD_PAIRS, 3, d, S // d, LANES), BF16)

    c4, c4_shape = cls(4)
    c16, c16_shape = cls(16)
    return pl.pallas_call(
        _proj_kernel,
        grid=(N // tm,),
        in_specs=[tok, _resident((1, D_MODEL)), _resident((D_MODEL, D_PROJ)), tab, tab, tab],
        out_specs=[pl.BlockSpec((2, 2, tm // 2, D_FOURIER), lambda i: (0, 0, i % nst, i // nst)),
                   c4, c16],
        out_shape=[jax.ShapeDtypeStruct((2, 2, S // 2, B * D_FOURIER), BF16), c4_shape, c16_shape],
        scratch_shapes=[pltpu.VMEM((SPLIT_BUFFERS, tm, LANES), F32),
                        pltpu.VMEM((SPLIT_BUFFERS, 4, tm // 4, LANES), F32),
                        pltpu.VMEM((2 * D_FOURIER // LANES, tm, LANES), F32)],
        compiler_params=_params(1),
        name="proj",
    )(x1, gm, win, cos_t, sin_lo, sin_hi)


def _class_order(n):
    return np.arange(n).reshape(n // 4, 4).T.reshape(-1)


def _band_bias(offsets, W, extra=None):
    qi = np.arange(Q_BLOCK)[:, None]
    kj = np.arange(W)[None, :]
    tiles = []
    for t, off in enumerate(offsets):
        ok = np.abs(kj + off - qi) <= RADIUS
        if extra is not None:
            ok &= extra[t](kj)
        tiles.append(np.where(ok, 0.0, MASK_VALUE))
    return jnp.asarray(np.stack(tiles), F32)


K_SLOT, V_SLOT, Q_SLOT = 0, 1, 2


def _attn_kernel(c4_ref, c16_ref, *refs, S, C, NB):
    if C > 1:
        rows_ref, prev_ref, next_ref, *refs = refs
    else:
        rows_ref, prev_ref, next_ref = c4_ref, None, None
    b1_ref, b4_ref, b16_ref, o_ref, acc_ref, m_ref, l_ref = refs
    phase = pl.program_id(2)
    c = pl.program_id(3)
    lane = lax.broadcasted_iota(jnp.int32, (1, LANES), 1)
    first_head = lane < HEAD_DIM
    contract_last = (((1,), (1,)), ((), ()))

    def block_stats(q2, k2, v2, bias):
        v_ones = jnp.concatenate([v2, jnp.ones_like(v2)], axis=1)
        per_head = []
        for head_mask in (first_head, jnp.logical_not(first_head)):
            qh = jnp.where(head_mask, q2, jnp.zeros_like(q2))
            s = lax.dot_general(qh, k2, contract_last, preferred_element_type=F32) + bias
            m = jnp.max(s, axis=-1, keepdims=True)
            p = jnp.exp2(s - m).astype(BF16)
            pvl = _dot(p, v_ones)
            per_head.append((pvl[:, :LANES], jnp.broadcast_to(m, (Q_BLOCK, LANES)), pvl[:, LANES:]))
        return tuple(jnp.where(first_head, a, b) for a, b in zip(*per_head))

    def state(ref, bb, rows):
        if isinstance(rows, (list, tuple)):
            return jnp.concatenate([ref[bb, r, :] for r in rows], axis=0)
        return ref[bb, rows, :]

    def merge(bb, rows, pv, m, l, mode):
        if mode == "first":
            acc_ref[bb, rows, :] = pv
            m_ref[bb, rows, :] = m
            l_ref[bb, rows, :] = l
            return None
        m_old = state(m_ref, bb, rows)
        m_new = jnp.maximum(m_old, m)
        a_old = jnp.exp2(m_old - m_new)
        a_blk = jnp.exp2(m - m_new)
        acc = a_old * state(acc_ref, bb, rows) + a_blk * pv
        den = a_old * state(l_ref, bb, rows) + a_blk * l
        if mode == "last":
            return acc / den
        acc_ref[bb, rows, :] = acc
        m_ref[bb, rows, :] = m_new
        l_ref[bb, rows, :] = den
        return None

    def strided_phase(d, cls_ref, bias_ref, mode):
        L = S // d
        nblk = L // Q_BLOCK
        W = min(KEY_WINDOW, L)
        cpc = d // C
        for bb in range(NB):
            for rl in range(cpc):
                for i in range(nblk):
                    a = i * Q_BLOCK
                    ws = min(max(a - RADIUS, 0), L - W)
                    typ = 0 if i == 0 else (2 if i == nblk - 1 else 1)
                    k2 = cls_ref[bb, K_SLOT, rl, ws:ws + W, :]
                    v2 = cls_ref[bb, V_SLOT, rl, ws:ws + W, :]
                    bias = bias_ref[typ]
                    if W < KEY_WINDOW:
                        pad = jnp.zeros((KEY_WINDOW - W, LANES), BF16)
                        k2 = jnp.concatenate([k2, pad], axis=0)
                        v2 = jnp.concatenate([v2, pad], axis=0)
                        bias = jnp.concatenate([bias, jnp.full((Q_BLOCK, KEY_WINDOW - W), MASK_VALUE, F32)],
                                               axis=1)
                    pv, m, l = block_stats(cls_ref[bb, Q_SLOT, rl, a:a + Q_BLOCK, :], k2, v2, bias)
                    cls = c * cpc + rl
                    if d == 4:
                        rows = pl.ds(pl.multiple_of(cls * L + a, Q_BLOCK), Q_BLOCK)
                    else:
                        rows = pl.ds((cls % 4) * (S // 4) + (d // 4) * a + cls // 4, Q_BLOCK, stride=4)
                    merge(bb, rows, pv, m, l, mode)

    def contiguous_phase():
        Sc = S // C
        nblk = Sc // Q_BLOCK
        Lc = Sc // 4
        halo = RADIUS // 4

        def gather(bb, slot, lo, hi):
            outside = jnp.zeros((halo, LANES), BF16)
            parts = []
            for r4 in range(4):
                if lo < 0:
                    parts.append(outside if prev_ref is None else prev_ref[bb, slot, r4])
                parts.append(rows_ref[bb, slot, r4, max(lo, 0) // 4:min(hi, Sc) // 4, :])
                if hi > Sc:
                    parts.append(outside if next_ref is None else next_ref[bb, slot, r4])
            return jnp.concatenate(parts, axis=0)

        assert Lc % halo == 0
        for bb in range(NB):
            for i in range(nblk):
                a = i * Q_BLOCK
                typ = 0
                if i == 0:
                    typ = jnp.where(c == 0, 1, typ)
                if i == nblk - 1:
                    typ = jnp.where(c == C - 1, 2, typ)
                pv, m, l = block_stats(gather(bb, Q_SLOT, a, a + Q_BLOCK),
                                       gather(bb, K_SLOT, a - RADIUS, a + Q_BLOCK + RADIUS),
                                       gather(bb, V_SLOT, a - RADIUS, a + Q_BLOCK + RADIUS), b1_ref[typ])
                pos = c * Sc + a
                part = Q_BLOCK // 4
                rows = [pl.ds(pl.multiple_of(r4 * (S // 4) + pos // 4, part), part) for r4 in range(4)]
                out = merge(bb, rows, pv, m, l, "last")
                for r4 in range(4):
                    o_ref[bb, pl.ds(pos + r4, part, stride=4), :] = out[r4 * part:(r4 + 1) * part, :]

    @pl.when(phase == 0)
    def _():
        strided_phase(4, c4_ref, b4_ref, "first")

    @pl.when(phase == 1)
    def _():
        strided_phase(16, c16_ref, b16_ref, "mid")

    @pl.when(phase == 2)
    def _():
        contiguous_phase()


def _attention(c4_kvq, c16_kvq, B, S, C, NB):
    Sc = S // C
    assert Sc % Q_BLOCK == 0 and 4 % C == 0 and B % NB == 0
    halo = RADIUS // 4
    Lc = Sc // 4
    halo_blocks = S // 4 // halo

    def chunk_of(own_phase, p, c):
        return jnp.where(p < own_phase, 0, jnp.where(p > own_phase, C - 1, c))

    def cls_spec(d, own_phase):
        L = S // d
        assert L % Q_BLOCK == 0 and (L == Q_BLOCK or L >= KEY_WINDOW)
        return pl.BlockSpec((NB, None, 3, d // C, L, LANES),
                            lambda b, h, p, c: (b, h, 0, chunk_of(own_phase, p, c), 0, 0))

    rows = pl.BlockSpec((NB, None, 3, 4, Lc, LANES), lambda b, h, p, c: (b, h, 0, 0, chunk_of(2, p, c), 0))
    prev = pl.BlockSpec((NB, None, 2, 4, halo, LANES),
                        lambda b, h, p, c: (b, h, 0, 0, jnp.maximum(chunk_of(2, p, c) * (Lc // halo) - 1, 0), 0))
    nxt = pl.BlockSpec((NB, None, 2, 4, halo, LANES),
                       lambda b, h, p, c: (b, h, 0, 0, jnp.minimum((chunk_of(2, p, c) + 1) * (Lc // halo),
                                                                    halo_blocks - 1), 0))
    c4, c16 = cls_spec(4, 0), cls_spec(16, 1)
    W4, W16 = min(KEY_WINDOW, S // 4), min(KEY_WINDOW, S // 16)
    clamped = (0, -RADIUS, -2 * RADIUS)
    b1 = _band_bias((-RADIUS,) * 3, KEY_WINDOW,
                    extra=(lambda kj: kj >= 0, lambda kj: kj >= RADIUS, lambda kj: kj < KEY_WINDOW - RADIUS))
    b1 = b1[:, _class_order(Q_BLOCK), :][:, :, _class_order(KEY_WINDOW)]
    b4, b16 = _band_bias(clamped, W4), _band_bias(clamped, W16)
    return pl.pallas_call(
        functools.partial(_attn_kernel, S=S, C=C, NB=NB),
        grid=(B // NB, N_HEAD_PAIRS, 3, C),
        in_specs=[c4, c16] + ([rows, prev, nxt] if C > 1 else [])
        + [_resident(b1.shape), _resident(b4.shape), _resident(b16.shape)],
        out_specs=pl.BlockSpec((NB, None, S, LANES), lambda b, h, p, c: (b, h, 0, 0)),
        out_shape=jax.ShapeDtypeStruct((B, N_HEAD_PAIRS, S, LANES), F32),
        scratch_shapes=[pltpu.VMEM((NB, S, LANES), F32)] * 3,
        compiler_params=_params(4),
        name="attn",
    )(c4_kvq, c16_kvq, *([c4_kvq] * 3 if C > 1 else []), b1, b4, b16)


def _dft_kernel(bc_ref, nbs_ref, r1_ref, r2_ref, a_ref, o_ref):
    t = pl.program_id(2)

    @pl.when(t == 0)
    def _():
        o_ref[...] = jnp.zeros_like(o_ref)

    for par in range(2):
        bc, nbs = bc_ref[par], nbs_ref[par]
        r1, r2 = r1_ref[par], r2_ref[par]
        t_cos = (bc * r1 + nbs * r2).astype(BF16)
        t_nsin = (nbs * r1 - bc * r2).astype(BF16)
        o_ref[par] += _dot(t_cos, a_ref[par, 0]) + _dot(t_nsin, a_ref[par, 1])

    @pl.when(t == pl.num_programs(2) - 1)
    def _():
        f0, f1 = o_ref[0], o_ref[1]
        o_ref[0] = f0 + f1
        o_ref[1] = f0 - f1


def _seq_dft(a, S, ts, tk, tn):
    ncols = a.shape[-1]
    H = S // 2
    t = 2 * jnp.arange(H, dtype=jnp.int32)[None, None, :] + jnp.arange(2, dtype=jnp.int32)[:, None, None]
    w = 2.0 * math.pi / S
    ang_r = ((jnp.arange(0, H, ts, dtype=jnp.int32)[None, :, None] * t) % S).astype(F32) * w
    lo = 32
    ang_hi = ((jnp.arange(0, ts, lo, dtype=jnp.int32)[None, :, None] * t) % S).astype(F32) * w
    ang_lo = ((jnp.arange(lo, dtype=jnp.int32)[None, :, None] * t) % S).astype(F32) * w
    c_hi, s_hi = jnp.cos(ang_hi)[:, :, None, :], jnp.sin(ang_hi)[:, :, None, :]
    c_lo, s_lo = jnp.cos(ang_lo)[:, None, :, :], jnp.sin(ang_lo)[:, None, :, :]
    bc = (c_hi * c_lo - s_hi * s_lo).reshape(2, ts, H)
    nbs = -(s_hi * c_lo + c_hi * s_lo).reshape(2, ts, H)
    scale = S ** -0.5
    r1 = (jnp.cos(ang_r) * scale).reshape(2, H // ts, 1, H)
    r2 = (jnp.sin(ang_r) * scale).reshape(2, H // ts, 1, H)
    btab = pl.BlockSpec((2, ts, tk), lambda s, n, t: (0, 0, t))
    rtab = pl.BlockSpec((2, None, 1, tk), lambda s, n, t: (0, s, 0, t))
    out = pl.pallas_call(
        _dft_kernel,
        grid=(H // ts, ncols // tn, H // tk),
        in_specs=[btab, btab, rtab, rtab, pl.BlockSpec((2, 2, tk, tn), lambda s, n, t: (0, 0, t, n))],
        out_specs=pl.BlockSpec((2, ts, tn), lambda s, n, t: (0, s, n)),
        out_shape=jax.ShapeDtypeStruct((2, H, ncols), F32),
        compiler_params=_params(3),
        name="seq_dft",
    )(bc, nbs, r1, r2, a)
    return out.reshape(S, ncols)


def _out_ffn_kernel(x1_ref, of_ref, oa_ref, gf_ref, ga_ref, wout_ref, g2_ref, wg_ref, wu_ref, wd_ref,
                    gfin_ref, y_ref):
    oa = jnp.concatenate([oa_ref[hp] for hp in range(N_HEAD_PAIRS)], axis=1)
    na = _rms(oa, ga_ref[...]).astype(BF16)
    nf = _rms(of_ref[...], gf_ref[...]).astype(BF16)
    x2 = x1_ref[...] + _dot(nf, wout_ref[:D_FOURIER, :]) + _dot(na, wout_ref[D_FOURIER:, :])
    h = _rms(x2, g2_ref[...]).astype(BF16)
    x3 = x2 + 0.5 * _swiglu(h, wg_ref, wu_ref, wd_ref)
    y_ref[...] = _rms(x3, gfin_ref[...])


def _out_ffn(x1, of, oa, B, S, gf, ga, wout, g2, wg, wu, wd, gfin, tm):
    N = B * S
    nst = S // tm
    tok = pl.BlockSpec((tm, D_MODEL), lambda i: (i, 0))
    att = pl.BlockSpec((None, N_HEAD_PAIRS, tm, LANES), lambda i: (i // nst, 0, i % nst, 0))
    return pl.pallas_call(
        _out_ffn_kernel,
        grid=(N // tm,),
        in_specs=[tok, pl.BlockSpec((tm, D_FOURIER), lambda i: (i % nst, i // nst)), att,
                  _resident((1, D_FOURIER)), _resident((1, D_ATTN)), _resident((D_MODEL, D_MODEL)),
                  _resident((1, D_MODEL)), _resident((D_MODEL, D_FF)), _resident((D_MODEL, D_FF)),
                  _resident((D_FF, D_MODEL)), _resident((1, D_MODEL))],
        out_specs=tok,
        out_shape=jax.ShapeDtypeStruct((N, D_MODEL), F32),
        compiler_params=_params(1),
        name="out_ffn",
    )(x1, of, oa, gf, ga, wout, g2, wg, wu, wd, gfin)


def _row(g):
    return g.reshape(1, -1).astype(F32)


def _trunk(x, p, win):
    B, S, _ = x.shape
    x1 = _ffn1(x.reshape(B * S, D_MODEL), _row(p["ffn1_norm"]), p["ffn1_w_gate"], p["ffn1_w_up"],
               p["ffn1_w_down"], tm=FFN_TOKENS)
    a, c4_kvq, c16_kvq = _proj(x1, B, S, _row(p["mix_norm"]), win, tm=PROJ_TOKENS)
    blocks_per_step = 64
    C = max(1, S // (blocks_per_step * Q_BLOCK))
    NB = max(1, blocks_per_step * Q_BLOCK // S)
    oa = _attention(c4_kvq, c16_kvq, B, S, C, NB)
    of = _seq_dft(a, S, ts=512, tk=512, tn=min(2048, a.shape[-1]))
    y = _out_ffn(x1, of, oa, B, S, _row(p["fourier_out_norm"]), _row(p["attn_out_norm"]), p["w_out"],
                 _row(p["ffn2_norm"]), p["ffn2_w_gate"], p["ffn2_w_up"], p["ffn2_w_down"],
                 _row(p["final_norm"]), tm=FFN_TOKENS)
    return y.reshape(B, S, D_MODEL)


def kernel(x_prompt, x_sample, ffn1_norm, ffn1_w_gate, ffn1_w_up, ffn1_w_down, mix_norm, w_in, fourier_w,
           fourier_out_norm, attn_out_norm, w_out, ffn2_norm, ffn2_w_gate, ffn2_w_up, ffn2_w_down, final_norm):
    assert ffn1_w_gate.shape[0] == 1, "single-layer trunk"
    folded = _fold_fourier_weights(w_in[0, :, :D_FOURIER], fourier_w[0])
    win = jnp.concatenate([folded, w_in[0, :, D_FOURIER:].astype(BF16)], axis=1)
    p = dict(
        ffn1_norm=ffn1_norm[0], ffn1_w_gate=ffn1_w_gate[0].astype(BF16), ffn1_w_up=ffn1_w_up[0].astype(BF16),
        ffn1_w_down=ffn1_w_down[0].astype(BF16), mix_norm=mix_norm[0],
        fourier_out_norm=fourier_out_norm[0], attn_out_norm=attn_out_norm[0], w_out=w_out[0].astype(BF16),
        ffn2_norm=ffn2_norm[0], ffn2_w_gate=ffn2_w_gate[0].astype(BF16), ffn2_w_up=ffn2_w_up[0].astype(BF16),
        ffn2_w_down=ffn2_w_down[0].astype(BF16), final_norm=final_norm)
    return (_trunk(x_prompt, p, win), _trunk(x_sample, p, win))
```

```python
import functools
import math

import jax
import jax.numpy as jnp
import numpy as np
from jax import lax
from jax.experimental import pallas as pl
from jax.experimental.pallas import tpu as pltpu

F32 = jnp.float32
BF16 = jnp.bfloat16

D_MODEL = 1024
D_FF = 2816
D_FOURIER = 256
N_FOURIER_GROUPS = 4
FOURIER_GROUP_DIM = 64
D_ATTN = 768
HEAD_DIM = 64
LANES = 128
N_HEAD_PAIRS = D_ATTN // LANES
D_PROJ = 2 * D_FOURIER + 3 * D_ATTN
RADIUS = 64
ROPE_THETA = 10000.0
RMS_EPS = 1e-6
MASK_VALUE = -1e30

MXU_COLS = 256
FF_CHUNK = MXU_COLS
FFN_TOKENS = 1024
PROJ_TOKENS = 1024
SPLIT_BUFFERS = 6
DFT_ROWS, DFT_DEPTH, DFT_COLS = 512, 1024, 1024
TABLE_SPLIT = 32
Q_BLOCK = 128
ATTN_BLOCKS_PER_STEP = 64
KEY_WINDOW = Q_BLOCK + 2 * RADIUS
LOG2_E = math.log2(math.e)
VMEM_LIMIT = 56 * 1024 * 1024


def _params(n_grid_dims):
    return pltpu.CompilerParams(dimension_semantics=("arbitrary",) * n_grid_dims,
                                vmem_limit_bytes=VMEM_LIMIT)


def _resident(shape):
    nd = len(shape)
    return pl.BlockSpec(shape, lambda *_: (0,) * nd, pipeline_mode=pl.Buffered(1))


def _rms(x, g):
    inv = lax.rsqrt(jnp.mean(x * x, axis=-1, keepdims=True) + RMS_EPS)
    return x * inv * g


def _dot(a, b):
    return jnp.dot(a, b, preferred_element_type=F32)


def _swiglu(h, wg_ref, wu_ref, wd_ref):
    acc = None
    for c in range(D_FF // FF_CHUNK):
        sl = slice(c * FF_CHUNK, (c + 1) * FF_CHUNK)
        g = _dot(h, wg_ref[:, sl])
        u = _dot(h, wu_ref[:, sl])
        a = (g * jax.nn.sigmoid(g) * u).astype(BF16)
        part = _dot(a, wd_ref[sl, :])
        acc = part if acc is None else acc + part
    return acc


def _fold_kernel(winf_ref, cbd_ref, sbd_ref, wbd_ref, o_ref):
    hi = lax.Precision.HIGHEST
    scale = FOURIER_GROUP_DIM ** -0.5
    gc = jnp.dot(cbd_ref[...], wbd_ref[...], precision=hi, preferred_element_type=F32) * scale
    gs = jnp.dot(sbd_ref[...], wbd_ref[...], precision=hi, preferred_element_type=F32) * scale
    winf = winf_ref[...]
    o_ref[:, :D_FOURIER] = jnp.dot(winf, gc, precision=hi, preferred_element_type=F32).astype(BF16)
    o_ref[:, D_FOURIER:] = jnp.dot(winf, gs, precision=hi, preferred_element_type=F32).astype(BF16)


def _fold_fourier_weights(w_in_f, fourier_w):
    c = np.arange(FOURIER_GROUP_DIM)
    ang = 2.0 * np.pi * ((c[:, None] * c[None, :]) % FOURIER_GROUP_DIM) / FOURIER_GROUP_DIM
    eye = np.eye(N_FOURIER_GROUPS)
    cbd = jnp.asarray(np.kron(eye, np.cos(ang)), F32)
    sbd = jnp.asarray(np.kron(eye, np.sin(ang)), F32)
    wbd = jnp.zeros((D_FOURIER, D_FOURIER), F32)
    for g in range(N_FOURIER_GROUPS):
        s = g * FOURIER_GROUP_DIM
        wbd = lax.dynamic_update_slice(wbd, fourier_w[g].astype(F32), (s, s))
    return pl.pallas_call(
        _fold_kernel,
        out_shape=jax.ShapeDtypeStruct((D_MODEL, 2 * D_FOURIER), BF16),
        name="fold_fourier",
    )(w_in_f, cbd, sbd, wbd)


def _ffn1_kernel(x_ref, g1_ref, wg_ref, wu_ref, wd_ref, x1_ref):
    x = x_ref[...]
    h = _rms(x, g1_ref[...]).astype(BF16)
    x1_ref[...] = x + 0.5 * _swiglu(h, wg_ref, wu_ref, wd_ref)


def _ffn1(xf, g1, wg, wu, wd, tm):
    N = xf.shape[0]
    tok = pl.BlockSpec((tm, D_MODEL), lambda i: (i, 0))
    return pl.pallas_call(
        _ffn1_kernel,
        grid=(N // tm,),
        in_specs=[tok, _resident((1, D_MODEL)), _resident((D_MODEL, D_FF)), _resident((D_MODEL, D_FF)),
                  _resident((D_FF, D_MODEL))],
        out_specs=tok,
        out_shape=jax.ShapeDtypeStruct((N, D_MODEL), F32),
        compiler_params=_params(1),
        name="ffn1",
    )(xf, g1, wg, wu, wd)


def _rope(t, cos, sin_lo, sin_hi):
    return (t * cos + pltpu.roll(t, LANES - HEAD_DIM // 2, axis=1) * sin_lo
            + pltpu.roll(t, HEAD_DIM // 2, axis=1) * sin_hi)


def _proj_kernel(x1_ref, gm_ref, win_ref, cos_ref, slo_ref, shi_ref,
                 a_ref, o4_ref, o16_ref,
                 nat_ref, cls4_ref, pa_ref):
    tm = x1_ref.shape[0]
    h2 = _rms(x1_ref[...], gm_ref[...]).astype(BF16)

    def emit(t, hp, slot):
        e = (hp * 3 + slot) % nat_ref.shape[0]
        nat_ref[e] = t
        for r4 in range(4):
            c4 = nat_ref[e, pl.ds(r4, tm // 4, stride=4), :]
            o4_ref[hp, slot, r4] = c4.astype(BF16)
            cls4_ref[e, r4] = c4
            for j in range(4):
                o16_ref[hp, slot, r4 + 4 * j] = cls4_ref[e, r4, pl.ds(j, tm // 16, stride=4), :].astype(BF16)

    cos = cos_ref[...]
    slo = slo_ref[...]
    shi = shi_ref[...]
    base = 2 * D_FOURIER
    for j in range(D_ATTN // MXU_COLS):
        c0 = j * MXU_COLS
        qq = _dot(h2, win_ref[:, base + c0: base + c0 + MXU_COLS])
        kk = _dot(h2, win_ref[:, base + D_ATTN + c0: base + D_ATTN + c0 + MXU_COLS])
        vv = _dot(h2, win_ref[:, base + 2 * D_ATTN + c0: base + 2 * D_ATTN + c0 + MXU_COLS])
        for i in range(MXU_COLS // LANES):
            sl = slice(i * LANES, (i + 1) * LANES)
            hp = (MXU_COLS // LANES) * j + i
            emit(_rope(qq[:, sl], cos, slo, shi) * (HEAD_DIM ** -0.5 * LOG2_E), hp, Q_SLOT)
            emit(_rope(kk[:, sl], cos, slo, shi), hp, K_SLOT)
            emit(vv[:, sl], hp, V_SLOT)

    pa = _dot(h2, win_ref[:, :2 * D_FOURIER])
    tiles_per_part = D_FOURIER // LANES
    for j in range(2 * tiles_per_part):
        pa_ref[j] = pa[:, j * LANES:(j + 1) * LANES]
    for par in range(2):
        for j in range(2 * tiles_per_part):
            lanes = slice((j % tiles_per_part) * LANES, (j % tiles_per_part + 1) * LANES)
            a_ref[par, j // tiles_per_part, :, lanes] = pa_ref[j, pl.ds(par, tm // 2, stride=2), :].astype(BF16)


def _rope_tables(S):
    inv_freq = ROPE_THETA ** (-jnp.arange(0, HEAD_DIM, 2, dtype=F32) / HEAD_DIM)
    freqs = jnp.arange(S, dtype=F32)[:, None] * inv_freq[None, :]
    cos, sin = jnp.cos(freqs), jnp.sin(freqs)
    zero = jnp.zeros_like(sin)
    cos_t = jnp.tile(cos, (1, LANES // (HEAD_DIM // 2)))
    sin_lo = jnp.tile(jnp.concatenate([-sin, zero], axis=1), (1, LANES // HEAD_DIM))
    sin_hi = jnp.tile(jnp.concatenate([zero, sin], axis=1), (1, LANES // HEAD_DIM))
    return cos_t, sin_lo, sin_hi


def _proj(x1, B, S, gm, win, tm):
    N = B * S
    nst = S // tm
    cos_t, sin_lo, sin_hi = _rope_tables(S)
    tok = pl.BlockSpec((tm, D_MODEL), lambda i: (i, 0))
    tab = pl.BlockSpec((tm, LANES), lambda i: (i % nst, 0))

    def cls(d):
        spec = pl.BlockSpec((None, N_HEAD_PAIRS, 3, d, tm // d, LANES),
                            lambda i: (i // nst, 0, 0, 0, i % nst, 0))
        return spec, jax.ShapeDtypeStruct((B, N_HEAD_PAIRS, 3, d, S // d, LANES), BF16)

    c4, c4_shape = cls(4)
    c16, c16_shape = cls(16)
    return pl.pallas_call(
        _proj_kernel,
        grid=(N // tm,),
        in_specs=[tok, _resident((1, D_MODEL)), _resident((D_MODEL, D_PROJ)), tab, tab, tab],
        out_specs=[pl.BlockSpec((2, 2, tm // 2, D_FOURIER), lambda i: (0, 0, i % nst, i // nst)),
                   c4, c16],
        out_shape=[jax.ShapeDtypeStruct((2, 2, S // 2, B * D_FOURIER), BF16), c4_shape, c16_shape],
        scratch_shapes=[pltpu.VMEM((SPLIT_BUFFERS, tm, LANES), F32),
                        pltpu.VMEM((SPLIT_BUFFERS, 4, tm // 4, LANES), F32),
                        pltpu.VMEM((2 * D_FOURIER // LANES, tm, LANES), F32)],
        compiler_params=_params(1),
        name="proj",
    )(x1, gm, win, cos_t, sin_lo, sin_hi)


def _class_order(n):
    return np.arange(n).reshape(n // 4, 4).T.reshape(-1)


def _band_bias(offsets, W, extra=None):
    qi = np.arange(Q_BLOCK)[:, None]
    kj = np.arange(W)[None, :]
    tiles = []
    for t, off in enumerate(offsets):
        ok = np.abs(kj + off - qi) <= RADIUS
        if extra is not None:
            ok &= extra[t](kj)
        tiles.append(np.where(ok, 0.0, MASK_VALUE))
    return jnp.asarray(np.stack(tiles), F32)


K_SLOT, V_SLOT, Q_SLOT = 0, 1, 2


def _attn_kernel(c4_ref, c16_ref, *refs, S, C, NB):
    if C > 1:
        rows_ref, prev_ref, next_ref, *refs = refs
    else:
        rows_ref, prev_ref, next_ref = c4_ref, None, None
    b1_ref, b4_ref, b16_ref, o_ref, acc_ref, m_ref, l_ref = refs
    phase = pl.program_id(2)
    c = pl.program_id(3)
    lane = lax.broadcasted_iota(jnp.int32, (1, LANES), 1)
    first_head = lane < HEAD_DIM
    contract_last = (((1,), (1,)), ((), ()))

    def block_stats(q2, k2, v2, bias):
        v_ones = jnp.concatenate([v2, jnp.ones_like(v2)], axis=1)
        per_head = []
        for head_mask in (first_head, jnp.logical_not(first_head)):
            qh = jnp.where(head_mask, q2, jnp.zeros_like(q2))
            s = lax.dot_general(qh, k2, contract_last, preferred_element_type=F32) + bias
            m = jnp.max(s, axis=-1, keepdims=True)
            p = jnp.exp2(s - m).astype(BF16)
            pvl = _dot(p, v_ones)
            per_head.append((pvl[:, :LANES], jnp.broadcast_to(m, (Q_BLOCK, LANES)), pvl[:, LANES:]))
        return tuple(jnp.where(first_head, a, b) for a, b in zip(*per_head))

    def state(ref, bb, rows):
        if isinstance(rows, (list, tuple)):
            return jnp.concatenate([ref[bb, r, :] for r in rows], axis=0)
        return ref[bb, rows, :]

    def merge(bb, rows, pv, m, l, mode):
        if mode == "first":
            acc_ref[bb, rows, :] = pv
            m_ref[bb, rows, :] = m
            l_ref[bb, rows, :] = l
            return None
        m_old = state(m_ref, bb, rows)
        m_new = jnp.maximum(m_old, m)
        a_old = jnp.exp2(m_old - m_new)
        a_blk = jnp.exp2(m - m_new)
        acc = a_old * state(acc_ref, bb, rows) + a_blk * pv
        den = a_old * state(l_ref, bb, rows) + a_blk * l
        if mode == "last":
            return acc / den
        acc_ref[bb, rows, :] = acc
        m_ref[bb, rows, :] = m_new
        l_ref[bb, rows, :] = den
        return None

    def strided_phase(d, cls_ref, bias_ref, mode):
        L = S // d
        nblk = L // Q_BLOCK
        W = min(KEY_WINDOW, L)
        cpc = d // C
        for bb in range(NB):
            for rl in range(cpc):
                for i in range(nblk):
                    a = i * Q_BLOCK
                    ws = min(max(a - RADIUS, 0), L - W)
                    typ = 0 if i == 0 else (2 if i == nblk - 1 else 1)
                    k2 = cls_ref[bb, K_SLOT, rl, ws:ws + W, :]
                    v2 = cls_ref[bb, V_SLOT, rl, ws:ws + W, :]
                    bias = bias_ref[typ]
                    if W < KEY_WINDOW:
                        pad = jnp.zeros((KEY_WINDOW - W, LANES), BF16)
                        k2 = jnp.concatenate([k2, pad], axis=0)
                        v2 = jnp.concatenate([v2, pad], axis=0)
                        bias = jnp.concatenate([bias, jnp.full((Q_BLOCK, KEY_WINDOW - W), MASK_VALUE, F32)],
                                               axis=1)
                    pv, m, l = block_stats(cls_ref[bb, Q_SLOT, rl, a:a + Q_BLOCK, :], k2, v2, bias)
                    cls = c * cpc + rl
                    if d == 4:
                        rows = pl.ds(pl.multiple_of(cls * L + a, Q_BLOCK), Q_BLOCK)
                    else:
                        rows = pl.ds((cls % 4) * (S // 4) + (d // 4) * a + cls // 4, Q_BLOCK, stride=4)
                    merge(bb, rows, pv, m, l, mode)

    def contiguous_phase():
        Sc = S // C
        nblk = Sc // Q_BLOCK
        Lc = Sc // 4
        halo = RADIUS // 4

        def gather(bb, slot, lo, hi):
            outside = jnp.zeros((halo, LANES), BF16)
            parts = []
            for r4 in range(4):
                if lo < 0:
                    parts.append(outside if prev_ref is None else prev_ref[bb, slot, r4])
                parts.append(rows_ref[bb, slot, r4, max(lo, 0) // 4:min(hi, Sc) // 4, :])
                if hi > Sc:
                    parts.append(outside if next_ref is None else next_ref[bb, slot, r4])
            return jnp.concatenate(parts, axis=0)

        assert Lc % halo == 0
        for bb in range(NB):
            for i in range(nblk):
                a = i * Q_BLOCK
                typ = 0
                if i == 0:
                    typ = jnp.where(c == 0, 1, typ)
                if i == nblk - 1:
                    typ = jnp.where(c == C - 1, 2, typ)
                pv, m, l = block_stats(gather(bb, Q_SLOT, a, a + Q_BLOCK),
                                       gather(bb, K_SLOT, a - RADIUS, a + Q_BLOCK + RADIUS),
                                       gather(bb, V_SLOT, a - RADIUS, a + Q_BLOCK + RADIUS), b1_ref[typ])
                pos = c * Sc + a
                part = Q_BLOCK // 4
                rows = [pl.ds(pl.multiple_of(r4 * (S // 4) + pos // 4, part), part) for r4 in range(4)]
                out = merge(bb, rows, pv, m, l, "last")
                for r4 in range(4):
                    o_ref[bb, pl.ds(pos + r4, part, stride=4), :] = out[r4 * part:(r4 + 1) * part, :]

    @pl.when(phase == 0)
    def _():
        strided_phase(4, c4_ref, b4_ref, "first")

    @pl.when(phase == 1)
    def _():
        strided_phase(16, c16_ref, b16_ref, "mid")

    @pl.when(phase == 2)
    def _():
        contiguous_phase()


def _attention(c4_kvq, c16_kvq, B, S, C, NB):
    Sc = S // C
    assert Sc % Q_BLOCK == 0 and 4 % C == 0 and B % NB == 0
    halo = RADIUS // 4
    Lc = Sc // 4
    halo_blocks = S // 4 // halo

    def chunk_of(own_phase, p, c):
        return jnp.where(p < own_phase, 0, jnp.where(p > own_phase, C - 1, c))

    def cls_spec(d, own_phase):
        L = S // d
        assert L % Q_BLOCK == 0 and (L == Q_BLOCK or L >= KEY_WINDOW)
        return pl.BlockSpec((NB, None, 3, d // C, L, LANES),
                            lambda b, h, p, c: (b, h, 0, chunk_of(own_phase, p, c), 0, 0))

    rows = pl.BlockSpec((NB, None, 3, 4, Lc, LANES), lambda b, h, p, c: (b, h, 0, 0, chunk_of(2, p, c), 0))
    prev = pl.BlockSpec((NB, None, 2, 4, halo, LANES),
                        lambda b, h, p, c: (b, h, 0, 0, jnp.maximum(chunk_of(2, p, c) * (Lc // halo) - 1, 0), 0))
    nxt = pl.BlockSpec((NB, None, 2, 4, halo, LANES),
                       lambda b, h, p, c: (b, h, 0, 0, jnp.minimum((chunk_of(2, p, c) + 1) * (Lc // halo),
                                                                    halo_blocks - 1), 0))
    c4, c16 = cls_spec(4, 0), cls_spec(16, 1)
    W4, W16 = min(KEY_WINDOW, S // 4), min(KEY_WINDOW, S // 16)
    clamped = (0, -RADIUS, -2 * RADIUS)
    b1 = _band_bias((-RADIUS,) * 3, KEY_WINDOW,
                    extra=(lambda kj: kj >= 0, lambda kj: kj >= RADIUS, lambda kj: kj < KEY_WINDOW - RADIUS))
    b1 = b1[:, _class_order(Q_BLOCK), :][:, :, _class_order(KEY_WINDOW)]
    b4, b16 = _band_bias(clamped, W4), _band_bias(clamped, W16)
    return pl.pallas_call(
        functools.partial(_attn_kernel, S=S, C=C, NB=NB),
        grid=(B // NB, N_HEAD_PAIRS, 3, C),
        in_specs=[c4, c16] + ([rows, prev, nxt] if C > 1 else [])
        + [_resident(b1.shape), _resident(b4.shape), _resident(b16.shape)],
        out_specs=pl.BlockSpec((NB, None, S, LANES), lambda b, h, p, c: (b, h, 0, 0)),
        out_shape=jax.ShapeDtypeStruct((B, N_HEAD_PAIRS, S, LANES), F32),
        scratch_shapes=[pltpu.VMEM((NB, S, LANES), F32)] * 3,
        compiler_params=_params(4),
        name="attn",
    )(c4_kvq, c16_kvq, *([c4_kvq] * 3 if C > 1 else []), b1, b4, b16)


def _dft_kernel(bc_ref, nbs_ref, r1_ref, r2_ref, a_ref, o_ref):
    t = pl.program_id(2)

    @pl.when(t == 0)
    def _():
        o_ref[...] = jnp.zeros_like(o_ref)

    for par in range(2):
        bc, nbs = bc_ref[par], nbs_ref[par]
        r1, r2 = r1_ref[par], r2_ref[par]
        t_cos = (bc * r1 + nbs * r2).astype(BF16)
        t_nsin = (nbs * r1 - bc * r2).astype(BF16)
        o_ref[par] += _dot(t_cos, a_ref[par, 0]) + _dot(t_nsin, a_ref[par, 1])

    @pl.when(t == pl.num_programs(2) - 1)
    def _():
        f0, f1 = o_ref[0], o_ref[1]
        o_ref[0] = f0 + f1
        o_ref[1] = f0 - f1


def _seq_dft(a, S, ts, tk, tn):
    ncols = a.shape[-1]
    H = S // 2
    t = 2 * jnp.arange(H, dtype=jnp.int32)[None, None, :] + jnp.arange(2, dtype=jnp.int32)[:, None, None]
    w = 2.0 * math.pi / S
    ang_r = ((jnp.arange(0, H, ts, dtype=jnp.int32)[None, :, None] * t) % S).astype(F32) * w
    lo = TABLE_SPLIT
    ang_hi = ((jnp.arange(0, ts, lo, dtype=jnp.int32)[None, :, None] * t) % S).astype(F32) * w
    ang_lo = ((jnp.arange(lo, dtype=jnp.int32)[None, :, None] * t) % S).astype(F32) * w
    c_hi, s_hi = jnp.cos(ang_hi)[:, :, None, :], jnp.sin(ang_hi)[:, :, None, :]
    c_lo, s_lo = jnp.cos(ang_lo)[:, None, :, :], jnp.sin(ang_lo)[:, None, :, :]
    bc = (c_hi * c_lo - s_hi * s_lo).reshape(2, ts, H)
    nbs = -(s_hi * c_lo + c_hi * s_lo).reshape(2, ts, H)
    scale = S ** -0.5
    r1 = (jnp.cos(ang_r) * scale).reshape(2, H // ts, 1, H)
    r2 = (jnp.sin(ang_r) * scale).reshape(2, H // ts, 1, H)
    btab = pl.BlockSpec((2, ts, tk), lambda s, n, t: (0, 0, t))
    rtab = pl.BlockSpec((2, None, 1, tk), lambda s, n, t: (0, s, 0, t))
    out = pl.pallas_call(
        _dft_kernel,
        grid=(H // ts, ncols // tn, H // tk),
        in_specs=[btab, btab, rtab, rtab, pl.BlockSpec((2, 2, tk, tn), lambda s, n, t: (0, 0, t, n))],
        out_specs=pl.BlockSpec((2, ts, tn), lambda s, n, t: (0, s, n)),
        out_shape=jax.ShapeDtypeStruct((2, H, ncols), F32),
        compiler_params=_params(3),
        name="seq_dft",
    )(bc, nbs, r1, r2, a)
    return out.reshape(S, ncols)


def _out_ffn_kernel(x1_ref, of_ref, oa_ref, gf_ref, ga_ref, wout_ref, g2_ref, wg_ref, wu_ref, wd_ref,
                    gfin_ref, y_ref):
    oa = jnp.concatenate([oa_ref[hp] for hp in range(N_HEAD_PAIRS)], axis=1)
    na = _rms(oa, ga_ref[...]).astype(BF16)
    nf = _rms(of_ref[...], gf_ref[...]).astype(BF16)
    x2 = x1_ref[...] + _dot(nf, wout_ref[:D_FOURIER, :]) + _dot(na, wout_ref[D_FOURIER:, :])
    h = _rms(x2, g2_ref[...]).astype(BF16)
    x3 = x2 + 0.5 * _swiglu(h, wg_ref, wu_ref, wd_ref)
    y_ref[...] = _rms(x3, gfin_ref[...])


def _out_ffn(x1, of, oa, B, S, gf, ga, wout, g2, wg, wu, wd, gfin, tm):
    N = B * S
    nst = S // tm
    tok = pl.BlockSpec((tm, D_MODEL), lambda i: (i, 0))
    att = pl.BlockSpec((None, N_HEAD_PAIRS, tm, LANES), lambda i: (i // nst, 0, i % nst, 0))
    return pl.pallas_call(
        _out_ffn_kernel,
        grid=(N // tm,),
        in_specs=[tok, pl.BlockSpec((tm, D_FOURIER), lambda i: (i % nst, i // nst)), att,
                  _resident((1, D_FOURIER)), _resident((1, D_ATTN)), _resident((D_MODEL, D_MODEL)),
                  _resident((1, D_MODEL)), _resident((D_MODEL, D_FF)), _resident((D_MODEL, D_FF)),
                  _resident((D_FF, D_MODEL)), _resident((1, D_MODEL))],
        out_specs=tok,
        out_shape=jax.ShapeDtypeStruct((N, D_MODEL), F32),
        compiler_params=_params(1),
        name="out_ffn",
    )(x1, of, oa, gf, ga, wout, g2, wg, wu, wd, gfin)


def _row(g):
    return g.reshape(1, -1).astype(F32)


def _trunk(x, p, win):
    B, S, _ = x.shape
    x1 = _ffn1(x.reshape(B * S, D_MODEL), _row(p["ffn1_norm"]), p["ffn1_w_gate"], p["ffn1_w_up"],
               p["ffn1_w_down"], tm=FFN_TOKENS)
    a, c4_kvq, c16_kvq = _proj(x1, B, S, _row(p["mix_norm"]), win, tm=PROJ_TOKENS)
    C = max(1, S // (ATTN_BLOCKS_PER_STEP * Q_BLOCK))
    NB = max(1, ATTN_BLOCKS_PER_STEP * Q_BLOCK // S)
    oa = _attention(c4_kvq, c16_kvq, B, S, C, NB)
    of = _seq_dft(a, S, ts=DFT_ROWS, tk=DFT_DEPTH, tn=min(DFT_COLS, a.shape[-1]))
    y = _out_ffn(x1, of, oa, B, S, _row(p["fourier_out_norm"]), _row(p["attn_out_norm"]), p["w_out"],
                 _row(p["ffn2_norm"]), p["ffn2_w_gate"], p["ffn2_w_up"], p["ffn2_w_down"],
                 _row(p["final_norm"]), tm=FFN_TOKENS)
    return y.reshape(B, S, D_MODEL)


def kernel(x_prompt, x_sample, ffn1_norm, ffn1_w_gate, ffn1_w_up, ffn1_w_down, mix_norm, w_in, fourier_w,
           fourier_out_norm, attn_out_norm, w_out, ffn2_norm, ffn2_w_gate, ffn2_w_up, ffn2_w_down, final_norm):
    assert ffn1_w_gate.shape[0] == 1, "single-layer trunk"
    folded = _fold_fourier_weights(w_in[0, :, :D_FOURIER], fourier_w[0])
    win = jnp.concatenate([folded, w_in[0, :, D_FOURIER:].astype(BF16)], axis=1)
    p = dict(
        ffn1_norm=ffn1_norm[0], ffn1_w_gate=ffn1_w_gate[0].astype(BF16), ffn1_w_up=ffn1_w_up[0].astype(BF16),
        ffn1_w_down=ffn1_w_down[0].astype(BF16), mix_norm=mix_norm[0],
        fourier_out_norm=fourier_out_norm[0], attn_out_norm=attn_out_norm[0], w_out=w_out[0].astype(BF16),
        ffn2_norm=ffn2_norm[0], ffn2_w_gate=ffn2_w_gate[0].astype(BF16), ffn2_w_up=ffn2_w_up[0].astype(BF16),
        ffn2_w_down=ffn2_w_down[0].astype(BF16), final_norm=final_norm)
    return (_trunk(x_prompt, p, win), _trunk(x_sample, p, win))
```

```python
import functools
import math

import jax
import jax.numpy as jnp
import numpy as np
from jax import lax
from jax.experimental import pallas as pl
from jax.experimental.pallas import tpu as pltpu

F32 = jnp.float32
BF16 = jnp.bfloat16

D_MODEL = 1024
D_FF = 2816
D_FOURIER = 256
N_FOURIER_GROUPS = 4
FOURIER_GROUP_DIM = 64
D_ATTN = 768
HEAD_DIM = 64
LANES = 128
N_HEAD_PAIRS = D_ATTN // LANES
D_PROJ = 2 * D_FOURIER + 3 * D_ATTN
RADIUS = 64
ROPE_THETA = 10000.0
RMS_EPS = 1e-6
MASK_VALUE = -1e30

MXU_COLS = 256
FF_CHUNK = MXU_COLS
FFN_TOKENS = 1024
PROJ_TOKENS = 1024
SPLIT_BUFFERS = 6
DFT_ROWS = 512
DFT_TILE_ELEMS = 1024 * 1024
DFT_MAX_DEPTH = 1024
TABLE_SPLIT = 32
Q_BLOCK = 128
ATTN_BLOCKS_PER_STEP = 64
KEY_WINDOW = Q_BLOCK + 2 * RADIUS
LOG2_E = math.log2(math.e)
VMEM_LIMIT = 56 * 1024 * 1024


def _params(n_grid_dims):
    return pltpu.CompilerParams(dimension_semantics=("arbitrary",) * n_grid_dims,
                                vmem_limit_bytes=VMEM_LIMIT)


def _resident(shape):
    nd = len(shape)
    return pl.BlockSpec(shape, lambda *_: (0,) * nd, pipeline_mode=pl.Buffered(1))


def _rms(x, g):
    inv = lax.rsqrt(jnp.mean(x * x, axis=-1, keepdims=True) + RMS_EPS)
    return x * inv * g


def _dot(a, b):
    return jnp.dot(a, b, preferred_element_type=F32)


def _swiglu(h, wg_ref, wu_ref, wd_ref):
    acc = None
    for c in range(D_FF // FF_CHUNK):
        sl = slice(c * FF_CHUNK, (c + 1) * FF_CHUNK)
        g = _dot(h, wg_ref[:, sl])
        u = _dot(h, wu_ref[:, sl])
        a = (g * jax.nn.sigmoid(g) * u).astype(BF16)
        part = _dot(a, wd_ref[sl, :])
        acc = part if acc is None else acc + part
    return acc


def _fold_kernel(winf_ref, cbd_ref, sbd_ref, wbd_ref, o_ref):
    hi = lax.Precision.HIGHEST
    scale = FOURIER_GROUP_DIM ** -0.5
    gc = jnp.dot(cbd_ref[...], wbd_ref[...], precision=hi, preferred_element_type=F32) * scale
    gs = jnp.dot(sbd_ref[...], wbd_ref[...], precision=hi, preferred_element_type=F32) * scale
    winf = winf_ref[...]
    o_ref[:, :D_FOURIER] = jnp.dot(winf, gc, precision=hi, preferred_element_type=F32).astype(BF16)
    o_ref[:, D_FOURIER:] = jnp.dot(winf, gs, precision=hi, preferred_element_type=F32).astype(BF16)


def _fold_fourier_weights(w_in_f, fourier_w):
    c = np.arange(FOURIER_GROUP_DIM)
    ang = 2.0 * np.pi * ((c[:, None] * c[None, :]) % FOURIER_GROUP_DIM) / FOURIER_GROUP_DIM
    eye = np.eye(N_FOURIER_GROUPS)
    cbd = jnp.asarray(np.kron(eye, np.cos(ang)), F32)
    sbd = jnp.asarray(np.kron(eye, np.sin(ang)), F32)
    wbd = jnp.zeros((D_FOURIER, D_FOURIER), F32)
    for g in range(N_FOURIER_GROUPS):
        s = g * FOURIER_GROUP_DIM
        wbd = lax.dynamic_update_slice(wbd, fourier_w[g].astype(F32), (s, s))
    return pl.pallas_call(
        _fold_kernel,
        out_shape=jax.ShapeDtypeStruct((D_MODEL, 2 * D_FOURIER), BF16),
        name="fold_fourier",
    )(w_in_f, cbd, sbd, wbd)


def _ffn1_kernel(x_ref, g1_ref, wg_ref, wu_ref, wd_ref, x1_ref):
    x = x_ref[...]
    h = _rms(x, g1_ref[...]).astype(BF16)
    x1_ref[...] = x + 0.5 * _swiglu(h, wg_ref, wu_ref, wd_ref)


def _ffn1(xf, g1, wg, wu, wd, tm):
    N = xf.shape[0]
    tok = pl.BlockSpec((tm, D_MODEL), lambda i: (i, 0))
    return pl.pallas_call(
        _ffn1_kernel,
        grid=(N // tm,),
        in_specs=[tok, _resident((1, D_MODEL)), _resident((D_MODEL, D_FF)), _resident((D_MODEL, D_FF)),
                  _resident((D_FF, D_MODEL))],
        out_specs=tok,
        out_shape=jax.ShapeDtypeStruct((N, D_MODEL), F32),
        compiler_params=_params(1),
        name="ffn1",
    )(xf, g1, wg, wu, wd)


def _rope(t, cos, sin_lo, sin_hi):
    return (t * cos + pltpu.roll(t, LANES - HEAD_DIM // 2, axis=1) * sin_lo
            + pltpu.roll(t, HEAD_DIM // 2, axis=1) * sin_hi)


def _proj_kernel(x1_ref, gm_ref, win_ref, cos_ref, slo_ref, shi_ref,
                 a_ref, o4_ref, o16_ref,
                 nat_ref, cls4_ref, pa_ref):
    tm = x1_ref.shape[0]
    h2 = _rms(x1_ref[...], gm_ref[...]).astype(BF16)

    def emit(t, hp, slot):
        e = (hp * 3 + slot) % nat_ref.shape[0]
        nat_ref[e] = t
        for r4 in range(4):
            c4 = nat_ref[e, pl.ds(r4, tm // 4, stride=4), :]
            o4_ref[hp, slot, r4] = c4.astype(BF16)
            cls4_ref[e, r4] = c4
            for j in range(4):
                o16_ref[hp, slot, r4 + 4 * j] = cls4_ref[e, r4, pl.ds(j, tm // 16, stride=4), :].astype(BF16)

    cos = cos_ref[...]
    slo = slo_ref[...]
    shi = shi_ref[...]
    base = 2 * D_FOURIER
    for j in range(D_ATTN // MXU_COLS):
        c0 = j * MXU_COLS
        qq = _dot(h2, win_ref[:, base + c0: base + c0 + MXU_COLS])
        kk = _dot(h2, win_ref[:, base + D_ATTN + c0: base + D_ATTN + c0 + MXU_COLS])
        vv = _dot(h2, win_ref[:, base + 2 * D_ATTN + c0: base + 2 * D_ATTN + c0 + MXU_COLS])
        for i in range(MXU_COLS // LANES):
            sl = slice(i * LANES, (i + 1) * LANES)
            hp = (MXU_COLS // LANES) * j + i
            emit(_rope(qq[:, sl], cos, slo, shi) * (HEAD_DIM ** -0.5 * LOG2_E), hp, Q_SLOT)
            emit(_rope(kk[:, sl], cos, slo, shi), hp, K_SLOT)
            emit(vv[:, sl], hp, V_SLOT)

    pa = _dot(h2, win_ref[:, :2 * D_FOURIER])
    tiles_per_part = D_FOURIER // LANES
    for j in range(2 * tiles_per_part):
        pa_ref[j] = pa[:, j * LANES:(j + 1) * LANES]
    for par in range(2):
        for j in range(2 * tiles_per_part):
            lanes = slice((j % tiles_per_part) * LANES, (j % tiles_per_part + 1) * LANES)
            a_ref[par, j // tiles_per_part, :, lanes] = pa_ref[j, pl.ds(par, tm // 2, stride=2), :].astype(BF16)


def _rope_tables(S):
    inv_freq = ROPE_THETA ** (-jnp.arange(0, HEAD_DIM, 2, dtype=F32) / HEAD_DIM)
    freqs = jnp.arange(S, dtype=F32)[:, None] * inv_freq[None, :]
    cos, sin = jnp.cos(freqs), jnp.sin(freqs)
    zero = jnp.zeros_like(sin)
    cos_t = jnp.tile(cos, (1, LANES // (HEAD_DIM // 2)))
    sin_lo = jnp.tile(jnp.concatenate([-sin, zero], axis=1), (1, LANES // HEAD_DIM))
    sin_hi = jnp.tile(jnp.concatenate([zero, sin], axis=1), (1, LANES // HEAD_DIM))
    return cos_t, sin_lo, sin_hi


def _proj(x1, B, S, gm, win, tm):
    N = B * S
    nst = S // tm
    cos_t, sin_lo, sin_hi = _rope_tables(S)
    tok = pl.BlockSpec((tm, D_MODEL), lambda i: (i, 0))
    tab = pl.BlockSpec((tm, LANES), lambda i: (i % nst, 0))

    def cls(d):
        spec = pl.BlockSpec((None, N_HEAD_PAIRS, 3, d, tm // d, LANES),
                            lambda i: (i // nst, 0, 0, 0, i % nst, 0))
        return spec, jax.ShapeDtypeStruct((B, N_HEAD_PAIRS, 3, d, S // d, LANES), BF16)

    c4, c4_shape = cls(4)
    c16, c16_shape = cls(16)
    return pl.pallas_call(
        _proj_kernel,
        grid=(N // tm,),
        in_specs=[tok, _resident((1, D_MODEL)), _resident((D_MODEL, D_PROJ)), tab, tab, tab],
        out_specs=[pl.BlockSpec((2, 2, tm // 2, D_FOURIER), lambda i: (0, 0, i % nst, i // nst)),
                   c4, c16],
        out_shape=[jax.ShapeDtypeStruct((2, 2, S // 2, B * D_FOURIER), BF16), c4_shape, c16_shape],
        scratch_shapes=[pltpu.VMEM((SPLIT_BUFFERS, tm, LANES), F32),
                        pltpu.VMEM((SPLIT_BUFFERS, 4, tm // 4, LANES), F32),
                        pltpu.VMEM((2 * D_FOURIER // LANES, tm, LANES), F32)],
        compiler_params=_params(1),
        name="proj",
    )(x1, gm, win, cos_t, sin_lo, sin_hi)


def _class_order(n):
    return np.arange(n).reshape(n // 4, 4).T.reshape(-1)


def _band_bias(offsets, W, extra=None):
    qi = np.arange(Q_BLOCK)[:, None]
    kj = np.arange(W)[None, :]
    tiles = []
    for t, off in enumerate(offsets):
        ok = np.abs(kj + off - qi) <= RADIUS
        if extra is not None:
            ok &= extra[t](kj)
        tiles.append(np.where(ok, 0.0, MASK_VALUE))
    return jnp.asarray(np.stack(tiles), F32)


K_SLOT, V_SLOT, Q_SLOT = 0, 1, 2


def _attn_kernel(c4_ref, c16_ref, *refs, S, C, NB):
    if C > 1:
        rows_ref, prev_ref, next_ref, *refs = refs
    else:
        rows_ref, prev_ref, next_ref = c4_ref, None, None
    b1_ref, b4_ref, b16_ref, o_ref, acc_ref, m_ref, l_ref = refs
    phase = pl.program_id(2)
    c = pl.program_id(3)
    lane = lax.broadcasted_iota(jnp.int32, (1, LANES), 1)
    first_head = lane < HEAD_DIM
    contract_last = (((1,), (1,)), ((), ()))

    def block_stats(q2, k2, v2, bias):
        v_ones = jnp.concatenate([v2, jnp.ones_like(v2)], axis=1)
        per_head = []
        for head_mask in (first_head, jnp.logical_not(first_head)):
            qh = jnp.where(head_mask, q2, jnp.zeros_like(q2))
            s = lax.dot_general(qh, k2, contract_last, preferred_element_type=F32) + bias
            m = jnp.max(s, axis=-1, keepdims=True)
            p = jnp.exp2(s - m).astype(BF16)
            pvl = _dot(p, v_ones)
            per_head.append((pvl[:, :LANES], jnp.broadcast_to(m, (Q_BLOCK, LANES)), pvl[:, LANES:]))
        return tuple(jnp.where(first_head, a, b) for a, b in zip(*per_head))

    def state(ref, bb, rows):
        if isinstance(rows, (list, tuple)):
            return jnp.concatenate([ref[bb, r, :] for r in rows], axis=0)
        return ref[bb, rows, :]

    def merge(bb, rows, pv, m, l, mode):
        if mode == "first":
            acc_ref[bb, rows, :] = pv
            m_ref[bb, rows, :] = m
            l_ref[bb, rows, :] = l
            return None
        m_old = state(m_ref, bb, rows)
        m_new = jnp.maximum(m_old, m)
        a_old = jnp.exp2(m_old - m_new)
        a_blk = jnp.exp2(m - m_new)
        acc = a_old * state(acc_ref, bb, rows) + a_blk * pv
        den = a_old * state(l_ref, bb, rows) + a_blk * l
        if mode == "last":
            return acc / den
        acc_ref[bb, rows, :] = acc
        m_ref[bb, rows, :] = m_new
        l_ref[bb, rows, :] = den
        return None

    def strided_phase(d, cls_ref, bias_ref, mode):
        L = S // d
        nblk = L // Q_BLOCK
        W = min(KEY_WINDOW, L)
        cpc = d // C
        for bb in range(NB):
            for rl in range(cpc):
                for i in range(nblk):
                    a = i * Q_BLOCK
                    ws = min(max(a - RADIUS, 0), L - W)
                    typ = 0 if i == 0 else (2 if i == nblk - 1 else 1)
                    k2 = cls_ref[bb, K_SLOT, rl, ws:ws + W, :]
                    v2 = cls_ref[bb, V_SLOT, rl, ws:ws + W, :]
                    bias = bias_ref[typ]
                    if W < KEY_WINDOW:
                        pad = jnp.zeros((KEY_WINDOW - W, LANES), BF16)
                        k2 = jnp.concatenate([k2, pad], axis=0)
                        v2 = jnp.concatenate([v2, pad], axis=0)
                        bias = jnp.concatenate([bias, jnp.full((Q_BLOCK, KEY_WINDOW - W), MASK_VALUE, F32)],
                                               axis=1)
                    pv, m, l = block_stats(cls_ref[bb, Q_SLOT, rl, a:a + Q_BLOCK, :], k2, v2, bias)
                    cls = c * cpc + rl
                    if d == 4:
                        rows = pl.ds(pl.multiple_of(cls * L + a, Q_BLOCK), Q_BLOCK)
                    else:
                        rows = pl.ds((cls % 4) * (S // 4) + (d // 4) * a + cls // 4, Q_BLOCK, stride=4)
                    merge(bb, rows, pv, m, l, mode)

    def contiguous_phase():
        Sc = S // C
        nblk = Sc // Q_BLOCK
        Lc = Sc // 4
        halo = RADIUS // 4

        def gather(bb, slot, lo, hi):
            outside = jnp.zeros((halo, LANES), BF16)
            parts = []
            for r4 in range(4):
                if lo < 0:
                    parts.append(outside if prev_ref is None else prev_ref[bb, slot, r4])
                parts.append(rows_ref[bb, slot, r4, max(lo, 0) // 4:min(hi, Sc) // 4, :])
                if hi > Sc:
                    parts.append(outside if next_ref is None else next_ref[bb, slot, r4])
            return jnp.concatenate(parts, axis=0)

        assert Lc % halo == 0
        for bb in range(NB):
            for i in range(nblk):
                a = i * Q_BLOCK
                typ = 0
                if i == 0:
                    typ = jnp.where(c == 0, 1, typ)
                if i == nblk - 1:
                    typ = jnp.where(c == C - 1, 2, typ)
                pv, m, l = block_stats(gather(bb, Q_SLOT, a, a + Q_BLOCK),
                                       gather(bb, K_SLOT, a - RADIUS, a + Q_BLOCK + RADIUS),
                                       gather(bb, V_SLOT, a - RADIUS, a + Q_BLOCK + RADIUS), b1_ref[typ])
                pos = c * Sc + a
                part = Q_BLOCK // 4
                rows = [pl.ds(pl.multiple_of(r4 * (S // 4) + pos // 4, part), part) for r4 in range(4)]
                out = merge(bb, rows, pv, m, l, "last")
                for r4 in range(4):
                    o_ref[bb, pl.ds(pos + r4, part, stride=4), :] = out[r4 * part:(r4 + 1) * part, :]

    @pl.when(phase == 0)
    def _():
        strided_phase(4, c4_ref, b4_ref, "first")

    @pl.when(phase == 1)
    def _():
        strided_phase(16, c16_ref, b16_ref, "mid")

    @pl.when(phase == 2)
    def _():
        contiguous_phase()


def _attention(c4_kvq, c16_kvq, B, S, C, NB):
    Sc = S // C
    assert Sc % Q_BLOCK == 0 and 4 % C == 0 and B % NB == 0
    halo = RADIUS // 4
    Lc = Sc // 4
    halo_blocks = S // 4 // halo

    def chunk_of(own_phase, p, c):
        return jnp.where(p < own_phase, 0, jnp.where(p > own_phase, C - 1, c))

    def cls_spec(d, own_phase):
        L = S // d
        assert L % Q_BLOCK == 0 and (L == Q_BLOCK or L >= KEY_WINDOW)
        return pl.BlockSpec((NB, None, 3, d // C, L, LANES),
                            lambda b, h, p, c: (b, h, 0, chunk_of(own_phase, p, c), 0, 0))

    rows = pl.BlockSpec((NB, None, 3, 4, Lc, LANES), lambda b, h, p, c: (b, h, 0, 0, chunk_of(2, p, c), 0))
    prev = pl.BlockSpec((NB, None, 2, 4, halo, LANES),
                        lambda b, h, p, c: (b, h, 0, 0, jnp.maximum(chunk_of(2, p, c) * (Lc // halo) - 1, 0), 0))
    nxt = pl.BlockSpec((NB, None, 2, 4, halo, LANES),
                       lambda b, h, p, c: (b, h, 0, 0, jnp.minimum((chunk_of(2, p, c) + 1) * (Lc // halo),
                                                                    halo_blocks - 1), 0))
    c4, c16 = cls_spec(4, 0), cls_spec(16, 1)
    W4, W16 = min(KEY_WINDOW, S // 4), min(KEY_WINDOW, S // 16)
    clamped = (0, -RADIUS, -2 * RADIUS)
    b1 = _band_bias((-RADIUS,) * 3, KEY_WINDOW,
                    extra=(lambda kj: kj >= 0, lambda kj: kj >= RADIUS, lambda kj: kj < KEY_WINDOW - RADIUS))
    b1 = b1[:, _class_order(Q_BLOCK), :][:, :, _class_order(KEY_WINDOW)]
    b4, b16 = _band_bias(clamped, W4), _band_bias(clamped, W16)
    return pl.pallas_call(
        functools.partial(_attn_kernel, S=S, C=C, NB=NB),
        grid=(B // NB, N_HEAD_PAIRS, 3, C),
        in_specs=[c4, c16] + ([rows, prev, nxt] if C > 1 else [])
        + [_resident(b1.shape), _resident(b4.shape), _resident(b16.shape)],
        out_specs=pl.BlockSpec((NB, None, S, LANES), lambda b, h, p, c: (b, h, 0, 0)),
        out_shape=jax.ShapeDtypeStruct((B, N_HEAD_PAIRS, S, LANES), F32),
        scratch_shapes=[pltpu.VMEM((NB, S, LANES), F32)] * 3,
        compiler_params=_params(4),
        name="attn",
    )(c4_kvq, c16_kvq, *([c4_kvq] * 3 if C > 1 else []), b1, b4, b16)


def _dft_kernel(bc_ref, nbs_ref, r1_ref, r2_ref, a_ref, o_ref):
    t = pl.program_id(2)

    @pl.when(t == 0)
    def _():
        o_ref[...] = jnp.zeros_like(o_ref)

    for par in range(2):
        bc, nbs = bc_ref[par], nbs_ref[par]
        r1, r2 = r1_ref[par], r2_ref[par]
        t_cos = (bc * r1 + nbs * r2).astype(BF16)
        t_nsin = (nbs * r1 - bc * r2).astype(BF16)
        o_ref[par] += _dot(t_cos, a_ref[par, 0]) + _dot(t_nsin, a_ref[par, 1])

    @pl.when(t == pl.num_programs(2) - 1)
    def _():
        f0, f1 = o_ref[0], o_ref[1]
        o_ref[0] = f0 + f1
        o_ref[1] = f0 - f1


def _seq_dft(a, S, ts, tk, tn):
    ncols = a.shape[-1]
    H = S // 2
    t = 2 * jnp.arange(H, dtype=jnp.int32)[None, None, :] + jnp.arange(2, dtype=jnp.int32)[:, None, None]
    w = 2.0 * math.pi / S
    ang_r = ((jnp.arange(0, H, ts, dtype=jnp.int32)[None, :, None] * t) % S).astype(F32) * w
    lo = TABLE_SPLIT
    ang_hi = ((jnp.arange(0, ts, lo, dtype=jnp.int32)[None, :, None] * t) % S).astype(F32) * w
    ang_lo = ((jnp.arange(lo, dtype=jnp.int32)[None, :, None] * t) % S).astype(F32) * w
    c_hi, s_hi = jnp.cos(ang_hi)[:, :, None, :], jnp.sin(ang_hi)[:, :, None, :]
    c_lo, s_lo = jnp.cos(ang_lo)[:, None, :, :], jnp.sin(ang_lo)[:, None, :, :]
    bc = (c_hi * c_lo - s_hi * s_lo).reshape(2, ts, H)
    nbs = -(s_hi * c_lo + c_hi * s_lo).reshape(2, ts, H)
    scale = S ** -0.5
    r1 = (jnp.cos(ang_r) * scale).reshape(2, H // ts, 1, H)
    r2 = (jnp.sin(ang_r) * scale).reshape(2, H // ts, 1, H)
    btab = pl.BlockSpec((2, ts, tk), lambda s, n, t: (0, 0, t))
    rtab = pl.BlockSpec((2, None, 1, tk), lambda s, n, t: (0, s, 0, t))
    out = pl.pallas_call(
        _dft_kernel,
        grid=(H // ts, ncols // tn, H // tk),
        in_specs=[btab, btab, rtab, rtab, pl.BlockSpec((2, 2, tk, tn), lambda s, n, t: (0, 0, t, n))],
        out_specs=pl.BlockSpec((2, ts, tn), lambda s, n, t: (0, s, n)),
        out_shape=jax.ShapeDtypeStruct((2, H, ncols), F32),
        compiler_params=_params(3),
        name="seq_dft",
    )(bc, nbs, r1, r2, a)
    return out.reshape(S, ncols)


def _out_ffn_kernel(x1_ref, of_ref, oa_ref, gf_ref, ga_ref, wout_ref, g2_ref, wg_ref, wu_ref, wd_ref,
                    gfin_ref, y_ref):
    oa = jnp.concatenate([oa_ref[hp] for hp in range(N_HEAD_PAIRS)], axis=1)
    na = _rms(oa, ga_ref[...]).astype(BF16)
    nf = _rms(of_ref[...], gf_ref[...]).astype(BF16)
    x2 = x1_ref[...] + _dot(nf, wout_ref[:D_FOURIER, :]) + _dot(na, wout_ref[D_FOURIER:, :])
    h = _rms(x2, g2_ref[...]).astype(BF16)
    x3 = x2 + 0.5 * _swiglu(h, wg_ref, wu_ref, wd_ref)
    y_ref[...] = _rms(x3, gfin_ref[...])


def _out_ffn(x1, of, oa, B, S, gf, ga, wout, g2, wg, wu, wd, gfin, tm):
    N = B * S
    nst = S // tm
    tok = pl.BlockSpec((tm, D_MODEL), lambda i: (i, 0))
    att = pl.BlockSpec((None, N_HEAD_PAIRS, tm, LANES), lambda i: (i // nst, 0, i % nst, 0))
    return pl.pallas_call(
        _out_ffn_kernel,
        grid=(N // tm,),
        in_specs=[tok, pl.BlockSpec((tm, D_FOURIER), lambda i: (i % nst, i // nst)), att,
                  _resident((1, D_FOURIER)), _resident((1, D_ATTN)), _resident((D_MODEL, D_MODEL)),
                  _resident((1, D_MODEL)), _resident((D_MODEL, D_FF)), _resident((D_MODEL, D_FF)),
                  _resident((D_FF, D_MODEL)), _resident((1, D_MODEL))],
        out_specs=tok,
        out_shape=jax.ShapeDtypeStruct((N, D_MODEL), F32),
        compiler_params=_params(1),
        name="out_ffn",
    )(x1, of, oa, gf, ga, wout, g2, wg, wu, wd, gfin)


def _row(g):
    return g.reshape(1, -1).astype(F32)


def _trunk(x, p, win):
    B, S, _ = x.shape
    x1 = _ffn1(x.reshape(B * S, D_MODEL), _row(p["ffn1_norm"]), p["ffn1_w_gate"], p["ffn1_w_up"],
               p["ffn1_w_down"], tm=FFN_TOKENS)
    a, c4_kvq, c16_kvq = _proj(x1, B, S, _row(p["mix_norm"]), win, tm=PROJ_TOKENS)
    C = max(1, S // (ATTN_BLOCKS_PER_STEP * Q_BLOCK))
    NB = max(1, ATTN_BLOCKS_PER_STEP * Q_BLOCK // S)
    oa = _attention(c4_kvq, c16_kvq, B, S, C, NB)
    tk = S // 2 if S // 2 <= DFT_MAX_DEPTH else DFT_MAX_DEPTH // 2
    of = _seq_dft(a, S, ts=DFT_ROWS, tk=tk, tn=min(DFT_TILE_ELEMS // tk, a.shape[-1]))
    y = _out_ffn(x1, of, oa, B, S, _row(p["fourier_out_norm"]), _row(p["attn_out_norm"]), p["w_out"],
                 _row(p["ffn2_norm"]), p["ffn2_w_gate"], p["ffn2_w_up"], p["ffn2_w_down"],
                 _row(p["final_norm"]), tm=FFN_TOKENS)
    return y.reshape(B, S, D_MODEL)


def kernel(x_prompt, x_sample, ffn1_norm, ffn1_w_gate, ffn1_w_up, ffn1_w_down, mix_norm, w_in, fourier_w,
           fourier_out_norm, attn_out_norm, w_out, ffn2_norm, ffn2_w_gate, ffn2_w_up, ffn2_w_down, final_norm):
    assert ffn1_w_gate.shape[0] == 1, "single-layer trunk"
    folded = _fold_fourier_weights(w_in[0, :, :D_FOURIER], fourier_w[0])
    win = jnp.concatenate([folded, w_in[0, :, D_FOURIER:].astype(BF16)], axis=1)
    p = dict(
        ffn1_norm=ffn1_norm[0], ffn1_w_gate=ffn1_w_gate[0].astype(BF16), ffn1_w_up=ffn1_w_up[0].astype(BF16),
        ffn1_w_down=ffn1_w_down[0].astype(BF16), mix_norm=mix_norm[0],
        fourier_out_norm=fourier_out_norm[0], attn_out_norm=attn_out_norm[0], w_out=w_out[0].astype(BF16),
        ffn2_norm=ffn2_norm[0], ffn2_w_gate=ffn2_w_gate[0].astype(BF16), ffn2_w_up=ffn2_w_up[0].astype(BF16),
        ffn2_w_down=ffn2_w_down[0].astype(BF16), final_norm=final_norm)
    return (_trunk(x_prompt, p, win), _trunk(x_sample, p, win))
```

```python
import functools
import math

import jax
import jax.numpy as jnp
import numpy as np
from jax import lax
from jax.experimental import pallas as pl
from jax.experimental.pallas import tpu as pltpu

F32 = jnp.float32
BF16 = jnp.bfloat16

D_MODEL = 1024
D_FF = 2816
D_FOURIER = 256
N_FOURIER_GROUPS = 4
FOURIER_GROUP_DIM = 64
D_ATTN = 768
HEAD_DIM = 64
LANES = 128
N_HEAD_PAIRS = D_ATTN // LANES
D_PROJ = 2 * D_FOURIER + 3 * D_ATTN
RADIUS = 64
ROPE_THETA = 10000.0
RMS_EPS = 1e-6
MASK_VALUE = -1e30

MXU_COLS = 256
FF_CHUNK = MXU_COLS
FFN_TOKENS = 1024
PROJ_TOKENS = 1024
SPLIT_BUFFERS = 6
DFT_ROWS = 512
DFT_TILE_ELEMS = 1024 * 1024
DFT_MAX_DEPTH = 1024
TABLE_SPLIT = 32
Q_BLOCK = 128
ATTN_BLOCKS_PER_STEP = 64
KEY_WINDOW = Q_BLOCK + 2 * RADIUS
LOG2_E = math.log2(math.e)
VMEM_LIMIT = 56 * 1024 * 1024


def _params(n_grid_dims):
    return pltpu.CompilerParams(dimension_semantics=("arbitrary",) * n_grid_dims,
                                vmem_limit_bytes=VMEM_LIMIT)


def _resident(shape):
    nd = len(shape)
    return pl.BlockSpec(shape, lambda *_: (0,) * nd, pipeline_mode=pl.Buffered(1))


def _rms(x, g):
    inv = lax.rsqrt(jnp.mean(x * x, axis=-1, keepdims=True) + RMS_EPS)
    return x * inv * g


def _dot(a, b):
    return jnp.dot(a, b, preferred_element_type=F32)


def _swiglu(h, wg_ref, wu_ref, wd_ref):
    acc = None
    for c in range(D_FF // FF_CHUNK):
        sl = slice(c * FF_CHUNK, (c + 1) * FF_CHUNK)
        g = _dot(h, wg_ref[:, sl])
        u = _dot(h, wu_ref[:, sl])
        a = (g * jax.nn.sigmoid(g) * u).astype(BF16)
        part = _dot(a, wd_ref[sl, :])
        acc = part if acc is None else acc + part
    return acc


def _fold_kernel(winf_ref, cbd_ref, sbd_ref, wbd_ref, o_ref):
    hi = lax.Precision.HIGHEST
    scale = FOURIER_GROUP_DIM ** -0.5
    gc = jnp.dot(cbd_ref[...], wbd_ref[...], precision=hi, preferred_element_type=F32) * scale
    gs = jnp.dot(sbd_ref[...], wbd_ref[...], precision=hi, preferred_element_type=F32) * scale
    winf = winf_ref[...]
    o_ref[:, :D_FOURIER] = jnp.dot(winf, gc, precision=hi, preferred_element_type=F32).astype(BF16)
    o_ref[:, D_FOURIER:] = jnp.dot(winf, gs, precision=hi, preferred_element_type=F32).astype(BF16)


def _fold_fourier_weights(w_in_f, fourier_w):
    c = np.arange(FOURIER_GROUP_DIM)
    ang = 2.0 * np.pi * ((c[:, None] * c[None, :]) % FOURIER_GROUP_DIM) / FOURIER_GROUP_DIM
    eye = np.eye(N_FOURIER_GROUPS)
    cbd = jnp.asarray(np.kron(eye, np.cos(ang)), F32)
    sbd = jnp.asarray(np.kron(eye, np.sin(ang)), F32)
    wbd = jnp.zeros((D_FOURIER, D_FOURIER), F32)
    for g in range(N_FOURIER_GROUPS):
        s = g * FOURIER_GROUP_DIM
        wbd = lax.dynamic_update_slice(wbd, fourier_w[g].astype(F32), (s, s))
    return pl.pallas_call(
        _fold_kernel,
        out_shape=jax.ShapeDtypeStruct((D_MODEL, 2 * D_FOURIER), BF16),
        name="fold_fourier",
    )(w_in_f, cbd, sbd, wbd)


def _ffn1_kernel(x_ref, g1_ref, wg_ref, wu_ref, wd_ref, x1_ref):
    x = x_ref[...]
    h = _rms(x, g1_ref[...]).astype(BF16)
    x1_ref[...] = x + 0.5 * _swiglu(h, wg_ref, wu_ref, wd_ref)


def _ffn1(xf, g1, wg, wu, wd, tm):
    N = xf.shape[0]
    tok = pl.BlockSpec((tm, D_MODEL), lambda i: (i, 0))
    return pl.pallas_call(
        _ffn1_kernel,
        grid=(N // tm,),
        in_specs=[tok, _resident((1, D_MODEL)), _resident((D_MODEL, D_FF)), _resident((D_MODEL, D_FF)),
                  _resident((D_FF, D_MODEL))],
        out_specs=tok,
        out_shape=jax.ShapeDtypeStruct((N, D_MODEL), F32),
        compiler_params=_params(1),
        name="ffn1",
    )(xf, g1, wg, wu, wd)


def _rope(t, cos, sin_lo, sin_hi):
    return (t * cos + pltpu.roll(t, LANES - HEAD_DIM // 2, axis=1) * sin_lo
            + pltpu.roll(t, HEAD_DIM // 2, axis=1) * sin_hi)


def _proj_kernel(x1_ref, gm_ref, win_ref, cos_ref, slo_ref, shi_ref,
                 a_ref, o4_ref, o16_ref,
                 nat_ref, cls4_ref, pa_ref):
    tm = x1_ref.shape[0]
    h2 = _rms(x1_ref[...], gm_ref[...]).astype(BF16)

    def emit(t, hp, slot):
        e = (hp * 3 + slot) % nat_ref.shape[0]
        nat_ref[e] = t
        for r4 in range(4):
            c4 = nat_ref[e, pl.ds(r4, tm // 4, stride=4), :]
            o4_ref[hp, slot, r4] = c4.astype(BF16)
            cls4_ref[e, r4] = c4
            for j in range(4):
                o16_ref[hp, slot, r4 + 4 * j] = cls4_ref[e, r4, pl.ds(j, tm // 16, stride=4), :].astype(BF16)

    cos = cos_ref[...]
    slo = slo_ref[...]
    shi = shi_ref[...]
    base = 2 * D_FOURIER
    for j in range(D_ATTN // MXU_COLS):
        c0 = j * MXU_COLS
        qq = _dot(h2, win_ref[:, base + c0: base + c0 + MXU_COLS])
        kk = _dot(h2, win_ref[:, base + D_ATTN + c0: base + D_ATTN + c0 + MXU_COLS])
        vv = _dot(h2, win_ref[:, base + 2 * D_ATTN + c0: base + 2 * D_ATTN + c0 + MXU_COLS])
        for i in range(MXU_COLS // LANES):
            sl = slice(i * LANES, (i + 1) * LANES)
            hp = (MXU_COLS // LANES) * j + i
            emit(_rope(qq[:, sl], cos, slo, shi) * (HEAD_DIM ** -0.5 * LOG2_E), hp, Q_SLOT)
            emit(_rope(kk[:, sl], cos, slo, shi), hp, K_SLOT)
            emit(vv[:, sl], hp, V_SLOT)

    pa = _dot(h2, win_ref[:, :2 * D_FOURIER])
    tiles_per_part = D_FOURIER // LANES
    for j in range(2 * tiles_per_part):
        pa_ref[j] = pa[:, j * LANES:(j + 1) * LANES]
    for par in range(2):
        for j in range(2 * tiles_per_part):
            lanes = slice((j % tiles_per_part) * LANES, (j % tiles_per_part + 1) * LANES)
            a_ref[par, j // tiles_per_part, :, lanes] = pa_ref[j, pl.ds(par, tm // 2, stride=2), :].astype(BF16)


def _rope_tables(S):
    inv_freq = ROPE_THETA ** (-jnp.arange(0, HEAD_DIM, 2, dtype=F32) / HEAD_DIM)
    freqs = jnp.arange(S, dtype=F32)[:, None] * inv_freq[None, :]
    cos, sin = jnp.cos(freqs), jnp.sin(freqs)
    zero = jnp.zeros_like(sin)
    cos_t = jnp.tile(cos, (1, LANES // (HEAD_DIM // 2)))
    sin_lo = jnp.tile(jnp.concatenate([-sin, zero], axis=1), (1, LANES // HEAD_DIM))
    sin_hi = jnp.tile(jnp.concatenate([zero, sin], axis=1), (1, LANES // HEAD_DIM))
    return cos_t, sin_lo, sin_hi


def _proj(x1, B, S, gm, win, tm):
    N = B * S
    nst = S // tm
    cos_t, sin_lo, sin_hi = _rope_tables(S)
    tok = pl.BlockSpec((tm, D_MODEL), lambda i: (i, 0))
    tab = pl.BlockSpec((tm, LANES), lambda i: (i % nst, 0))

    def cls(d):
        spec = pl.BlockSpec((None, N_HEAD_PAIRS, 3, d, tm // d, LANES),
                            lambda i: (i // nst, 0, 0, 0, i % nst, 0))
        return spec, jax.ShapeDtypeStruct((B, N_HEAD_PAIRS, 3, d, S // d, LANES), BF16)

    c4, c4_shape = cls(4)
    c16, c16_shape = cls(16)
    return pl.pallas_call(
        _proj_kernel,
        grid=(N // tm,),
        in_specs=[tok, _resident((1, D_MODEL)), _resident((D_MODEL, D_PROJ)), tab, tab, tab],
        out_specs=[pl.BlockSpec((2, 2, tm // 2, D_FOURIER), lambda i: (0, 0, i % nst, i // nst)),
                   c4, c16],
        out_shape=[jax.ShapeDtypeStruct((2, 2, S // 2, B * D_FOURIER), BF16), c4_shape, c16_shape],
        scratch_shapes=[pltpu.VMEM((SPLIT_BUFFERS, tm, LANES), F32),
                        pltpu.VMEM((SPLIT_BUFFERS, 4, tm // 4, LANES), F32),
                        pltpu.VMEM((2 * D_FOURIER // LANES, tm, LANES), F32)],
        compiler_params=_params(1),
        name="proj",
    )(x1, gm, win, cos_t, sin_lo, sin_hi)


def _class_order(n):
    return np.arange(n).reshape(n // 4, 4).T.reshape(-1)


def _band_bias(offsets, W, extra=None):
    qi = np.arange(Q_BLOCK)[:, None]
    kj = np.arange(W)[None, :]
    tiles = []
    for t, off in enumerate(offsets):
        ok = np.abs(kj + off - qi) <= RADIUS
        if extra is not None:
            ok &= extra[t](kj)
        tiles.append(np.where(ok, 0.0, MASK_VALUE))
    return jnp.asarray(np.stack(tiles), F32)


K_SLOT, V_SLOT, Q_SLOT = 0, 1, 2


def _attn_kernel(c4_ref, c16_ref, *refs, S, C, NB):
    if C > 1:
        rows_ref, prev_ref, next_ref, *refs = refs
    else:
        rows_ref, prev_ref, next_ref = c4_ref, None, None
    b1_ref, b4_ref, b16_ref, o_ref, acc_ref, m_ref, l_ref = refs
    phase = pl.program_id(2)
    c = pl.program_id(3)
    lane = lax.broadcasted_iota(jnp.int32, (1, LANES), 1)
    first_head = lane < HEAD_DIM
    contract_last = (((1,), (1,)), ((), ()))

    def block_stats(q2, k2, v2, bias):
        v_ones = jnp.concatenate([v2, jnp.ones_like(v2)], axis=1)
        per_head = []
        for head_mask in (first_head, jnp.logical_not(first_head)):
            qh = jnp.where(head_mask, q2, jnp.zeros_like(q2))
            s = lax.dot_general(qh, k2, contract_last, preferred_element_type=F32)
            real = bias.shape[1]
            s = s[:, :real] + bias
            m = jnp.max(s, axis=-1, keepdims=True)
            p = jnp.exp2(s - m).astype(BF16)
            if real < k2.shape[0]:
                p = jnp.concatenate([p, jnp.zeros((Q_BLOCK, k2.shape[0] - real), BF16)], axis=1)
            pvl = _dot(p, v_ones)
            per_head.append((pvl[:, :LANES], jnp.broadcast_to(m, (Q_BLOCK, LANES)), pvl[:, LANES:]))
        return tuple(jnp.where(first_head, a, b) for a, b in zip(*per_head))

    def state(ref, bb, rows):
        if isinstance(rows, (list, tuple)):
            return jnp.concatenate([ref[bb, r, :] for r in rows], axis=0)
        return ref[bb, rows, :]

    def merge(bb, rows, pv, m, l, mode):
        if mode == "first":
            acc_ref[bb, rows, :] = pv
            m_ref[bb, rows, :] = m
            l_ref[bb, rows, :] = l
            return None
        m_old = state(m_ref, bb, rows)
        m_new = jnp.maximum(m_old, m)
        a_old = jnp.exp2(m_old - m_new)
        a_blk = jnp.exp2(m - m_new)
        acc = a_old * state(acc_ref, bb, rows) + a_blk * pv
        den = a_old * state(l_ref, bb, rows) + a_blk * l
        if mode == "last":
            return acc / den
        acc_ref[bb, rows, :] = acc
        m_ref[bb, rows, :] = m_new
        l_ref[bb, rows, :] = den
        return None

    def strided_phase(d, cls_ref, bias_ref, mode):
        L = S // d
        nblk = L // Q_BLOCK
        W = min(KEY_WINDOW, L)
        cpc = d // C
        for bb in range(NB):
            for rl in range(cpc):
                for i in range(nblk):
                    a = i * Q_BLOCK
                    ws = min(max(a - RADIUS, 0), L - W)
                    typ = 0 if i == 0 else (2 if i == nblk - 1 else 1)
                    k2 = cls_ref[bb, K_SLOT, rl, ws:ws + W, :]
                    v2 = cls_ref[bb, V_SLOT, rl, ws:ws + W, :]
                    bias = bias_ref[typ]
                    if W < KEY_WINDOW:
                        pad = jnp.zeros((KEY_WINDOW - W, LANES), BF16)
                        k2 = jnp.concatenate([k2, pad], axis=0)
                        v2 = jnp.concatenate([v2, pad], axis=0)
                    pv, m, l = block_stats(cls_ref[bb, Q_SLOT, rl, a:a + Q_BLOCK, :], k2, v2, bias)
                    cls = c * cpc + rl
                    if d == 4:
                        rows = pl.ds(pl.multiple_of(cls * L + a, Q_BLOCK), Q_BLOCK)
                    else:
                        rows = pl.ds((cls % 4) * (S // 4) + (d // 4) * a + cls // 4, Q_BLOCK, stride=4)
                    merge(bb, rows, pv, m, l, mode)

    def contiguous_phase():
        Sc = S // C
        nblk = Sc // Q_BLOCK
        Lc = Sc // 4
        halo = RADIUS // 4

        def gather(bb, slot, lo, hi):
            outside = jnp.zeros((halo, LANES), BF16)
            parts = []
            for r4 in range(4):
                if lo < 0:
                    parts.append(outside if prev_ref is None else prev_ref[bb, slot, r4])
                parts.append(rows_ref[bb, slot, r4, max(lo, 0) // 4:min(hi, Sc) // 4, :])
                if hi > Sc:
                    parts.append(outside if next_ref is None else next_ref[bb, slot, r4])
            return jnp.concatenate(parts, axis=0)

        assert Lc % halo == 0
        for bb in range(NB):
            for i in range(nblk):
                a = i * Q_BLOCK
                typ = 0
                if i == 0:
                    typ = jnp.where(c == 0, 1, typ)
                if i == nblk - 1:
                    typ = jnp.where(c == C - 1, 2, typ)
                pv, m, l = block_stats(gather(bb, Q_SLOT, a, a + Q_BLOCK),
                                       gather(bb, K_SLOT, a - RADIUS, a + Q_BLOCK + RADIUS),
                                       gather(bb, V_SLOT, a - RADIUS, a + Q_BLOCK + RADIUS), b1_ref[typ])
                pos = c * Sc + a
                part = Q_BLOCK // 4
                rows = [pl.ds(pl.multiple_of(r4 * (S // 4) + pos // 4, part), part) for r4 in range(4)]
                out = merge(bb, rows, pv, m, l, "last")
                for r4 in range(4):
                    o_ref[bb, pl.ds(pos + r4, part, stride=4), :] = out[r4 * part:(r4 + 1) * part, :]

    @pl.when(phase == 0)
    def _():
        strided_phase(4, c4_ref, b4_ref, "first")

    @pl.when(phase == 1)
    def _():
        strided_phase(16, c16_ref, b16_ref, "mid")

    @pl.when(phase == 2)
    def _():
        contiguous_phase()


def _attention(c4_kvq, c16_kvq, B, S, C, NB):
    Sc = S // C
    assert Sc % Q_BLOCK == 0 and 4 % C == 0 and B % NB == 0
    halo = RADIUS // 4
    Lc = Sc // 4
    halo_blocks = S // 4 // halo

    def chunk_of(own_phase, p, c):
        return jnp.where(p < own_phase, 0, jnp.where(p > own_phase, C - 1, c))

    def cls_spec(d, own_phase):
        L = S // d
        assert L % Q_BLOCK == 0 and (L == Q_BLOCK or L >= KEY_WINDOW)
        return pl.BlockSpec((NB, None, 3, d // C, L, LANES),
                            lambda b, h, p, c: (b, h, 0, chunk_of(own_phase, p, c), 0, 0))

    rows = pl.BlockSpec((NB, None, 3, 4, Lc, LANES), lambda b, h, p, c: (b, h, 0, 0, chunk_of(2, p, c), 0))
    prev = pl.BlockSpec((NB, None, 2, 4, halo, LANES),
                        lambda b, h, p, c: (b, h, 0, 0, jnp.maximum(chunk_of(2, p, c) * (Lc // halo) - 1, 0), 0))
    nxt = pl.BlockSpec((NB, None, 2, 4, halo, LANES),
                       lambda b, h, p, c: (b, h, 0, 0, jnp.minimum((chunk_of(2, p, c) + 1) * (Lc // halo),
                                                                    halo_blocks - 1), 0))
    c4, c16 = cls_spec(4, 0), cls_spec(16, 1)
    W4, W16 = min(KEY_WINDOW, S // 4), min(KEY_WINDOW, S // 16)
    clamped = (0, -RADIUS, -2 * RADIUS)
    b1 = _band_bias((-RADIUS,) * 3, KEY_WINDOW,
                    extra=(lambda kj: kj >= 0, lambda kj: kj >= RADIUS, lambda kj: kj < KEY_WINDOW - RADIUS))
    b1 = b1[:, _class_order(Q_BLOCK), :][:, :, _class_order(KEY_WINDOW)]
    b4, b16 = _band_bias(clamped, W4), _band_bias(clamped, W16)
    return pl.pallas_call(
        functools.partial(_attn_kernel, S=S, C=C, NB=NB),
        grid=(B // NB, N_HEAD_PAIRS, 3, C),
        in_specs=[c4, c16] + ([rows, prev, nxt] if C > 1 else [])
        + [_resident(b1.shape), _resident(b4.shape), _resident(b16.shape)],
        out_specs=pl.BlockSpec((NB, None, S, LANES), lambda b, h, p, c: (b, h, 0, 0)),
        out_shape=jax.ShapeDtypeStruct((B, N_HEAD_PAIRS, S, LANES), F32),
        scratch_shapes=[pltpu.VMEM((NB, S, LANES), F32)] * 3,
        compiler_params=_params(4),
        name="attn",
    )(c4_kvq, c16_kvq, *([c4_kvq] * 3 if C > 1 else []), b1, b4, b16)


def _dft_kernel(bc_ref, nbs_ref, r1_ref, r2_ref, a_ref, o_ref):
    t = pl.program_id(2)

    @pl.when(t == 0)
    def _():
        o_ref[...] = jnp.zeros_like(o_ref)

    for par in range(2):
        bc, nbs = bc_ref[par], nbs_ref[par]
        r1, r2 = r1_ref[par], r2_ref[par]
        t_cos = (bc * r1 + nbs * r2).astype(BF16)
        t_nsin = (nbs * r1 - bc * r2).astype(BF16)
        o_ref[par] += _dot(t_cos, a_ref[par, 0]) + _dot(t_nsin, a_ref[par, 1])

    @pl.when(t == pl.num_programs(2) - 1)
    def _():
        f0, f1 = o_ref[0], o_ref[1]
        o_ref[0] = f0 + f1
        o_ref[1] = f0 - f1


def _seq_dft(a, S, ts, tk, tn):
    ncols = a.shape[-1]
    H = S // 2
    t = 2 * jnp.arange(H, dtype=jnp.int32)[None, None, :] + jnp.arange(2, dtype=jnp.int32)[:, None, None]
    w = 2.0 * math.pi / S
    ang_r = ((jnp.arange(0, H, ts, dtype=jnp.int32)[None, :, None] * t) % S).astype(F32) * w
    lo = TABLE_SPLIT
    ang_hi = ((jnp.arange(0, ts, lo, dtype=jnp.int32)[None, :, None] * t) % S).astype(F32) * w
    ang_lo = ((jnp.arange(lo, dtype=jnp.int32)[None, :, None] * t) % S).astype(F32) * w
    c_hi, s_hi = jnp.cos(ang_hi)[:, :, None, :], jnp.sin(ang_hi)[:, :, None, :]
    c_lo, s_lo = jnp.cos(ang_lo)[:, None, :, :], jnp.sin(ang_lo)[:, None, :, :]
    bc = (c_hi * c_lo - s_hi * s_lo).reshape(2, ts, H)
    nbs = -(s_hi * c_lo + c_hi * s_lo).reshape(2, ts, H)
    scale = S ** -0.5
    r1 = (jnp.cos(ang_r) * scale).reshape(2, H // ts, 1, H)
    r2 = (jnp.sin(ang_r) * scale).reshape(2, H // ts, 1, H)
    btab = pl.BlockSpec((2, ts, tk), lambda s, n, t: (0, 0, t))
    rtab = pl.BlockSpec((2, None, 1, tk), lambda s, n, t: (0, s, 0, t))
    out = pl.pallas_call(
        _dft_kernel,
        grid=(H // ts, ncols // tn, H // tk),
        in_specs=[btab, btab, rtab, rtab, pl.BlockSpec((2, 2, tk, tn), lambda s, n, t: (0, 0, t, n))],
        out_specs=pl.BlockSpec((2, ts, tn), lambda s, n, t: (0, s, n)),
        out_shape=jax.ShapeDtypeStruct((2, H, ncols), F32),
        compiler_params=_params(3),
        name="seq_dft",
    )(bc, nbs, r1, r2, a)
    return out.reshape(S, ncols)


def _out_ffn_kernel(x1_ref, of_ref, oa_ref, gf_ref, ga_ref, wout_ref, g2_ref, wg_ref, wu_ref, wd_ref,
                    gfin_ref, y_ref):
    tm = x1_ref.shape[0]
    for half in range(2):
        r = slice(half * tm // 2, (half + 1) * tm // 2)
        oa = jnp.concatenate([oa_ref[hp, r, :] for hp in range(N_HEAD_PAIRS)], axis=1)
        na = _rms(oa, ga_ref[...]).astype(BF16)
        nf = _rms(of_ref[r, :], gf_ref[...]).astype(BF16)
        x2 = x1_ref[r, :] + _dot(nf, wout_ref[:D_FOURIER, :]) + _dot(na, wout_ref[D_FOURIER:, :])
        h = _rms(x2, g2_ref[...]).astype(BF16)
        x3 = x2 + 0.5 * _swiglu(h, wg_ref, wu_ref, wd_ref)
        y_ref[r, :] = _rms(x3, gfin_ref[...])


def _out_ffn(x1, of, oa, B, S, gf, ga, wout, g2, wg, wu, wd, gfin, tm):
    N = B * S
    nst = S // tm
    tok = pl.BlockSpec((tm, D_MODEL), lambda i: (i, 0))
    att = pl.BlockSpec((None, N_HEAD_PAIRS, tm, LANES), lambda i: (i // nst, 0, i % nst, 0))
    return pl.pallas_call(
        _out_ffn_kernel,
        grid=(N // tm,),
        in_specs=[tok, pl.BlockSpec((tm, D_FOURIER), lambda i: (i % nst, i // nst)), att,
                  _resident((1, D_FOURIER)), _resident((1, D_ATTN)), _resident((D_MODEL, D_MODEL)),
                  _resident((1, D_MODEL)), _resident((D_MODEL, D_FF)), _resident((D_MODEL, D_FF)),
                  _resident((D_FF, D_MODEL)), _resident((1, D_MODEL))],
        out_specs=tok,
        out_shape=jax.ShapeDtypeStruct((N, D_MODEL), F32),
        compiler_params=_params(1),
        name="out_ffn",
    )(x1, of, oa, gf, ga, wout, g2, wg, wu, wd, gfin)


def _row(g):
    return g.reshape(1, -1).astype(F32)


def _trunk(x, p, win):
    B, S, _ = x.shape
    x1 = _ffn1(x.reshape(B * S, D_MODEL), _row(p["ffn1_norm"]), p["ffn1_w_gate"], p["ffn1_w_up"],
               p["ffn1_w_down"], tm=FFN_TOKENS)
    a, c4_kvq, c16_kvq = _proj(x1, B, S, _row(p["mix_norm"]), win, tm=PROJ_TOKENS)
    C = max(1, S // (ATTN_BLOCKS_PER_STEP * Q_BLOCK))
    NB = max(1, ATTN_BLOCKS_PER_STEP * Q_BLOCK // S)
    oa = _attention(c4_kvq, c16_kvq, B, S, C, NB)
    tk = S // 2 if S // 2 <= DFT_MAX_DEPTH else DFT_MAX_DEPTH // 2
    of = _seq_dft(a, S, ts=DFT_ROWS, tk=tk, tn=min(DFT_TILE_ELEMS // tk, a.shape[-1]))
    y = _out_ffn(x1, of, oa, B, S, _row(p["fourier_out_norm"]), _row(p["attn_out_norm"]), p["w_out"],
                 _row(p["ffn2_norm"]), p["ffn2_w_gate"], p["ffn2_w_up"], p["ffn2_w_down"],
                 _row(p["final_norm"]), tm=FFN_TOKENS)
    return y.reshape(B, S, D_MODEL)


def kernel(x_prompt, x_sample, ffn1_norm, ffn1_w_gate, ffn1_w_up, ffn1_w_down, mix_norm, w_in, fourier_w,
           fourier_out_norm, attn_out_norm, w_out, ffn2_norm, ffn2_w_gate, ffn2_w_up, ffn2_w_down, final_norm):
    assert ffn1_w_gate.shape[0] == 1, "single-layer trunk"
    folded = _fold_fourier_weights(w_in[0, :, :D_FOURIER], fourier_w[0])
    win = jnp.concatenate([folded, w_in[0, :, D_FOURIER:].astype(BF16)], axis=1)
    p = dict(
        ffn1_norm=ffn1_norm[0], ffn1_w_gate=ffn1_w_gate[0].astype(BF16), ffn1_w_up=ffn1_w_up[0].astype(BF16),
        ffn1_w_down=ffn1_w_down[0].astype(BF16), mix_norm=mix_norm[0],
        fourier_out_norm=fourier_out_norm[0], attn_out_norm=attn_out_norm[0], w_out=w_out[0].astype(BF16),
        ffn2_norm=ffn2_norm[0], ffn2_w_gate=ffn2_w_gate[0].astype(BF16), ffn2_w_up=ffn2_w_up[0].astype(BF16),
        ffn2_w_down=ffn2_w_down[0].astype(BF16), final_norm=final_norm)
    return (_trunk(x_prompt, p, win), _trunk(x_sample, p, win))
```

```python
import functools
import math

import jax
import jax.numpy as jnp
import numpy as np
from jax import lax
from jax.experimental import pallas as pl
from jax.experimental.pallas import tpu as pltpu

F32 = jnp.float32
BF16 = jnp.bfloat16

D_MODEL = 1024
D_FF = 2816
D_FOURIER = 256
N_FOURIER_GROUPS = 4
FOURIER_GROUP_DIM = 64
D_ATTN = 768
HEAD_DIM = 64
LANES = 128
N_HEAD_PAIRS = D_ATTN // LANES
D_PROJ = 2 * D_FOURIER + 3 * D_ATTN
RADIUS = 64
ROPE_THETA = 10000.0
RMS_EPS = 1e-6
MASK_VALUE = -1e30

MXU_COLS = 256
FF_CHUNK = MXU_COLS
FFN_TOKENS = 1024
PROJ_TOKENS = 1024
SPLIT_BUFFERS = 6
DFT_ROWS = 512
DFT_TILE_ELEMS = 1024 * 1024
DFT_MAX_DEPTH = 1024
TABLE_SPLIT = 32
Q_BLOCK = 128
ATTN_BLOCKS_PER_STEP = 64
KEY_WINDOW = Q_BLOCK + 2 * RADIUS
LOG2_E = math.log2(math.e)
VMEM_LIMIT = 56 * 1024 * 1024


def _params(n_grid_dims):
    return pltpu.CompilerParams(dimension_semantics=("arbitrary",) * n_grid_dims,
                                vmem_limit_bytes=VMEM_LIMIT)


def _resident(shape):
    nd = len(shape)
    return pl.BlockSpec(shape, lambda *_: (0,) * nd, pipeline_mode=pl.Buffered(1))


def _rms(x, g):
    inv = lax.rsqrt(jnp.mean(x * x, axis=-1, keepdims=True) + RMS_EPS)
    return x * inv * g


def _dot(a, b):
    return jnp.dot(a, b, preferred_element_type=F32)


def _swiglu(h, wg_ref, wu_ref, wd_ref):
    acc = None
    for c in range(D_FF // FF_CHUNK):
        sl = slice(c * FF_CHUNK, (c + 1) * FF_CHUNK)
        g = _dot(h, wg_ref[:, sl])
        u = _dot(h, wu_ref[:, sl])
        a = (g * jax.nn.sigmoid(g) * u).astype(BF16)
        part = _dot(a, wd_ref[sl, :])
        acc = part if acc is None else acc + part
    return acc


def _fold_kernel(winf_ref, cbd_ref, sbd_ref, wbd_ref, o_ref):
    hi = lax.Precision.HIGHEST
    scale = FOURIER_GROUP_DIM ** -0.5
    gc = jnp.dot(cbd_ref[...], wbd_ref[...], precision=hi, preferred_element_type=F32) * scale
    gs = jnp.dot(sbd_ref[...], wbd_ref[...], precision=hi, preferred_element_type=F32) * scale
    winf = winf_ref[...]
    o_ref[:, :D_FOURIER] = jnp.dot(winf, gc, precision=hi, preferred_element_type=F32).astype(BF16)
    o_ref[:, D_FOURIER:] = jnp.dot(winf, gs, precision=hi, preferred_element_type=F32).astype(BF16)


def _fold_fourier_weights(w_in_f, fourier_w):
    c = np.arange(FOURIER_GROUP_DIM)
    ang = 2.0 * np.pi * ((c[:, None] * c[None, :]) % FOURIER_GROUP_DIM) / FOURIER_GROUP_DIM
    eye = np.eye(N_FOURIER_GROUPS)
    cbd = jnp.asarray(np.kron(eye, np.cos(ang)), F32)
    sbd = jnp.asarray(np.kron(eye, np.sin(ang)), F32)
    wbd = jnp.zeros((D_FOURIER, D_FOURIER), F32)
    for g in range(N_FOURIER_GROUPS):
        s = g * FOURIER_GROUP_DIM
        wbd = lax.dynamic_update_slice(wbd, fourier_w[g].astype(F32), (s, s))
    return pl.pallas_call(
        _fold_kernel,
        out_shape=jax.ShapeDtypeStruct((D_MODEL, 2 * D_FOURIER), BF16),
        name="fold_fourier",
    )(w_in_f, cbd, sbd, wbd)


def _ffn1_kernel(x_ref, g1_ref, wg_ref, wu_ref, wd_ref, x1_ref):
    x = x_ref[...]
    h = _rms(x, g1_ref[...]).astype(BF16)
    x1_ref[...] = x + 0.5 * _swiglu(h, wg_ref, wu_ref, wd_ref)


def _ffn1(xf, g1, wg, wu, wd, tm):
    N = xf.shape[0]
    tok = pl.BlockSpec((tm, D_MODEL), lambda i: (i, 0))
    return pl.pallas_call(
        _ffn1_kernel,
        grid=(N // tm,),
        in_specs=[tok, _resident((1, D_MODEL)), _resident((D_MODEL, D_FF)), _resident((D_MODEL, D_FF)),
                  _resident((D_FF, D_MODEL))],
        out_specs=tok,
        out_shape=jax.ShapeDtypeStruct((N, D_MODEL), F32),
        compiler_params=_params(1),
        name="ffn1",
    )(xf, g1, wg, wu, wd)


def _rope(t, cos, sin_lo, sin_hi):
    return (t * cos + pltpu.roll(t, LANES - HEAD_DIM // 2, axis=1) * sin_lo
            + pltpu.roll(t, HEAD_DIM // 2, axis=1) * sin_hi)


def _proj_kernel(x1_ref, gm_ref, win_ref, cos_ref, slo_ref, shi_ref,
                 a_ref, o4_ref, o16_ref,
                 nat_ref, cls4_ref, pa_ref):
    tm = x1_ref.shape[0] // 2
    for half in range(2):
        _proj_half(half, tm, x1_ref, gm_ref, win_ref, cos_ref, slo_ref, shi_ref, a_ref, o4_ref, o16_ref,
                   nat_ref, cls4_ref, pa_ref)


def _proj_half(half, tm, x1_ref, gm_ref, win_ref, cos_ref, slo_ref, shi_ref, a_ref, o4_ref, o16_ref,
               nat_ref, cls4_ref, pa_ref):
    rows = slice(half * tm, (half + 1) * tm)
    h2 = _rms(x1_ref[rows, :], gm_ref[...]).astype(BF16)

    def part(n):
        return slice(half * n, (half + 1) * n)

    def emit(t, hp, slot):
        e = (hp * 3 + slot + (nat_ref.shape[0] // 2) * half) % nat_ref.shape[0]
        nat_ref[e] = t
        for r4 in range(4):
            c4 = nat_ref[e, pl.ds(r4, tm // 4, stride=4), :]
            o4_ref[hp, slot, r4, part(tm // 4), :] = c4.astype(BF16)
            cls4_ref[e, r4] = c4
            for j in range(4):
                o16_ref[hp, slot, r4 + 4 * j, part(tm // 16), :] = (
                    cls4_ref[e, r4, pl.ds(j, tm // 16, stride=4), :].astype(BF16))

    cos = cos_ref[rows, :]
    slo = slo_ref[rows, :]
    shi = shi_ref[rows, :]
    base = 2 * D_FOURIER
    for j in range(D_ATTN // MXU_COLS):
        c0 = j * MXU_COLS
        qq = _dot(h2, win_ref[:, base + c0: base + c0 + MXU_COLS])
        kk = _dot(h2, win_ref[:, base + D_ATTN + c0: base + D_ATTN + c0 + MXU_COLS])
        vv = _dot(h2, win_ref[:, base + 2 * D_ATTN + c0: base + 2 * D_ATTN + c0 + MXU_COLS])
        for i in range(MXU_COLS // LANES):
            sl = slice(i * LANES, (i + 1) * LANES)
            hp = (MXU_COLS // LANES) * j + i
            emit(_rope(qq[:, sl], cos, slo, shi) * (HEAD_DIM ** -0.5 * LOG2_E), hp, Q_SLOT)
            emit(_rope(kk[:, sl], cos, slo, shi), hp, K_SLOT)
            emit(vv[:, sl], hp, V_SLOT)

    pa = _dot(h2, win_ref[:, :2 * D_FOURIER])
    tiles_per_part = D_FOURIER // LANES
    for j in range(2 * tiles_per_part):
        pa_ref[j, rows, :] = pa[:, j * LANES:(j + 1) * LANES]
    for par in range(2):
        for j in range(2 * tiles_per_part):
            lanes = slice((j % tiles_per_part) * LANES, (j % tiles_per_part + 1) * LANES)
            a_ref[par, j // tiles_per_part, part(tm // 2), lanes] = (
                pa_ref[j, pl.ds(half * tm + par, tm // 2, stride=2), :].astype(BF16))


def _rope_tables(S):
    inv_freq = ROPE_THETA ** (-jnp.arange(0, HEAD_DIM, 2, dtype=F32) / HEAD_DIM)
    freqs = jnp.arange(S, dtype=F32)[:, None] * inv_freq[None, :]
    cos, sin = jnp.cos(freqs), jnp.sin(freqs)
    zero = jnp.zeros_like(sin)
    cos_t = jnp.tile(cos, (1, LANES // (HEAD_DIM // 2)))
    sin_lo = jnp.tile(jnp.concatenate([-sin, zero], axis=1), (1, LANES // HEAD_DIM))
    sin_hi = jnp.tile(jnp.concatenate([zero, sin], axis=1), (1, LANES // HEAD_DIM))
    return cos_t, sin_lo, sin_hi


def _proj(x1, B, S, gm, win, tm):
    N = B * S
    nst = S // tm
    cos_t, sin_lo, sin_hi = _rope_tables(S)
    tok = pl.BlockSpec((tm, D_MODEL), lambda i: (i, 0))
    tab = pl.BlockSpec((tm, LANES), lambda i: (i % nst, 0))

    def cls(d):
        spec = pl.BlockSpec((None, N_HEAD_PAIRS, 3, d, tm // d, LANES),
                            lambda i: (i // nst, 0, 0, 0, i % nst, 0))
        return spec, jax.ShapeDtypeStruct((B, N_HEAD_PAIRS, 3, d, S // d, LANES), BF16)

    c4, c4_shape = cls(4)
    c16, c16_shape = cls(16)
    return pl.pallas_call(
        _proj_kernel,
        grid=(N // tm,),
        in_specs=[tok, _resident((1, D_MODEL)), _resident((D_MODEL, D_PROJ)), tab, tab, tab],
        out_specs=[pl.BlockSpec((2, 2, tm // 2, D_FOURIER), lambda i: (0, 0, i % nst, i // nst)),
                   c4, c16],
        out_shape=[jax.ShapeDtypeStruct((2, 2, S // 2, B * D_FOURIER), BF16), c4_shape, c16_shape],
        scratch_shapes=[pltpu.VMEM((SPLIT_BUFFERS, tm // 2, LANES), F32),
                        pltpu.VMEM((SPLIT_BUFFERS, 4, tm // 8, LANES), F32),
                        pltpu.VMEM((2 * D_FOURIER // LANES, tm, LANES), F32)],
        compiler_params=_params(1),
        name="proj",
    )(x1, gm, win, cos_t, sin_lo, sin_hi)


def _class_order(n):
    return np.arange(n).reshape(n // 4, 4).T.reshape(-1)


def _band_bias(offsets, W, extra=None):
    qi = np.arange(Q_BLOCK)[:, None]
    kj = np.arange(W)[None, :]
    tiles = []
    for t, off in enumerate(offsets):
        ok = np.abs(kj + off - qi) <= RADIUS
        if extra is not None:
            ok &= extra[t](kj)
        tiles.append(np.where(ok, 0.0, MASK_VALUE))
    return jnp.asarray(np.stack(tiles), F32)


K_SLOT, V_SLOT, Q_SLOT = 0, 1, 2


def _attn_kernel(c4_ref, c16_ref, *refs, S, C, NB):
    if C > 1:
        rows_ref, prev_ref, next_ref, *refs = refs
    else:
        rows_ref, prev_ref, next_ref = c4_ref, None, None
    b1_ref, b4_ref, b16_ref, o_ref, acc_ref, m_ref, l_ref = refs
    phase = pl.program_id(2)
    c = pl.program_id(3)
    lane = lax.broadcasted_iota(jnp.int32, (1, LANES), 1)
    first_head = lane < HEAD_DIM
    contract_last = (((1,), (1,)), ((), ()))

    def block_stats(q2, k2, v2, bias):
        v_ones = jnp.concatenate([v2, jnp.ones_like(v2)], axis=1)
        per_head = []
        for head_mask in (first_head, jnp.logical_not(first_head)):
            qh = jnp.where(head_mask, q2, jnp.zeros_like(q2))
            s = lax.dot_general(qh, k2, contract_last, preferred_element_type=F32)
            real = bias.shape[1]
            s = s[:, :real] + bias
            m = jnp.max(s, axis=-1, keepdims=True)
            p = jnp.exp2(s - m).astype(BF16)
            if real < k2.shape[0]:
                p = jnp.concatenate([p, jnp.zeros((Q_BLOCK, k2.shape[0] - real), BF16)], axis=1)
            pvl = _dot(p, v_ones)
            per_head.append((pvl[:, :LANES], jnp.broadcast_to(m, (Q_BLOCK, LANES)), pvl[:, LANES:]))
        return tuple(jnp.where(first_head, a, b) for a, b in zip(*per_head))

    def state(ref, bb, rows):
        if isinstance(rows, (list, tuple)):
            return jnp.concatenate([ref[bb, r, :] for r in rows], axis=0)
        return ref[bb, rows, :]

    def merge(bb, rows, pv, m, l, mode):
        if mode == "first":
            acc_ref[bb, rows, :] = pv
            m_ref[bb, rows, :] = m
            l_ref[bb, rows, :] = l
            return None
        m_old = state(m_ref, bb, rows)
        m_new = jnp.maximum(m_old, m)
        a_old = jnp.exp2(m_old - m_new)
        a_blk = jnp.exp2(m - m_new)
        acc = a_old * state(acc_ref, bb, rows) + a_blk * pv
        den = a_old * state(l_ref, bb, rows) + a_blk * l
        if mode == "last":
            return acc / den
        acc_ref[bb, rows, :] = acc
        m_ref[bb, rows, :] = m_new
        l_ref[bb, rows, :] = den
        return None

    def strided_phase(d, cls_ref, bias_ref, mode):
        L = S // d
        nblk = L // Q_BLOCK
        W = min(KEY_WINDOW, L)
        cpc = d // C
        for bb in range(NB):
            for rl in range(cpc):
                for i in range(nblk):
                    a = i * Q_BLOCK
                    ws = min(max(a - RADIUS, 0), L - W)
                    typ = 0 if i == 0 else (2 if i == nblk - 1 else 1)
                    k2 = cls_ref[bb, K_SLOT, rl, ws:ws + W, :]
                    v2 = cls_ref[bb, V_SLOT, rl, ws:ws + W, :]
                    bias = bias_ref[typ]
                    if W < KEY_WINDOW:
                        pad = jnp.zeros((KEY_WINDOW - W, LANES), BF16)
                        k2 = jnp.concatenate([k2, pad], axis=0)
                        v2 = jnp.concatenate([v2, pad], axis=0)
                    pv, m, l = block_stats(cls_ref[bb, Q_SLOT, rl, a:a + Q_BLOCK, :], k2, v2, bias)
                    cls = c * cpc + rl
                    if d == 4:
                        rows = pl.ds(pl.multiple_of(cls * L + a, Q_BLOCK), Q_BLOCK)
                    else:
                        rows = pl.ds((cls % 4) * (S // 4) + (d // 4) * a + cls // 4, Q_BLOCK, stride=4)
                    merge(bb, rows, pv, m, l, mode)

    def contiguous_phase():
        Sc = S // C
        nblk = Sc // Q_BLOCK
        Lc = Sc // 4
        halo = RADIUS // 4

        def gather(bb, slot, lo, hi):
            outside = jnp.zeros((halo, LANES), BF16)
            parts = []
            for r4 in range(4):
                if lo < 0:
                    parts.append(outside if prev_ref is None else prev_ref[bb, slot, r4])
                parts.append(rows_ref[bb, slot, r4, max(lo, 0) // 4:min(hi, Sc) // 4, :])
                if hi > Sc:
                    parts.append(outside if next_ref is None else next_ref[bb, slot, r4])
            return jnp.concatenate(parts, axis=0)

        assert Lc % halo == 0
        for bb in range(NB):
            for i in range(nblk):
                a = i * Q_BLOCK
                typ = 0
                if i == 0:
                    typ = jnp.where(c == 0, 1, typ)
                if i == nblk - 1:
                    typ = jnp.where(c == C - 1, 2, typ)
                pv, m, l = block_stats(gather(bb, Q_SLOT, a, a + Q_BLOCK),
                                       gather(bb, K_SLOT, a - RADIUS, a + Q_BLOCK + RADIUS),
                                       gather(bb, V_SLOT, a - RADIUS, a + Q_BLOCK + RADIUS), b1_ref[typ])
                pos = c * Sc + a
                part = Q_BLOCK // 4
                rows = [pl.ds(pl.multiple_of(r4 * (S // 4) + pos // 4, part), part) for r4 in range(4)]
                out = merge(bb, rows, pv, m, l, "last")
                for r4 in range(4):
                    o_ref[bb, pl.ds(pos + r4, part, stride=4), :] = out[r4 * part:(r4 + 1) * part, :]

    @pl.when(phase == 0)
    def _():
        strided_phase(4, c4_ref, b4_ref, "first")

    @pl.when(phase == 1)
    def _():
        strided_phase(16, c16_ref, b16_ref, "mid")

    @pl.when(phase == 2)
    def _():
        contiguous_phase()


def _attention(c4_kvq, c16_kvq, B, S, C, NB):
    Sc = S // C
    assert Sc % Q_BLOCK == 0 and 4 % C == 0 and B % NB == 0
    halo = RADIUS // 4
    Lc = Sc // 4
    halo_blocks = S // 4 // halo

    def chunk_of(own_phase, p, c):
        return jnp.where(p < own_phase, 0, jnp.where(p > own_phase, C - 1, c))

    def cls_spec(d, own_phase):
        L = S // d
        assert L % Q_BLOCK == 0 and (L == Q_BLOCK or L >= KEY_WINDOW)
        return pl.BlockSpec((NB, None, 3, d // C, L, LANES),
                            lambda b, h, p, c: (b, h, 0, chunk_of(own_phase, p, c), 0, 0))

    rows = pl.BlockSpec((NB, None, 3, 4, Lc, LANES), lambda b, h, p, c: (b, h, 0, 0, chunk_of(2, p, c), 0))
    prev = pl.BlockSpec((NB, None, 2, 4, halo, LANES),
                        lambda b, h, p, c: (b, h, 0, 0, jnp.maximum(chunk_of(2, p, c) * (Lc // halo) - 1, 0), 0))
    nxt = pl.BlockSpec((NB, None, 2, 4, halo, LANES),
                       lambda b, h, p, c: (b, h, 0, 0, jnp.minimum((chunk_of(2, p, c) + 1) * (Lc // halo),
                                                                    halo_blocks - 1), 0))
    c4, c16 = cls_spec(4, 0), cls_spec(16, 1)
    W4, W16 = min(KEY_WINDOW, S // 4), min(KEY_WINDOW, S // 16)
    clamped = (0, -RADIUS, -2 * RADIUS)
    b1 = _band_bias((-RADIUS,) * 3, KEY_WINDOW,
                    extra=(lambda kj: kj >= 0, lambda kj: kj >= RADIUS, lambda kj: kj < KEY_WINDOW - RADIUS))
    b1 = b1[:, _class_order(Q_BLOCK), :][:, :, _class_order(KEY_WINDOW)]
    b4, b16 = _band_bias(clamped, W4), _band_bias(clamped, W16)
    return pl.pallas_call(
        functools.partial(_attn_kernel, S=S, C=C, NB=NB),
        grid=(B // NB, N_HEAD_PAIRS, 3, C),
        in_specs=[c4, c16] + ([rows, prev, nxt] if C > 1 else [])
        + [_resident(b1.shape), _resident(b4.shape), _resident(b16.shape)],
        out_specs=pl.BlockSpec((NB, None, S, LANES), lambda b, h, p, c: (b, h, 0, 0)),
        out_shape=jax.ShapeDtypeStruct((B, N_HEAD_PAIRS, S, LANES), F32),
        scratch_shapes=[pltpu.VMEM((NB, S, LANES), F32)] * 3,
        compiler_params=_params(4),
        name="attn",
    )(c4_kvq, c16_kvq, *([c4_kvq] * 3 if C > 1 else []), b1, b4, b16)


def _dft_kernel(bc_ref, nbs_ref, r1_ref, r2_ref, a_ref, o_ref):
    t = pl.program_id(2)

    @pl.when(t == 0)
    def _():
        o_ref[...] = jnp.zeros_like(o_ref)

    for par in range(2):
        bc, nbs = bc_ref[par], nbs_ref[par]
        r1, r2 = r1_ref[par], r2_ref[par]
        t_cos = (bc * r1 + nbs * r2).astype(BF16)
        t_nsin = (nbs * r1 - bc * r2).astype(BF16)
        o_ref[par] += _dot(t_cos, a_ref[par, 0]) + _dot(t_nsin, a_ref[par, 1])

    @pl.when(t == pl.num_programs(2) - 1)
    def _():
        f0, f1 = o_ref[0], o_ref[1]
        o_ref[0] = f0 + f1
        o_ref[1] = f0 - f1


def _seq_dft(a, S, ts, tk, tn):
    ncols = a.shape[-1]
    H = S // 2
    t = 2 * jnp.arange(H, dtype=jnp.int32)[None, None, :] + jnp.arange(2, dtype=jnp.int32)[:, None, None]
    w = 2.0 * math.pi / S
    ang_r = ((jnp.arange(0, H, ts, dtype=jnp.int32)[None, :, None] * t) % S).astype(F32) * w
    lo = TABLE_SPLIT
    ang_hi = ((jnp.arange(0, ts, lo, dtype=jnp.int32)[None, :, None] * t) % S).astype(F32) * w
    ang_lo = ((jnp.arange(lo, dtype=jnp.int32)[None, :, None] * t) % S).astype(F32) * w
    c_hi, s_hi = jnp.cos(ang_hi)[:, :, None, :], jnp.sin(ang_hi)[:, :, None, :]
    c_lo, s_lo = jnp.cos(ang_lo)[:, None, :, :], jnp.sin(ang_lo)[:, None, :, :]
    bc = (c_hi * c_lo - s_hi * s_lo).reshape(2, ts, H)
    nbs = -(s_hi * c_lo + c_hi * s_lo).reshape(2, ts, H)
    scale = S ** -0.5
    r1 = (jnp.cos(ang_r) * scale).reshape(2, H // ts, 1, H)
    r2 = (jnp.sin(ang_r) * scale).reshape(2, H // ts, 1, H)
    btab = pl.BlockSpec((2, ts, tk), lambda s, n, t: (0, 0, t))
    rtab = pl.BlockSpec((2, None, 1, tk), lambda s, n, t: (0, s, 0, t))
    out = pl.pallas_call(
        _dft_kernel,
        grid=(H // ts, ncols // tn, H // tk),
        in_specs=[btab, btab, rtab, rtab, pl.BlockSpec((2, 2, tk, tn), lambda s, n, t: (0, 0, t, n))],
        out_specs=pl.BlockSpec((2, ts, tn), lambda s, n, t: (0, s, n)),
        out_shape=jax.ShapeDtypeStruct((2, H, ncols), F32),
        compiler_params=_params(3),
        name="seq_dft",
    )(bc, nbs, r1, r2, a)
    return out.reshape(S, ncols)


def _out_ffn_kernel(x1_ref, of_ref, oa_ref, gf_ref, ga_ref, wout_ref, g2_ref, wg_ref, wu_ref, wd_ref,
                    gfin_ref, y_ref):
    tm = x1_ref.shape[0]
    for half in range(2):
        r = slice(half * tm // 2, (half + 1) * tm // 2)
        oa = jnp.concatenate([oa_ref[hp, r, :] for hp in range(N_HEAD_PAIRS)], axis=1)
        na = _rms(oa, ga_ref[...]).astype(BF16)
        nf = _rms(of_ref[r, :], gf_ref[...]).astype(BF16)
        x2 = x1_ref[r, :] + _dot(nf, wout_ref[:D_FOURIER, :]) + _dot(na, wout_ref[D_FOURIER:, :])
        h = _rms(x2, g2_ref[...]).astype(BF16)
        x3 = x2 + 0.5 * _swiglu(h, wg_ref, wu_ref, wd_ref)
        y_ref[r, :] = _rms(x3, gfin_ref[...])


def _out_ffn(x1, of, oa, B, S, gf, ga, wout, g2, wg, wu, wd, gfin, tm):
    N = B * S
    nst = S // tm
    tok = pl.BlockSpec((tm, D_MODEL), lambda i: (i, 0))
    att = pl.BlockSpec((None, N_HEAD_PAIRS, tm, LANES), lambda i: (i // nst, 0, i % nst, 0))
    return pl.pallas_call(
        _out_ffn_kernel,
        grid=(N // tm,),
        in_specs=[tok, pl.BlockSpec((tm, D_FOURIER), lambda i: (i % nst, i // nst)), att,
                  _resident((1, D_FOURIER)), _resident((1, D_ATTN)), _resident((D_MODEL, D_MODEL)),
                  _resident((1, D_MODEL)), _resident((D_MODEL, D_FF)), _resident((D_MODEL, D_FF)),
                  _resident((D_FF, D_MODEL)), _resident((1, D_MODEL))],
        out_specs=tok,
        out_shape=jax.ShapeDtypeStruct((N, D_MODEL), F32),
        compiler_params=_params(1),
        name="out_ffn",
    )(x1, of, oa, gf, ga, wout, g2, wg, wu, wd, gfin)


def _row(g):
    return g.reshape(1, -1).astype(F32)


def _trunk(x, p, win):
    B, S, _ = x.shape
    x1 = _ffn1(x.reshape(B * S, D_MODEL), _row(p["ffn1_norm"]), p["ffn1_w_gate"], p["ffn1_w_up"],
               p["ffn1_w_down"], tm=FFN_TOKENS)
    a, c4_kvq, c16_kvq = _proj(x1, B, S, _row(p["mix_norm"]), win, tm=PROJ_TOKENS)
    C = max(1, S // (ATTN_BLOCKS_PER_STEP * Q_BLOCK))
    NB = max(1, ATTN_BLOCKS_PER_STEP * Q_BLOCK // S)
    oa = _attention(c4_kvq, c16_kvq, B, S, C, NB)
    tk = S // 2 if S // 2 <= DFT_MAX_DEPTH else DFT_MAX_DEPTH // 2
    of = _seq_dft(a, S, ts=DFT_ROWS, tk=tk, tn=min(DFT_TILE_ELEMS // tk, a.shape[-1]))
    y = _out_ffn(x1, of, oa, B, S, _row(p["fourier_out_norm"]), _row(p["attn_out_norm"]), p["w_out"],
                 _row(p["ffn2_norm"]), p["ffn2_w_gate"], p["ffn2_w_up"], p["ffn2_w_down"],
                 _row(p["final_norm"]), tm=FFN_TOKENS)
    return y.reshape(B, S, D_MODEL)


def kernel(x_prompt, x_sample, ffn1_norm, ffn1_w_gate, ffn1_w_up, ffn1_w_down, mix_norm, w_in, fourier_w,
           fourier_out_norm, attn_out_norm, w_out, ffn2_norm, ffn2_w_gate, ffn2_w_up, ffn2_w_down, final_norm):
    assert ffn1_w_gate.shape[0] == 1, "single-layer trunk"
    folded = _fold_fourier_weights(w_in[0, :, :D_FOURIER], fourier_w[0])
    win = jnp.concatenate([folded, w_in[0, :, D_FOURIER:].astype(BF16)], axis=1)
    p = dict(
        ffn1_norm=ffn1_norm[0], ffn1_w_gate=ffn1_w_gate[0].astype(BF16), ffn1_w_up=ffn1_w_up[0].astype(BF16),
        ffn1_w_down=ffn1_w_down[0].astype(BF16), mix_norm=mix_norm[0],
        fourier_out_norm=fourier_out_norm[0], attn_out_norm=attn_out_norm[0], w_out=w_out[0].astype(BF16),
        ffn2_norm=ffn2_norm[0], ffn2_w_gate=ffn2_w_gate[0].astype(BF16), ffn2_w_up=ffn2_w_up[0].astype(BF16),
        ffn2_w_down=ffn2_w_down[0].astype(BF16), final_norm=final_norm)
    return (_trunk(x_prompt, p, win), _trunk(x_sample, p, win))
```

```python
import functools
import math

import jax
import jax.numpy as jnp
import numpy as np
from jax import lax
from jax.experimental import pallas as pl
from jax.experimental.pallas import tpu as pltpu

F32 = jnp.float32
BF16 = jnp.bfloat16

D_MODEL = 1024
D_FF = 2816
D_FOURIER = 256
N_FOURIER_GROUPS = 4
FOURIER_GROUP_DIM = 64
D_ATTN = 768
HEAD_DIM = 64
LANES = 128
N_HEAD_PAIRS = D_ATTN // LANES
D_PROJ = 2 * D_FOURIER + 3 * D_ATTN
RADIUS = 64
ROPE_THETA = 10000.0
RMS_EPS = 1e-6
MASK_VALUE = -1e30

MXU_COLS = 256
FF_CHUNK = MXU_COLS
FFN_TOKENS = 1024
PROJ_TOKENS = 1024
SPLIT_BUFFERS = 6
DFT_ROWS = 512
DFT_TILE_ELEMS = 1024 * 1024
DFT_MAX_DEPTH = 1024
TABLE_SPLIT = 32
Q_BLOCK = 128
ATTN_BLOCKS_PER_STEP = 64
KEY_WINDOW = Q_BLOCK + 2 * RADIUS
LOG2_E = math.log2(math.e)
VMEM_LIMIT = 56 * 1024 * 1024


def _params(n_grid_dims):
    return pltpu.CompilerParams(dimension_semantics=("arbitrary",) * n_grid_dims,
                                vmem_limit_bytes=VMEM_LIMIT)


def _resident(shape):
    nd = len(shape)
    return pl.BlockSpec(shape, lambda *_: (0,) * nd, pipeline_mode=pl.Buffered(1))


def _rms(x, g):
    inv = lax.rsqrt(jnp.mean(x * x, axis=-1, keepdims=True) + RMS_EPS)
    return x * inv * g


def _dot(a, b):
    return jnp.dot(a, b, preferred_element_type=F32)


def _swiglu(h, wg_ref, wu_ref, wd_ref):
    acc = None
    for c in range(D_FF // FF_CHUNK):
        sl = slice(c * FF_CHUNK, (c + 1) * FF_CHUNK)
        g = _dot(h, wg_ref[:, sl])
        u = _dot(h, wu_ref[:, sl])
        a = (g * jax.nn.sigmoid(g) * u).astype(BF16)
        part = _dot(a, wd_ref[sl, :])
        acc = part if acc is None else acc + part
    return acc


def _fold_kernel(winf_ref, cbd_ref, sbd_ref, wbd_ref, o_ref):
    hi = lax.Precision.HIGHEST
    scale = FOURIER_GROUP_DIM ** -0.5
    gc = jnp.dot(cbd_ref[...], wbd_ref[...], precision=hi, preferred_element_type=F32) * scale
    gs = jnp.dot(sbd_ref[...], wbd_ref[...], precision=hi, preferred_element_type=F32) * scale
    winf = winf_ref[...]
    o_ref[:, :D_FOURIER] = jnp.dot(winf, gc, precision=hi, preferred_element_type=F32).astype(BF16)
    o_ref[:, D_FOURIER:] = jnp.dot(winf, gs, precision=hi, preferred_element_type=F32).astype(BF16)


def _fold_fourier_weights(w_in_f, fourier_w):
    c = np.arange(FOURIER_GROUP_DIM)
    ang = 2.0 * np.pi * ((c[:, None] * c[None, :]) % FOURIER_GROUP_DIM) / FOURIER_GROUP_DIM
    eye = np.eye(N_FOURIER_GROUPS)
    cbd = jnp.asarray(np.kron(eye, np.cos(ang)), F32)
    sbd = jnp.asarray(np.kron(eye, np.sin(ang)), F32)
    wbd = jnp.zeros((D_FOURIER, D_FOURIER), F32)
    for g in range(N_FOURIER_GROUPS):
        s = g * FOURIER_GROUP_DIM
        wbd = lax.dynamic_update_slice(wbd, fourier_w[g].astype(F32), (s, s))
    return pl.pallas_call(
        _fold_kernel,
        out_shape=jax.ShapeDtypeStruct((D_MODEL, 2 * D_FOURIER), BF16),
        name="fold_fourier",
    )(w_in_f, cbd, sbd, wbd)


def _ffn1_kernel(x_ref, g1_ref, wg_ref, wu_ref, wd_ref, x1_ref):
    x = x_ref[...]
    h = _rms(x, g1_ref[...]).astype(BF16)
    x1_ref[...] = x + 0.5 * _swiglu(h, wg_ref, wu_ref, wd_ref)


def _ffn1(xf, g1, wg, wu, wd, tm):
    N = xf.shape[0]
    tok = pl.BlockSpec((tm, D_MODEL), lambda i: (i, 0))
    return pl.pallas_call(
        _ffn1_kernel,
        grid=(N // tm,),
        in_specs=[tok, _resident((1, D_MODEL)), _resident((D_MODEL, D_FF)), _resident((D_MODEL, D_FF)),
                  _resident((D_FF, D_MODEL))],
        out_specs=tok,
        out_shape=jax.ShapeDtypeStruct((N, D_MODEL), F32),
        compiler_params=_params(1),
        name="ffn1",
    )(xf, g1, wg, wu, wd)


def _rope(t, cos, sin_lo, sin_hi):
    return (t * cos + pltpu.roll(t, LANES - HEAD_DIM // 2, axis=1) * sin_lo
            + pltpu.roll(t, HEAD_DIM // 2, axis=1) * sin_hi)


def _proj_kernel(x1_ref, gm_ref, win_ref, cos_ref, slo_ref, shi_ref,
                 a_ref, o4_ref, o16_ref,
                 nat_ref, cls4_ref, pa_ref):
    tm = x1_ref.shape[0]
    h2 = _rms(x1_ref[...], gm_ref[...]).astype(BF16)

    def emit(t, hp, slot):
        e = (hp * 3 + slot) % nat_ref.shape[0]
        nat_ref[e] = t
        for r4 in range(4):
            c4 = nat_ref[e, pl.ds(r4, tm // 4, stride=4), :]
            o4_ref[hp, slot, r4] = c4.astype(BF16)
            cls4_ref[e, r4] = c4
            for j in range(4):
                o16_ref[hp, slot, r4 + 4 * j] = cls4_ref[e, r4, pl.ds(j, tm // 16, stride=4), :].astype(BF16)

    cos = cos_ref[...]
    slo = slo_ref[...]
    shi = shi_ref[...]
    base = 2 * D_FOURIER
    for j in range(D_ATTN // MXU_COLS):
        c0 = j * MXU_COLS
        qq = _dot(h2, win_ref[:, base + c0: base + c0 + MXU_COLS])
        kk = _dot(h2, win_ref[:, base + D_ATTN + c0: base + D_ATTN + c0 + MXU_COLS])
        vv = _dot(h2, win_ref[:, base + 2 * D_ATTN + c0: base + 2 * D_ATTN + c0 + MXU_COLS])
        for i in range(MXU_COLS // LANES):
            sl = slice(i * LANES, (i + 1) * LANES)
            hp = (MXU_COLS // LANES) * j + i
            emit(_rope(qq[:, sl], cos, slo, shi) * (HEAD_DIM ** -0.5 * LOG2_E), hp, Q_SLOT)
            emit(_rope(kk[:, sl], cos, slo, shi), hp, K_SLOT)
            emit(vv[:, sl], hp, V_SLOT)

    pa = _dot(h2, win_ref[:, :2 * D_FOURIER])
    tiles_per_part = D_FOURIER // LANES
    for j in range(2 * tiles_per_part):
        pa_ref[j] = pa[:, j * LANES:(j + 1) * LANES]
    for par in range(2):
        for j in range(2 * tiles_per_part):
            lanes = slice((j % tiles_per_part) * LANES, (j % tiles_per_part + 1) * LANES)
            a_ref[par, j // tiles_per_part, :, lanes] = pa_ref[j, pl.ds(par, tm // 2, stride=2), :].astype(BF16)


def _rope_tables(S):
    inv_freq = ROPE_THETA ** (-jnp.arange(0, HEAD_DIM, 2, dtype=F32) / HEAD_DIM)
    freqs = jnp.arange(S, dtype=F32)[:, None] * inv_freq[None, :]
    cos, sin = jnp.cos(freqs), jnp.sin(freqs)
    zero = jnp.zeros_like(sin)
    cos_t = jnp.tile(cos, (1, LANES // (HEAD_DIM // 2)))
    sin_lo = jnp.tile(jnp.concatenate([-sin, zero], axis=1), (1, LANES // HEAD_DIM))
    sin_hi = jnp.tile(jnp.concatenate([zero, sin], axis=1), (1, LANES // HEAD_DIM))
    return cos_t, sin_lo, sin_hi


def _proj(x1, B, S, gm, win, tm):
    N = B * S
    nst = S // tm
    cos_t, sin_lo, sin_hi = _rope_tables(S)
    tok = pl.BlockSpec((tm, D_MODEL), lambda i: (i, 0))
    tab = pl.BlockSpec((tm, LANES), lambda i: (i % nst, 0))

    def cls(d):
        spec = pl.BlockSpec((None, N_HEAD_PAIRS, 3, d, tm // d, LANES),
                            lambda i: (i // nst, 0, 0, 0, i % nst, 0))
        return spec, jax.ShapeDtypeStruct((B, N_HEAD_PAIRS, 3, d, S // d, LANES), BF16)

    c4, c4_shape = cls(4)
    c16, c16_shape = cls(16)
    vmem = pl.BlockSpec(memory_space=pltpu.VMEM)
    hbm = pl.BlockSpec(memory_space=pl.ANY)
    tok_ahead = pl.BlockSpec((tm, D_MODEL), lambda i: (i, 0), pipeline_mode=pl.Buffered(3))
    a_spec = pl.BlockSpec((2, 2, tm // 2, D_FOURIER), lambda i: (0, 0, i % nst, i // nst))

    def outer(x1_hbm, gm_ref, win_ref, cos_hbm, slo_hbm, shi_hbm, a_hbm, o4_hbm, o16_hbm, nat_ref, cls4_ref, pa_ref):
        def body(x1_ref, cos_ref, slo_ref, shi_ref, a_ref, o4_ref, o16_ref):
            _proj_kernel(x1_ref, gm_ref, win_ref, cos_ref, slo_ref, shi_ref, a_ref, o4_ref, o16_ref,
                         nat_ref, cls4_ref, pa_ref)

        pltpu.emit_pipeline(body, grid=(N // tm,), in_specs=[tok_ahead, tab, tab, tab],
                            out_specs=[a_spec, c4, c16])(x1_hbm, cos_hbm, slo_hbm, shi_hbm, a_hbm, o4_hbm, o16_hbm)

    return pl.pallas_call(
        outer,
        in_specs=[hbm, vmem, vmem, hbm, hbm, hbm],
        out_specs=[hbm, hbm, hbm],
        out_shape=[jax.ShapeDtypeStruct((2, 2, S // 2, B * D_FOURIER), BF16), c4_shape, c16_shape],
        scratch_shapes=[pltpu.VMEM((SPLIT_BUFFERS, tm, LANES), F32),
                        pltpu.VMEM((SPLIT_BUFFERS, 4, tm // 4, LANES), F32),
                        pltpu.VMEM((2 * D_FOURIER // LANES, tm, LANES), F32)],
        compiler_params=pltpu.CompilerParams(vmem_limit_bytes=VMEM_LIMIT),
        name="proj",
    )(x1, gm, win, cos_t, sin_lo, sin_hi)


def _class_order(n):
    return np.arange(n).reshape(n // 4, 4).T.reshape(-1)


def _band_bias(offsets, W, extra=None):
    qi = np.arange(Q_BLOCK)[:, None]
    kj = np.arange(W)[None, :]
    tiles = []
    for t, off in enumerate(offsets):
        ok = np.abs(kj + off - qi) <= RADIUS
        if extra is not None:
            ok &= extra[t](kj)
        tiles.append(np.where(ok, 0.0, MASK_VALUE))
    return jnp.asarray(np.stack(tiles), F32)


K_SLOT, V_SLOT, Q_SLOT = 0, 1, 2


def _attn_kernel(c4_ref, c16_ref, *refs, S, C, NB):
    if C > 1:
        rows_ref, prev_ref, next_ref, *refs = refs
    else:
        rows_ref, prev_ref, next_ref = c4_ref, None, None
    b1_ref, b4_ref, b16_ref, o_ref, acc_ref, m_ref, l_ref = refs
    phase = pl.program_id(2)
    c = pl.program_id(3)
    lane = lax.broadcasted_iota(jnp.int32, (1, LANES), 1)
    first_head = lane < HEAD_DIM
    contract_last = (((1,), (1,)), ((), ()))

    def block_stats(q2, k2, v2, bias):
        v_ones = jnp.concatenate([v2, jnp.ones_like(v2)], axis=1)
        per_head = []
        for head_mask in (first_head, jnp.logical_not(first_head)):
            qh = jnp.where(head_mask, q2, jnp.zeros_like(q2))
            s = lax.dot_general(qh, k2, contract_last, preferred_element_type=F32)
            real = bias.shape[1]
            s = s[:, :real] + bias
            m = jnp.max(s, axis=-1, keepdims=True)
            p = jnp.exp2(s - m).astype(BF16)
            if real < k2.shape[0]:
                p = jnp.concatenate([p, jnp.zeros((Q_BLOCK, k2.shape[0] - real), BF16)], axis=1)
            pvl = _dot(p, v_ones)
            per_head.append((pvl[:, :LANES], jnp.broadcast_to(m, (Q_BLOCK, LANES)), pvl[:, LANES:]))
        return tuple(jnp.where(first_head, a, b) for a, b in zip(*per_head))

    def state(ref, bb, rows):
        if isinstance(rows, (list, tuple)):
            return jnp.concatenate([ref[bb, r, :] for r in rows], axis=0)
        return ref[bb, rows, :]

    def merge(bb, rows, pv, m, l, mode):
        if mode == "first":
            acc_ref[bb, rows, :] = pv
            m_ref[bb, rows, :] = m
            l_ref[bb, rows, :] = l
            return None
        m_old = state(m_ref, bb, rows)
        m_new = jnp.maximum(m_old, m)
        a_old = jnp.exp2(m_old - m_new)
        a_blk = jnp.exp2(m - m_new)
        acc = a_old * state(acc_ref, bb, rows) + a_blk * pv
        den = a_old * state(l_ref, bb, rows) + a_blk * l
        if mode == "last":
            return acc / den
        acc_ref[bb, rows, :] = acc
        m_ref[bb, rows, :] = m_new
        l_ref[bb, rows, :] = den
        return None

    def strided_phase(d, cls_ref, bias_ref, mode):
        L = S // d
        nblk = L // Q_BLOCK
        W = min(KEY_WINDOW, L)
        cpc = d // C
        for bb in range(NB):
            for rl in range(cpc):
                for i in range(nblk):
                    a = i * Q_BLOCK
                    ws = min(max(a - RADIUS, 0), L - W)
                    typ = 0 if i == 0 else (2 if i == nblk - 1 else 1)
                    k2 = cls_ref[bb, K_SLOT, rl, ws:ws + W, :]
                    v2 = cls_ref[bb, V_SLOT, rl, ws:ws + W, :]
                    bias = bias_ref[typ]
                    if W < KEY_WINDOW:
                        pad = jnp.zeros((KEY_WINDOW - W, LANES), BF16)
                        k2 = jnp.concatenate([k2, pad], axis=0)
                        v2 = jnp.concatenate([v2, pad], axis=0)
                    pv, m, l = block_stats(cls_ref[bb, Q_SLOT, rl, a:a + Q_BLOCK, :], k2, v2, bias)
                    cls = c * cpc + rl
                    if d == 4:
                        rows = pl.ds(pl.multiple_of(cls * L + a, Q_BLOCK), Q_BLOCK)
                    else:
                        rows = pl.ds((cls % 4) * (S // 4) + (d // 4) * a + cls // 4, Q_BLOCK, stride=4)
                    merge(bb, rows, pv, m, l, mode)

    def contiguous_phase():
        Sc = S // C
        nblk = Sc // Q_BLOCK
        Lc = Sc // 4
        halo = RADIUS // 4

        def gather(bb, slot, lo, hi):
            outside = jnp.zeros((halo, LANES), BF16)
            parts = []
            for r4 in range(4):
                if lo < 0:
                    parts.append(outside if prev_ref is None else prev_ref[bb, slot, r4])
                parts.append(rows_ref[bb, slot, r4, max(lo, 0) // 4:min(hi, Sc) // 4, :])
                if hi > Sc:
                    parts.append(outside if next_ref is None else next_ref[bb, slot, r4])
            return jnp.concatenate(parts, axis=0)

        assert Lc % halo == 0
        for bb in range(NB):
            for i in range(nblk):
                a = i * Q_BLOCK
                typ = 0
                if i == 0:
                    typ = jnp.where(c == 0, 1, typ)
                if i == nblk - 1:
                    typ = jnp.where(c == C - 1, 2, typ)
                pv, m, l = block_stats(gather(bb, Q_SLOT, a, a + Q_BLOCK),
                                       gather(bb, K_SLOT, a - RADIUS, a + Q_BLOCK + RADIUS),
                                       gather(bb, V_SLOT, a - RADIUS, a + Q_BLOCK + RADIUS), b1_ref[typ])
                pos = c * Sc + a
                part = Q_BLOCK // 4
                rows = [pl.ds(pl.multiple_of(r4 * (S // 4) + pos // 4, part), part) for r4 in range(4)]
                out = merge(bb, rows, pv, m, l, "last")
                for r4 in range(4):
                    o_ref[bb, pl.ds(pos + r4, part, stride=4), :] = out[r4 * part:(r4 + 1) * part, :]

    @pl.when(phase == 0)
    def _():
        strided_phase(4, c4_ref, b4_ref, "first")

    @pl.when(phase == 1)
    def _():
        strided_phase(16, c16_ref, b16_ref, "mid")

    @pl.when(phase == 2)
    def _():
        contiguous_phase()


def _attention(c4_kvq, c16_kvq, B, S, C, NB):
    Sc = S // C
    assert Sc % Q_BLOCK == 0 and 4 % C == 0 and B % NB == 0
    halo = RADIUS // 4
    Lc = Sc // 4
    halo_blocks = S // 4 // halo

    def chunk_of(own_phase, p, c):
        return jnp.where(p < own_phase, 0, jnp.where(p > own_phase, C - 1, c))

    def cls_spec(d, own_phase):
        L = S // d
        assert L % Q_BLOCK == 0 and (L == Q_BLOCK or L >= KEY_WINDOW)
        return pl.BlockSpec((NB, None, 3, d // C, L, LANES),
                            lambda b, h, p, c: (b, h, 0, chunk_of(own_phase, p, c), 0, 0))

    rows = pl.BlockSpec((NB, None, 3, 4, Lc, LANES), lambda b, h, p, c: (b, h, 0, 0, chunk_of(2, p, c), 0))
    prev = pl.BlockSpec((NB, None, 2, 4, halo, LANES),
                        lambda b, h, p, c: (b, h, 0, 0, jnp.maximum(chunk_of(2, p, c) * (Lc // halo) - 1, 0), 0))
    nxt = pl.BlockSpec((NB, None, 2, 4, halo, LANES),
                       lambda b, h, p, c: (b, h, 0, 0, jnp.minimum((chunk_of(2, p, c) + 1) * (Lc // halo),
                                                                    halo_blocks - 1), 0))
    c4, c16 = cls_spec(4, 0), cls_spec(16, 1)
    W4, W16 = min(KEY_WINDOW, S // 4), min(KEY_WINDOW, S // 16)
    clamped = (0, -RADIUS, -2 * RADIUS)
    b1 = _band_bias((-RADIUS,) * 3, KEY_WINDOW,
                    extra=(lambda kj: kj >= 0, lambda kj: kj >= RADIUS, lambda kj: kj < KEY_WINDOW - RADIUS))
    b1 = b1[:, _class_order(Q_BLOCK), :][:, :, _class_order(KEY_WINDOW)]
    b4, b16 = _band_bias(clamped, W4), _band_bias(clamped, W16)
    return pl.pallas_call(
        functools.partial(_attn_kernel, S=S, C=C, NB=NB),
        grid=(B // NB, N_HEAD_PAIRS, 3, C),
        in_specs=[c4, c16] + ([rows, prev, nxt] if C > 1 else [])
        + [_resident(b1.shape), _resident(b4.shape), _resident(b16.shape)],
        out_specs=pl.BlockSpec((NB, None, S, LANES), lambda b, h, p, c: (b, h, 0, 0)),
        out_shape=jax.ShapeDtypeStruct((B, N_HEAD_PAIRS, S, LANES), F32),
        scratch_shapes=[pltpu.VMEM((NB, S, LANES), F32)] * 3,
        compiler_params=_params(4),
        name="attn",
    )(c4_kvq, c16_kvq, *([c4_kvq] * 3 if C > 1 else []), b1, b4, b16)


def _dft_kernel(bc_ref, nbs_ref, r1_ref, r2_ref, a_ref, o_ref):
    t = pl.program_id(2)

    @pl.when(t == 0)
    def _():
        o_ref[...] = jnp.zeros_like(o_ref)

    for par in range(2):
        bc, nbs = bc_ref[par], nbs_ref[par]
        r1, r2 = r1_ref[par], r2_ref[par]
        t_cos = (bc * r1 + nbs * r2).astype(BF16)
        t_nsin = (nbs * r1 - bc * r2).astype(BF16)
        o_ref[par] += _dot(t_cos, a_ref[par, 0]) + _dot(t_nsin, a_ref[par, 1])

    @pl.when(t == pl.num_programs(2) - 1)
    def _():
        f0, f1 = o_ref[0], o_ref[1]
        o_ref[0] = f0 + f1
        o_ref[1] = f0 - f1


def _seq_dft(a, S, ts, tk, tn):
    ncols = a.shape[-1]
    H = S // 2
    t = 2 * jnp.arange(H, dtype=jnp.int32)[None, None, :] + jnp.arange(2, dtype=jnp.int32)[:, None, None]
    w = 2.0 * math.pi / S
    ang_r = ((jnp.arange(0, H, ts, dtype=jnp.int32)[None, :, None] * t) % S).astype(F32) * w
    lo = TABLE_SPLIT
    ang_hi = ((jnp.arange(0, ts, lo, dtype=jnp.int32)[None, :, None] * t) % S).astype(F32) * w
    ang_lo = ((jnp.arange(lo, dtype=jnp.int32)[None, :, None] * t) % S).astype(F32) * w
    c_hi, s_hi = jnp.cos(ang_hi)[:, :, None, :], jnp.sin(ang_hi)[:, :, None, :]
    c_lo, s_lo = jnp.cos(ang_lo)[:, None, :, :], jnp.sin(ang_lo)[:, None, :, :]
    bc = (c_hi * c_lo - s_hi * s_lo).reshape(2, ts, H)
    nbs = -(s_hi * c_lo + c_hi * s_lo).reshape(2, ts, H)
    scale = S ** -0.5
    r1 = (jnp.cos(ang_r) * scale).reshape(2, H // ts, 1, H)
    r2 = (jnp.sin(ang_r) * scale).reshape(2, H // ts, 1, H)
    btab = pl.BlockSpec((2, ts, tk), lambda s, n, t: (0, 0, t))
    rtab = pl.BlockSpec((2, None, 1, tk), lambda s, n, t: (0, s, 0, t))
    out = pl.pallas_call(
        _dft_kernel,
        grid=(H // ts, ncols // tn, H // tk),
        in_specs=[btab, btab, rtab, rtab, pl.BlockSpec((2, 2, tk, tn), lambda s, n, t: (0, 0, t, n))],
        out_specs=pl.BlockSpec((2, ts, tn), lambda s, n, t: (0, s, n)),
        out_shape=jax.ShapeDtypeStruct((2, H, ncols), F32),
        compiler_params=_params(3),
        name="seq_dft",
    )(bc, nbs, r1, r2, a)
    return out.reshape(S, ncols)


def _out_ffn_kernel(x1_ref, of_ref, oa_ref, gf_ref, ga_ref, wout_ref, g2_ref, wg_ref, wu_ref, wd_ref,
                    gfin_ref, y_ref):
    tm = x1_ref.shape[0]
    for half in range(2):
        r = slice(half * tm // 2, (half + 1) * tm // 2)
        oa = jnp.concatenate([oa_ref[hp, r, :] for hp in range(N_HEAD_PAIRS)], axis=1)
        na = _rms(oa, ga_ref[...]).astype(BF16)
        nf = _rms(of_ref[r, :], gf_ref[...]).astype(BF16)
        x2 = x1_ref[r, :] + _dot(nf, wout_ref[:D_FOURIER, :]) + _dot(na, wout_ref[D_FOURIER:, :])
        h = _rms(x2, g2_ref[...]).astype(BF16)
        x3 = x2 + 0.5 * _swiglu(h, wg_ref, wu_ref, wd_ref)
        y_ref[r, :] = _rms(x3, gfin_ref[...])


def _out_ffn(x1, of, oa, B, S, gf, ga, wout, g2, wg, wu, wd, gfin, tm):
    N = B * S
    nst = S // tm
    tok = pl.BlockSpec((tm, D_MODEL), lambda i: (i, 0))
    att = pl.BlockSpec((None, N_HEAD_PAIRS, tm, LANES), lambda i: (i // nst, 0, i % nst, 0))
    return pl.pallas_call(
        _out_ffn_kernel,
        grid=(N // tm,),
        in_specs=[tok, pl.BlockSpec((tm, D_FOURIER), lambda i: (i % nst, i // nst)), att,
                  _resident((1, D_FOURIER)), _resident((1, D_ATTN)), _resident((D_MODEL, D_MODEL)),
                  _resident((1, D_MODEL)), _resident((D_MODEL, D_FF)), _resident((D_MODEL, D_FF)),
                  _resident((D_FF, D_MODEL)), _resident((1, D_MODEL))],
        out_specs=tok,
        out_shape=jax.ShapeDtypeStruct((N, D_MODEL), F32),
        compiler_params=_params(1),
        name="out_ffn",
    )(x1, of, oa, gf, ga, wout, g2, wg, wu, wd, gfin)


def _row(g):
    return g.reshape(1, -1).astype(F32)


def _trunk(x, p, win):
    B, S, _ = x.shape
    x1 = _ffn1(x.reshape(B * S, D_MODEL), _row(p["ffn1_norm"]), p["ffn1_w_gate"], p["ffn1_w_up"],
               p["ffn1_w_down"], tm=FFN_TOKENS)
    a, c4_kvq, c16_kvq = _proj(x1, B, S, _row(p["mix_norm"]), win, tm=PROJ_TOKENS)
    C = max(1, S // (ATTN_BLOCKS_PER_STEP * Q_BLOCK))
    NB = max(1, ATTN_BLOCKS_PER_STEP * Q_BLOCK // S)
    oa = _attention(c4_kvq, c16_kvq, B, S, C, NB)
    tk = S // 2 if S // 2 <= DFT_MAX_DEPTH else DFT_MAX_DEPTH // 2
    of = _seq_dft(a, S, ts=DFT_ROWS, tk=tk, tn=min(DFT_TILE_ELEMS // tk, a.shape[-1]))
    y = _out_ffn(x1, of, oa, B, S, _row(p["fourier_out_norm"]), _row(p["attn_out_norm"]), p["w_out"],
                 _row(p["ffn2_norm"]), p["ffn2_w_gate"], p["ffn2_w_up"], p["ffn2_w_down"],
                 _row(p["final_norm"]), tm=FFN_TOKENS)
    return y.reshape(B, S, D_MODEL)


def kernel(x_prompt, x_sample, ffn1_norm, ffn1_w_gate, ffn1_w_up, ffn1_w_down, mix_norm, w_in, fourier_w,
           fourier_out_norm, attn_out_norm, w_out, ffn2_norm, ffn2_w_gate, ffn2_w_up, ffn2_w_down, final_norm):
    assert ffn1_w_gate.shape[0] == 1, "single-layer trunk"
    folded = _fold_fourier_weights(w_in[0, :, :D_FOURIER], fourier_w[0])
    win = jnp.concatenate([folded, w_in[0, :, D_FOURIER:].astype(BF16)], axis=1)
    p = dict(
        ffn1_norm=ffn1_norm[0], ffn1_w_gate=ffn1_w_gate[0].astype(BF16), ffn1_w_up=ffn1_w_up[0].astype(BF16),
        ffn1_w_down=ffn1_w_down[0].astype(BF16), mix_norm=mix_norm[0],
        fourier_out_norm=fourier_out_norm[0], attn_out_norm=attn_out_norm[0], w_out=w_out[0].astype(BF16),
        ffn2_norm=ffn2_norm[0], ffn2_w_gate=ffn2_w_gate[0].astype(BF16), ffn2_w_up=ffn2_w_up[0].astype(BF16),
        ffn2_w_down=ffn2_w_down[0].astype(BF16), final_norm=final_norm)
    return (_trunk(x_prompt, p, win), _trunk(x_sample, p, win))
```
